```python
import functools
import jax, jax.numpy as jnp
from jax import lax
import numpy as np

D_MODEL = 1024
BATCH = 2
SEQ = 8192
DEPTH = 2
DEC_BATCH = 128
DEC_SEQ = 1
PAST_LEN = 2048
PAGE_SIZE = 128

D_MIX = D_MODEL
NSA_WIDTH = D_MIX // 2
NSA_HD = 64
NSA_HEADS = NSA_WIDTH // NSA_HD
NSA_KV_HEADS = 2
NSA_GROUP = NSA_HEADS // NSA_KV_HEADS
CMP_BLOCK = 32
CMP_STRIDE = 16
CMP_RATIO = CMP_BLOCK // CMP_STRIDE
CMP_HIDDEN = 2 * NSA_HD
SEL_BLOCK = 64
SEL_TOPK = 16
WINDOW = 512
Q_BLOCK = 128
FORCE_BONUS = 1000.0
GLA_WIDTH = D_MIX - NSA_WIDTH
GLA_HEADS = 4
GLA_DV = GLA_WIDTH // GLA_HEADS
GLA_DK = GLA_DV // 2
GLA_RANK = 16
GLA_TAU = 16.0
GLA_CHUNK = 64
FFN_DIM = 2816
N_MOD = 9
EPS = 1e-6
KV_COLS = 2 * NSA_KV_HEADS * NSA_HD
IN_SIZES = (NSA_WIDTH, KV_COLS, KV_COLS, KV_COLS, 3 * NSA_HEADS,
            GLA_HEADS * GLA_DK, GLA_HEADS * GLA_DK, GLA_WIDTH, GLA_RANK, GLA_WIDTH)
IN_COLS = sum(IN_SIZES)

kernel_name = 'hybrid_nsa_gla_macaron_adaln_step'


def rms_norm(x, g):
    xf = x.astype(jnp.float32)
    y = xf * lax.rsqrt(jnp.mean(xf * xf, axis=-1, keepdims=True) + EPS)
    return (y * g.astype(jnp.float32)).astype(x.dtype)


def alibi_slopes(n):
    return jnp.asarray(np.exp2(-8.0 * np.arange(1, n + 1) / n), dtype=jnp.float32)


def masked_softmax(logits, mask):
    m = jnp.max(jnp.where(mask, logits, -jnp.inf), axis=-1, keepdims=True)
    m = jnp.where(jnp.isfinite(m), m, 0.0)
    e = jnp.where(mask, jnp.exp(logits - m), 0.0)
    return e / jnp.maximum(jnp.sum(e, axis=-1, keepdims=True), 1e-30)


def swiglu(h, w_in, w_out):
    g, u = jnp.split(jnp.einsum('btd,df->btf', h, w_in), 2, axis=-1)
    return jnp.einsum('btf,fd->btd', jax.nn.silu(g) * u, w_out)


def compress(x, pe, w1, b1, w2):
    b, t, g, hd = x.shape
    n_chunks = -(-t // CMP_STRIDE)
    x = jnp.pad(x, ((0, 0), (0, n_chunks * CMP_STRIDE - t), (0, 0), (0, 0)))
    ch = x.reshape(b, n_chunks, CMP_STRIDE, g, hd)
    nc = n_chunks - CMP_RATIO + 1
    pe_r = pe.reshape(CMP_RATIO, CMP_STRIDE, hd)
    w1_r = w1.reshape(CMP_RATIO, CMP_STRIDE, hd, CMP_HIDDEN)
    h = b1
    for r in range(CMP_RATIO):
        h = h + jnp.einsum('bnsgd,sdh->bngh', ch[:, r:r + nc] + pe_r[r][:, None, :], w1_r[r])
    return jnp.einsum('bngh,hd->bngd', jax.nn.gelu(h), w2)


def sel_blocks(x):
    b, t, g, hd = x.shape
    ns = -(-t // SEL_BLOCK)
    x = jnp.pad(x, ((0, 0), (0, ns * SEL_BLOCK - t), (0, 0), (0, 0)))
    return x.reshape(b, ns, SEL_BLOCK, g, hd).transpose(0, 3, 1, 2, 4)


def nsa_core(q, gates, t_pos, ck, cv, sk, sv, wk, wv, w_pos, slopes):
    b, nq = q.shape[:2]
    G, R = NSA_KV_HEADS, NSA_GROUP
    qg = q.reshape(b, nq, G, R, NSA_HD) * (NSA_HD ** -0.5)
    sl5 = slopes.reshape(G, R)[None, :, :, None, None]
    nc = ck.shape[1]
    c_start = jnp.arange(nc) * CMP_STRIDE
    c_end = c_start + (CMP_BLOCK - 1)
    dist_c = t_pos[:, None] - c_end[None, :]
    lc = jnp.einsum('bqgrd,bcgd->bgrqc', qg, ck).astype(jnp.float32)
    pc = masked_softmax(lc - sl5 * dist_c.astype(jnp.float32), dist_c >= 0)
    o_cmp = jnp.einsum('bgrqc,bcgd->bqgrd', pc.astype(cv.dtype), cv)
    ns = sk.shape[2]
    s_start = jnp.arange(ns) * SEL_BLOCK
    overlap = ((c_start[:, None] <= s_start[None, :] + SEL_BLOCK - 1)
               & (c_end[:, None] >= s_start[None, :])).astype(jnp.float32)
    imp = jnp.einsum('bgrqc,cs->bgqs', pc, overlap)
    cur = t_pos // SEL_BLOCK
    blk = jnp.arange(ns)
    forced = (blk[None, :] == 0) | (blk[None, :] == cur[:, None]) | (blk[None, :] == cur[:, None] - 1)
    valid = s_start[None, :] <= t_pos[:, None]
    score = jnp.where(valid, imp + jnp.where(forced, FORCE_BONUS, 0.0), -1e30)
    n_sel = min(SEL_TOPK, ns)
    _, idx = lax.top_k(score, n_sel)
    gather = jax.vmap(jax.vmap(lambda blocks, ids: blocks[ids]))
    kg = gather(sk, idx)
    vg = gather(sv, idx)
    s_pos = idx[..., None] * SEL_BLOCK + jnp.arange(SEL_BLOCK)
    dist_s = t_pos[None, None, :, None, None] - s_pos
    ls = jnp.einsum('bqgrd,bgqksd->bgrqks', qg, kg).astype(jnp.float32)
    ls = ls - slopes.reshape(G, R)[None, :, :, None, None, None] * dist_s[:, :, None].astype(jnp.float32)
    nk = n_sel * SEL_BLOCK
    ps = masked_softmax(ls.reshape(b, G, R, nq, nk), (dist_s >= 0)[:, :, None].reshape(b, G, 1, nq, nk))
    o_slc = jnp.einsum('bgrqn,bgqnd->bqgrd', ps.astype(vg.dtype), vg.reshape(b, G, nq, nk, NSA_HD))
    dist_w = t_pos[:, None] - w_pos[None, :]
    mask_w = (dist_w >= 0) & (dist_w < WINDOW) & (w_pos[None, :] >= 0)
    lw = jnp.einsum('bqgrd,bwgd->bgrqw', qg, wk).astype(jnp.float32)
    pw = masked_softmax(lw - sl5 * dist_w.astype(jnp.float32), mask_w)
    o_win = jnp.einsum('bgrqw,bwgd->bqgrd', pw.astype(wv.dtype), wv)
    gg = gates.reshape(b, nq, G, R, 3)
    o = gg[..., 0:1] * o_cmp + gg[..., 1:2] * o_slc + gg[..., 2:3] * o_win
    return o.reshape(b, nq, NSA_WIDTH)


def nsa_prompt(q, gates, kv_c, kv_s, kv_w, cmp, slopes):
    pe, w1, b1, w2 = cmp
    b, t = q.shape[:2]
    ck = compress(kv_c[:, :, 0], pe[0], w1[0], b1[0], w2[0])
    cv = compress(kv_c[:, :, 1], pe[1], w1[1], b1[1], w2[1])
    sk = sel_blocks(kv_s[:, :, 0])
    sv = sel_blocks(kv_s[:, :, 1])
    wkv = jnp.pad(kv_w, ((0, 0), (WINDOW, 0), (0, 0), (0, 0), (0, 0)))
    nqb = t // Q_BLOCK
    qb = q.reshape(b, nqb, Q_BLOCK, NSA_HEADS, NSA_HD).swapaxes(0, 1)
    gb = gates.reshape(b, nqb, Q_BLOCK, NSA_HEADS, 3).swapaxes(0, 1)

    def step(args):
        qi, gi, i = args
        start = i * Q_BLOCK
        t_pos = start + jnp.arange(Q_BLOCK)
        wi = lax.dynamic_slice_in_dim(wkv, start, WINDOW + Q_BLOCK, axis=1)
        w_pos = start - WINDOW + jnp.arange(WINDOW + Q_BLOCK)
        return nsa_core(qi, gi, t_pos, ck, cv, sk, sv, wi[:, :, 0], wi[:, :, 1], w_pos, slopes)

    o = lax.map(step, (qb, gb, jnp.arange(nqb)))
    return o.swapaxes(0, 1).reshape(b, t, NSA_WIDTH)


def nsa_sample(q, gates, kv_c, kv_s, kv_w, cmp, slopes, cache_c, cache_s, cache_w, page_table):
    pe, w1, b1, w2 = cmp
    nb, ns_new = q.shape[:2]
    past = page_table.shape[1] * cache_c.shape[1]
    wl = cache_w.shape[1]

    def gather_pages(cache):
        return cache[page_table].reshape(nb, past, 2, NSA_KV_HEADS, NSA_HD)

    full_c = jnp.concatenate([gather_pages(cache_c), kv_c], axis=1)
    full_s = jnp.concatenate([gather_pages(cache_s), kv_s], axis=1)
    win = jnp.concatenate([cache_w, kv_w], axis=1)
    ck = compress(full_c[:, :, 0], pe[0], w1[0], b1[0], w2[0])
    cv = compress(full_c[:, :, 1], pe[1], w1[1], b1[1], w2[1])
    sk = sel_blocks(full_s[:, :, 0])
    sv = sel_blocks(full_s[:, :, 1])
    t_pos = past + jnp.arange(ns_new)
    w_pos = past - wl + jnp.arange(wl + ns_new)
    return nsa_core(q, gates, t_pos, ck, cv, sk, sv, win[:, :, 0], win[:, :, 1], w_pos, slopes)


def gla_chunk(q, k, v, log_a, s0):
    c = q.shape[2]
    cum = jnp.cumsum(log_a, axis=2)
    causal = jnp.tril(jnp.ones((c, c), dtype=bool))
    diff = cum[:, :, :, None, :] - cum[:, :, None, :, :]
    decay = jnp.exp(jnp.where(causal[:, :, None], diff, -jnp.inf))
    attn = jnp.einsum('bhtd,bhsd,bhtsd->bhts', q, k, decay)
    o = jnp.einsum('bhtd,bhde->bhte', q * jnp.exp(cum), s0) + jnp.einsum('bhts,bhse->bhte', attn, v)
    last = cum[:, :, -1:, :]
    s_new = jnp.exp(last[:, :, 0, :, None]) * s0 + jnp.einsum('bhsd,bhse->bhde', k * jnp.exp(last - cum), v)
    return o, s_new


def gla_prompt(q, k, v, log_a):
    b, t = q.shape[:2]
    n = t // GLA_CHUNK

    def chunks(z):
        return z.astype(jnp.float32).reshape(b, n, GLA_CHUNK, GLA_HEADS, z.shape[-1]).transpose(1, 0, 3, 2, 4)

    def step(s, inp):
        o, s = gla_chunk(inp[0], inp[1], inp[2], inp[3], s)
        return s, o

    s0 = jnp.zeros((b, GLA_HEADS, GLA_DK, GLA_DV), jnp.float32)
    s_fin, o = lax.scan(step, s0, (chunks(q), chunks(k), chunks(v), chunks(log_a)))
    return o.transpose(1, 0, 3, 2, 4).reshape(b, t, GLA_HEADS, GLA_DV), s_fin


def mix_prompt(q_n, g_n, kv_c, kv_s, kv_w, q_g, k_g, v_g, log_a, cmp, slopes):
    t = q_n.shape[1]
    o_nsa = nsa_prompt(q_n, g_n, kv_c, kv_s, kv_w, cmp, slopes)
    o_gla, s_fin = gla_prompt(q_g, k_g, v_g, log_a)
    return o_nsa, o_gla.astype(q_n.dtype), (kv_c, kv_s, kv_w[:, t - min(WINDOW, t):], s_fin)


def mix_sample(q_n, g_n, kv_c, kv_s, kv_w, q_g, k_g, v_g, log_a, cmp, slopes,
               cache_c, cache_s, cache_w, state, page_table):
    o_nsa = nsa_sample(q_n, g_n, kv_c, kv_s, kv_w, cmp, slopes, cache_c, cache_s, cache_w, page_table)

    def heads_first(z):
        return z.astype(jnp.float32).transpose(0, 2, 1, 3)

    o, s_new = gla_chunk(heads_first(q_g), heads_first(k_g), heads_first(v_g), heads_first(log_a),
                         state.astype(jnp.float32))
    return o_nsa, o.transpose(0, 2, 1, 3).astype(q_n.dtype), (kv_c, kv_s, kv_w, s_new)


def trunk_layer(x, c, mix_fn, w_ada, b_ada, norm_g, f1_in, f1_out, w_in, gla_wa2, gla_ba, gla_norm,
                w_out, f2_in, f2_out):
    b, t, _ = x.shape
    mod = (jnp.einsum('bd,de->be', jax.nn.silu(c), w_ada) + b_ada).reshape(b, 1, N_MOD, D_MODEL)
    h = rms_norm(x, norm_g[0]) * (1.0 + mod[:, :, 1]) + mod[:, :, 0]
    x = x + 0.5 * mod[:, :, 2] * swiglu(h, f1_in, f1_out)
    h = rms_norm(x, norm_g[1]) * (1.0 + mod[:, :, 4]) + mod[:, :, 3]
    p = jnp.einsum('btd,de->bte', h, w_in)
    offsets = np.cumsum(IN_SIZES)[:-1].tolist()
    q_n, kv_c, kv_s, kv_w, g_n, q_g, k_g, v_g, a_g, o_g = jnp.split(p, offsets, axis=-1)
    q_n = q_n.reshape(b, t, NSA_HEADS, NSA_HD)
    kv_c = kv_c.reshape(b, t, 2, NSA_KV_HEADS, NSA_HD)
    kv_s = kv_s.reshape(b, t, 2, NSA_KV_HEADS, NSA_HD)
    kv_w = kv_w.reshape(b, t, 2, NSA_KV_HEADS, NSA_HD)
    g_n = jax.nn.sigmoid(g_n).reshape(b, t, NSA_HEADS, 3)
    q_g = q_g.reshape(b, t, GLA_HEADS, GLA_DK) * (GLA_DK ** -0.5)
    k_g = k_g.reshape(b, t, GLA_HEADS, GLA_DK)
    v_g = v_g.reshape(b, t, GLA_HEADS, GLA_DV)
    a_pre = (jnp.einsum('btr,re->bte', a_g, gla_wa2) + gla_ba).astype(jnp.float32)
    log_a = (jax.nn.log_sigmoid(a_pre) / GLA_TAU).reshape(b, t, GLA_HEADS, GLA_DK)
    o_nsa, o_gla, state = mix_fn(q_n, g_n, kv_c, kv_s, kv_w, q_g, k_g, v_g, log_a)
    o_gla = rms_norm(o_gla, gla_norm).reshape(b, t, GLA_WIDTH) * jax.nn.silu(o_g)
    y = jnp.einsum('bte,ed->btd', jnp.concatenate([o_nsa, o_gla], axis=-1), w_out)
    x = x + mod[:, :, 5] * y
    h = rms_norm(x, norm_g[2]) * (1.0 + mod[:, :, 7]) + mod[:, :, 6]
    x = x + 0.5 * mod[:, :, 8] * swiglu(h, f2_in, f2_out)
    return x, state


def setup_inputs(seed: int = 0) -> dict:
    key = jax.random.key(seed)
    ks = jax.random.split(key, 32)
    f32 = jnp.float32
    n_pages = PAST_LEN // PAGE_SIZE
    n_used = DEC_BATCH * n_pages
    n_phys = n_used + max(n_used // 4, 1)
    win_len = min(WINDOW, PAST_LEN)

    def nrm(k, shape, scale):
        return scale * jax.random.normal(k, shape, f32)

    page_table = jax.random.permutation(ks[8], n_phys)[:n_used].reshape(DEC_BATCH, n_pages).astype(jnp.int32)
    kv_shape = (DEPTH, n_phys, PAGE_SIZE, 2, NSA_KV_HEADS, NSA_HD)
    return {
        'x_prompt': nrm(ks[0], (BATCH, SEQ, D_MODEL), 1.0),
        'x_sample': nrm(ks[1], (DEC_BATCH, DEC_SEQ, D_MODEL), 1.0),
        'c_prompt': nrm(ks[2], (BATCH, D_MODEL), 1.0),
        'c_sample': nrm(ks[3], (DEC_BATCH, D_MODEL), 1.0),
        'cache_kv_cmp': nrm(ks[4], kv_shape, 1.0),
        'cache_kv_slc': nrm(ks[5], kv_shape, 1.0),
        'cache_kv_win': nrm(ks[6], (DEPTH, DEC_BATCH, win_len, 2, NSA_KV_HEADS, NSA_HD), 1.0),
        'state_gla': nrm(ks[7], (DEPTH, DEC_BATCH, GLA_HEADS, GLA_DK, GLA_DV), 0.3),
        'page_table': page_table,
        'w_ada': nrm(ks[9], (DEPTH, D_MODEL, N_MOD * D_MODEL), 0.5 * D_MODEL ** -0.5),
        'b_ada': nrm(ks[10], (DEPTH, N_MOD * D_MODEL), 0.02),
        'norm_g': 1.0 + nrm(ks[11], (DEPTH, 3, D_MODEL), 0.05),
        'ffn1_w_in': nrm(ks[12], (DEPTH, D_MODEL, 2 * FFN_DIM), D_MODEL ** -0.5),
        'ffn1_w_out': nrm(ks[13], (DEPTH, FFN_DIM, D_MODEL), FFN_DIM ** -0.5),
        'w_in': nrm(ks[14], (DEPTH, D_MODEL, IN_COLS), D_MODEL ** -0.5),
        'cmp_pe': nrm(ks[15], (DEPTH, 2, CMP_BLOCK, NSA_HD), 0.1),
        'cmp_w1': nrm(ks[16], (DEPTH, 2, CMP_BLOCK * NSA_HD, CMP_HIDDEN), (CMP_BLOCK * NSA_HD) ** -0.5),
        'cmp_b1': nrm(ks[17], (DEPTH, 2, CMP_HIDDEN), 0.02),
        'cmp_w2': nrm(ks[18], (DEPTH, 2, CMP_HIDDEN, NSA_HD), CMP_HIDDEN ** -0.5),
        'gla_wa2': nrm(ks[19], (DEPTH, GLA_RANK, GLA_HEADS * GLA_DK), GLA_RANK ** -0.5),
        'gla_ba': nrm(ks[20], (DEPTH, GLA_HEADS * GLA_DK), 0.1),
        'gla_norm': 1.0 + nrm(ks[21], (DEPTH, GLA_DV), 0.05),
        'w_out': nrm(ks[22], (DEPTH, D_MIX, D_MODEL), D_MIX ** -0.5),
        'ffn2_w_in': nrm(ks[23], (DEPTH, D_MODEL, 2 * FFN_DIM), D_MODEL ** -0.5),
        'ffn2_w_out': nrm(ks[24], (DEPTH, FFN_DIM, D_MODEL), FFN_DIM ** -0.5),
        'final_norm': 1.0 + nrm(ks[25], (D_MODEL,), 0.05),
    }


def reference(x_prompt, x_sample, c_prompt, c_sample, cache_kv_cmp, cache_kv_slc, cache_kv_win, state_gla,
              page_table, w_ada, b_ada, norm_g, ffn1_w_in, ffn1_w_out, w_in, cmp_pe, cmp_w1, cmp_b1, cmp_w2,
              gla_wa2, gla_ba, gla_norm, w_out, ffn2_w_in, ffn2_w_out, final_norm):
    slopes = alibi_slopes(NSA_HEADS)
    xp, xs = x_prompt, x_sample
    p_c, p_s, p_w, p_g = [], [], [], []
    s_c, s_s, s_w, s_g = [], [], [], []
    for l in range(DEPTH):
        cmp = (cmp_pe[l], cmp_w1[l], cmp_b1[l], cmp_w2[l])
        common = (w_ada[l], b_ada[l], norm_g[l], ffn1_w_in[l], ffn1_w_out[l], w_in[l], gla_wa2[l], gla_ba[l],
                  gla_norm[l], w_out[l], ffn2_w_in[l], ffn2_w_out[l])
        fp = functools.partial(mix_prompt, cmp=cmp, slopes=slopes)
        fs = functools.partial(mix_sample, cmp=cmp, slopes=slopes, cache_c=cache_kv_cmp[l],
                               cache_s=cache_kv_slc[l], cache_w=cache_kv_win[l], state=state_gla[l],
                               page_table=page_table)
        xp, stp = trunk_layer(xp, c_prompt, fp, *common)
        xs, sts = trunk_layer(xs, c_sample, fs, *common)
        p_c.append(stp[0]); p_s.append(stp[1]); p_w.append(stp[2]); p_g.append(stp[3])
        s_c.append(sts[0]); s_s.append(sts[1]); s_w.append(sts[2]); s_g.append(sts[3])
    y_prompt = rms_norm(xp, final_norm)
    y_sample = rms_norm(xs, final_norm)
    return (y_prompt, y_sample, jnp.stack(p_c), jnp.stack(p_s), jnp.stack(p_w), jnp.stack(p_g),
            jnp.stack(s_c), jnp.stack(s_s), jnp.stack(s_w), jnp.stack(s_g))
```

```python
import functools

import numpy as np
import jax
import jax.numpy as jnp
from jax import lax
from jax.experimental import pallas as pl
from jax.experimental.pallas import tpu as pltpu

F32 = jnp.float32
BF16 = jnp.bfloat16

D_MODEL = 1024
NSA_HD = 64
NSA_HEADS = 8
NSA_KV_HEADS = 2
NSA_GROUP = 4
NSA_WIDTH = NSA_HEADS * NSA_HD
CMP_BLOCK = 32
CMP_STRIDE = 16
CMP_HIDDEN = 2 * NSA_HD
SEL_BLOCK = 64
SEL_TOPK = 16
WINDOW = 512
FORCE_BONUS = 1000.0
GLA_HEADS = 4
GLA_DV = 128
GLA_DK = 64
GLA_WIDTH = GLA_HEADS * GLA_DV
GLA_RANK = 16
GLA_TAU = 16.0
GLA_CHUNK = 64
GLA_SUB = 16
FFN_DIM = 2816
N_MOD = 9
EPS = 1e-6
KV_COLS = 2 * NSA_KV_HEADS * NSA_HD
IN_SIZES = (NSA_WIDTH, KV_COLS, KV_COLS, KV_COLS, 3 * NSA_HEADS,
            GLA_HEADS * GLA_DK, GLA_HEADS * GLA_DK, GLA_WIDTH, GLA_RANK, GLA_WIDTH)
IN_PAD = (NSA_WIDTH, KV_COLS, KV_COLS, KV_COLS, 128,
          GLA_HEADS * GLA_DK, GLA_HEADS * GLA_DK, GLA_WIDTH, 128, GLA_WIDTH)
IN_OFF = tuple(int(v) for v in np.cumsum((0,) + IN_PAD))

LANES = 128
NEG = -1e30
VMEM_LIMIT = 56 * 1024 * 1024
FFN_TF = FFN_DIM // 2


def _cparams(sem):
    return pltpu.CompilerParams(dimension_semantics=sem, vmem_limit_bytes=VMEM_LIMIT)


def _dot(a, b):
    return jnp.dot(a, b, preferred_element_type=F32)


def _dot_nt(a, b):
    return lax.dot_general(a, b, (((1,), (1,)), ((), ())), preferred_element_type=F32)


def _split3(x):
    x1 = x.astype(BF16)
    r = x - x1.astype(F32)
    x2 = r.astype(BF16)
    x3 = (r - x2.astype(F32)).astype(BF16)
    return x1, x2, x3


def _rms_mod(x, g, sc, sh):
    y = x * lax.rsqrt(jnp.mean(x * x, axis=-1, keepdims=True) + EPS) * g
    return y * (1.0 + sc) + sh


def _masked_softmax(lg, mask):
    lg = jnp.where(mask, lg, NEG)
    m = jnp.max(lg, axis=-1, keepdims=True)
    e = jnp.where(mask, jnp.exp(lg - m), 0.0)
    return e / jnp.maximum(jnp.sum(e, axis=-1, keepdims=True), 1e-30)


def _top_k_mask(score, blk, k):
    sel = jnp.zeros_like(score)
    big = float(score.shape[-1])
    for _ in range(k):
        m = jnp.max(score, axis=-1, keepdims=True)
        idx = jnp.min(jnp.where(score == m, blk, big), axis=-1, keepdims=True)
        hit = blk == idx
        sel = jnp.where(hit, 1.0, sel)
        score = jnp.where(hit, -jnp.inf, score)
    return sel


def _ada_kernel(c_ref, w_ref, b_ref, o_ref):
    c = c_ref[...]
    s = c * jax.nn.sigmoid(c)
    o_ref[0] = _dot(s.astype(BF16), w_ref[0].astype(BF16)) + b_ref[0]


def _ada(c_all, w_ada, b_ada):
    depth, d, n = w_ada.shape
    m = c_all.shape[0]
    tn = 1024
    return pl.pallas_call(
        _ada_kernel,
        grid=(depth, n // tn),
        in_specs=[pl.BlockSpec((m, d), lambda l, j: (0, 0)),
                  pl.BlockSpec((1, d, tn), lambda l, j: (l, 0, j)),
                  pl.BlockSpec((1, 1, tn), lambda l, j: (l, 0, j))],
        out_specs=pl.BlockSpec((1, m, tn), lambda l, j: (l, 0, j)),
        out_shape=jax.ShapeDtypeStruct((depth, m, n), F32),
        compiler_params=_cparams(("parallel", "parallel")),
        name="ada",
    )(c_all, w_ada, b_ada.reshape(depth, 1, n))


def _mod_spec(mod, tm, nargs):
    r = mod.shape[1]
    d = mod.shape[2]
    if nargs == 3:
        if r == 1:
            return pl.BlockSpec((1, 1, d), lambda b, i, f: (b, 0, 0))
        return pl.BlockSpec((1, tm, d), lambda b, i, f: (b, i, 0))
    if r == 1:
        return pl.BlockSpec((1, 1, d), lambda b, i: (b, 0, 0))
    return pl.BlockSpec((1, tm, d), lambda b, i: (b, i, 0))


def _ffn_kernel(x_ref, sh_ref, sc_ref, gt_ref, g_ref, wg_ref, wu_ref, wo_ref, o_ref, h_scr, acc_scr):
    f = pl.program_id(2)

    @pl.when(f == 0)
    def _():
        h = _rms_mod(x_ref[0], g_ref[...], sc_ref[0], sh_ref[0])
        h_scr[...] = h.astype(BF16)
        acc_scr[...] = jnp.zeros_like(acc_scr)

    h = h_scr[...]
    g = _dot(h, wg_ref[...])
    u = _dot(h, wu_ref[...])
    a = (g * jax.nn.sigmoid(g) * u).astype(BF16)
    acc_scr[...] += _dot(a, wo_ref[...])

    @pl.when(f == pl.num_programs(2) - 1)
    def _():
        o_ref[0] = x_ref[0] + 0.5 * gt_ref[0] * acc_scr[...]


def _ffn(x, sh, sc, gt, g, w_in, w_out, tm):
    bx, tx, d = x.shape
    nf = FFN_DIM // FFN_TF
    xs = pl.BlockSpec((1, tm, d), lambda b, i, f: (b, i, 0))
    ms = _mod_spec(sh, tm, 3)
    return pl.pallas_call(
        _ffn_kernel,
        grid=(bx, tx // tm, nf),
        in_specs=[xs, ms, ms, ms,
                  pl.BlockSpec((1, d), lambda b, i, f: (0, 0)),
                  pl.BlockSpec((d, FFN_TF), lambda b, i, f: (0, f)),
                  pl.BlockSpec((d, FFN_TF), lambda b, i, f: (0, f + nf)),
                  pl.BlockSpec((FFN_TF, d), lambda b, i, f: (f, 0))],
        out_specs=xs,
        out_shape=jax.ShapeDtypeStruct(x.shape, F32),
        scratch_shapes=[pltpu.VMEM((tm, d), BF16), pltpu.VMEM((tm, d), F32)],
        compiler_params=_cparams(("parallel", "parallel", "arbitrary")),
        name="ffn",
    )(x, sh, sc, gt, g, w_in, w_in, w_out)


def _proj_kernel(x_ref, sh_ref, sc_ref, g_ref, w_ref, wa2_ref, ba_ref,
                 qn_ref, kvc_ref, kvs_ref, kvw_ref, gn_ref, qg_ref, kg_ref, vg_ref, la_ref, og_ref):
    h = _rms_mod(x_ref[0], g_ref[...], sc_ref[0], sh_ref[0]).astype(BF16)
    p = _dot(h, w_ref[...])
    o = IN_OFF
    qn_ref[0] = (p[:, o[0]:o[1]] * (NSA_HD ** -0.5)).astype(BF16)
    kvc_ref[0] = p[:, o[1]:o[2]]
    kvs_ref[0] = p[:, o[2]:o[3]]
    kvw_ref[0] = p[:, o[3]:o[4]]
    gn_ref[0] = jax.nn.sigmoid(p[:, o[4]:o[5]])
    qg_ref[0] = p[:, o[5]:o[6]] * (GLA_DK ** -0.5)
    kg_ref[0] = p[:, o[6]:o[7]]
    vg_ref[0] = p[:, o[7]:o[8]]
    a_pre = _dot(p[:, o[8]:o[9]].astype(BF16), wa2_ref[...]) + ba_ref[...]
    log_sig = jnp.minimum(a_pre, 0.0) - jnp.log(1.0 + jnp.exp(-jnp.abs(a_pre)))
    la_ref[0] = log_sig * (1.0 / GLA_TAU)
    og_ref[0] = p[:, o[9]:o[10]]


def _proj(x, sh, sc, g, w_all, wa2p, ba, tm):
    bx, tx, d = x.shape
    xs = pl.BlockSpec((1, tm, d), lambda b, i: (b, i, 0))
    ms = _mod_spec(sh, tm, 2)
    widths = (NSA_WIDTH, KV_COLS, KV_COLS, KV_COLS, 128, 256, 256, GLA_WIDTH, 256, GLA_WIDTH)
    dtypes = (BF16,) + (F32,) * 9
    full = lambda shape: pl.BlockSpec(shape, lambda b, i: (0,) * len(shape))
    return pl.pallas_call(
        _proj_kernel,
        grid=(bx, tx // tm),
        in_specs=[xs, ms, ms, full((1, d)), full(w_all.shape), full(wa2p.shape), full(ba.shape)],
        out_specs=[pl.BlockSpec((1, tm, w), lambda b, i: (b, i, 0)) for w in widths],
        out_shape=[jax.ShapeDtypeStruct((bx, tx, w), dt) for w, dt in zip(widths, dtypes)],
        compiler_params=_cparams(("parallel", "parallel")),
        name="proj",
    )(x, sh, sc, g, w_all, wa2p, ba)


def _cmp_kernel(x_ref, pelo_ref, pehi_ref, w1lo_ref, w1hi_ref, b1_ref, w2_ref, o_ref):
    x = x_ref[0]
    n = x.shape[0]
    a = _dot((x + pelo_ref[...]).astype(BF16), w1lo_ref[...])
    b = _dot((x + pehi_ref[...]).astype(BF16), w1hi_ref[...])
    h = a + pltpu.roll(b, n - 1, 0) + b1_ref[...]
    o_ref[0] = _dot(jax.nn.gelu(h).astype(BF16), w2_ref[...])


def _cmp_prompt(x16, cw):
    b, n, w = x16.shape
    full = lambda a: pl.BlockSpec(a.shape, lambda i: (0,) * a.ndim)
    args = (cw["pe_lo"], cw["pe_hi"], cw["w1lo"], cw["w1hi"], cw["b1"], cw["w2"])
    return pl.pallas_call(
        _cmp_kernel,
        grid=(b,),
        in_specs=[pl.BlockSpec((1, n, w), lambda i: (i, 0, 0))] + [full(a) for a in args],
        out_specs=pl.BlockSpec((1, n, KV_COLS), lambda i: (i, 0, 0)),
        out_shape=jax.ShapeDtypeStruct((b, n, KV_COLS), F32),
        compiler_params=_cparams(("parallel",)),
        name="cmp_prompt",
    )(x16, *args)


def _nsa_prompt_kernel(slope_ref, q_ref, gate_ref, ck_ref, cv_ref, sk_ref, sv_ref, wk_ref, wv_ref, ov_ref,
                       o_ref, *, tq, tk, ncp, nsp):
    g = pl.program_id(1)
    i = pl.program_id(2)
    q0 = i * tq
    row = q0 + lax.broadcasted_iota(jnp.int32, (tq, 1), 0)
    slopes = [slope_ref[g * NSA_GROUP + r] for r in range(NSA_GROUP)]
    qs = [q_ref[0, 0, r] for r in range(NSA_GROUP)]

    ck = ck_ref[0, 0, 0]
    cv = cv_ref[0, 0, 0]
    c_end = lax.broadcasted_iota(jnp.int32, (1, ncp), 1) * CMP_STRIDE + (CMP_BLOCK - 1)
    dist_c = row - c_end
    mask_c = dist_c >= 0
    dist_cf = dist_c.astype(F32)
    o_cmp = []
    psum = jnp.zeros((tq, ncp), F32)
    for r in range(NSA_GROUP):
        pc = _masked_softmax(_dot_nt(qs[r], ck) - slopes[r] * dist_cf, mask_c)
        o_cmp.append(_dot(pc.astype(BF16), cv))
        psum = psum + pc
    p_hi = psum.astype(BF16)
    p_lo = (psum - p_hi.astype(F32)).astype(BF16)
    imp = _dot(p_hi, ov_ref[...]) + _dot(p_lo, ov_ref[...])

    blk_i = lax.broadcasted_iota(jnp.int32, (1, nsp), 1)
    cur = jnp.right_shift(row, 6)
    forced = jnp.where(blk_i == 0, 1.0, 0.0) + jnp.where(blk_i == cur, 1.0, 0.0) + jnp.where(blk_i == cur - 1, 1.0, 0.0)
    score = jnp.where(blk_i * SEL_BLOCK <= row, imp + jnp.where(forced > 0.5, FORCE_BONUS, 0.0), NEG)
    sel = _top_k_mask(score, blk_i.astype(F32), SEL_TOPK).astype(BF16)

    blk_col = lax.broadcasted_iota(jnp.int32, (nsp, 1), 0)

    def sel_step(kt, carry):
        ms, ls, accs = carry
        k0 = pl.multiple_of(kt * tk, tk)
        k_t = sk_ref[0, 0, 0, pl.ds(k0, tk), :]
        v_t = sv_ref[0, 0, 0, pl.ds(k0, tk), :]
        kpos = k0 + lax.broadcasted_iota(jnp.int32, (1, tk), 1)
        expand = jnp.where(blk_col == jnp.right_shift(kpos, 6), 1.0, 0.0).astype(BF16)
        selx = _dot(sel, expand)
        dist = row - kpos
        neg = jnp.where(dist >= 0, (selx - 1.0) * 1e30, NEG)
        dist_f = dist.astype(F32)
        ms2, ls2, accs2 = [], [], []
        for r in range(NSA_GROUP):
            lg = _dot_nt(qs[r], k_t) + (neg - slopes[r] * dist_f)
            m_new = jnp.maximum(ms[r], jnp.max(lg, axis=-1, keepdims=True))
            alpha = jnp.exp(ms[r] - m_new)
            p = jnp.exp(lg - m_new)
            ms2.append(m_new)
            ls2.append(alpha * ls[r] + jnp.sum(p, axis=-1, keepdims=True))
            accs2.append(alpha * accs[r] + _dot(p.astype(BF16), v_t))
        return tuple(ms2), tuple(ls2), tuple(accs2)

    init = (tuple(jnp.full((tq, 1), NEG, F32) for _ in range(NSA_GROUP)),
            tuple(jnp.zeros((tq, 1), F32) for _ in range(NSA_GROUP)),
            tuple(jnp.zeros((tq, NSA_HD), F32) for _ in range(NSA_GROUP)))
    _, ls, accs = lax.fori_loop(0, q0 // tk + 1, sel_step, init)

    wl = WINDOW + tq
    w0 = pl.multiple_of(jnp.maximum(q0 - WINDOW, 0), tq)
    k_w = wk_ref[0, 0, 0, pl.ds(w0, wl), :]
    v_w = wv_ref[0, 0, 0, pl.ds(w0, wl), :]
    dist_w = row - (w0 + lax.broadcasted_iota(jnp.int32, (1, wl), 1))
    neg_w = jnp.where(dist_w >= 0, jnp.where(dist_w < WINDOW, 0.0, NEG), NEG)
    dist_wf = dist_w.astype(F32)
    gates = gate_ref[0, 0]
    for r in range(NSA_GROUP):
        lg = _dot_nt(qs[r], k_w) + (neg_w - slopes[r] * dist_wf)
        p = jnp.exp(lg - jnp.max(lg, axis=-1, keepdims=True))
        o_win = _dot(p.astype(BF16), v_w) / jnp.sum(p, axis=-1, keepdims=True)
        o_slc = accs[r] / ls[r]
        o_ref[0, 0, r] = (gates[:, 3 * r:3 * r + 1] * o_cmp[r] + gates[:, 3 * r + 1:3 * r + 2] * o_slc
                          + gates[:, 3 * r + 2:3 * r + 3] * o_win)


def _overlap_matrix(ncp, nsp):
    c = np.arange(ncp)[:, None] * CMP_STRIDE
    s = np.arange(nsp)[None, :] * SEL_BLOCK
    return jnp.asarray((c <= s + SEL_BLOCK - 1) & (c + CMP_BLOCK - 1 >= s), dtype=BF16)


def _alibi_slopes():
    return jnp.asarray(np.exp2(-8.0 * np.arange(1, NSA_HEADS + 1) / NSA_HEADS), dtype=F32)


def _nsa_prompt(q5, gates, ckv, skv, wkv, tq, tk):
    b, g, r, t, hd = q5.shape
    ncp = ckv.shape[3]
    nsp = max(LANES, t // SEL_BLOCK)
    kv = lambda n, which: pl.BlockSpec((1, 1, 1, n, hd), lambda bi, gi, i: (bi, which, gi, 0, 0))
    return pl.pallas_call(
        functools.partial(_nsa_prompt_kernel, tq=tq, tk=tk, ncp=ncp, nsp=nsp),
        grid=(b, g, t // tq),
        in_specs=[pl.BlockSpec(memory_space=pltpu.SMEM),
                  pl.BlockSpec((1, 1, r, tq, hd), lambda bi, gi, i: (bi, gi, 0, i, 0)),
                  pl.BlockSpec((1, 1, tq, 16), lambda bi, gi, i: (bi, gi, i, 0)),
                  kv(ncp, 0), kv(ncp, 1), kv(t, 0), kv(t, 1), kv(t, 0), kv(t, 1),
                  pl.BlockSpec((ncp, nsp), lambda bi, gi, i: (0, 0))],
        out_specs=pl.BlockSpec((1, 1, r, tq, hd), lambda bi, gi, i: (bi, gi, 0, i, 0)),
        out_shape=jax.ShapeDtypeStruct((b, g, r, t, hd), F32),
        compiler_params=_cparams(("parallel", "parallel", "arbitrary")),
        name="nsa_prompt",
    )(_alibi_slopes(), q5, gates, ckv, ckv, skv, skv, wkv, wkv, _overlap_matrix(ncp, nsp))


def _gla_prompt_kernel(q_ref, k_ref, la_ref, kt_ref, lat_ref, v_ref, o_ref, s_ref, s_scr, *, nch):
    c = GLA_CHUNK
    ti = pl.program_id(2)

    @pl.when(ti == 0)
    def _():
        s_scr[...] = jnp.zeros_like(s_scr)

    r_i = lax.broadcasted_iota(jnp.int32, (c, c), 0)
    c_i = lax.broadcasted_iota(jnp.int32, (c, c), 1)
    tril = jnp.where(c_i <= r_i, 1.0, 0.0).astype(BF16)
    triu = jnp.where(r_i <= c_i, 1.0, 0.0).astype(BF16)
    pos = lax.broadcasted_iota(jnp.int32, (c, 1), 0)
    nsub = c // GLA_SUB
    w_i = lax.broadcasted_iota(jnp.int32, (1, GLA_SUB, 1), 1)

    def chunk(ci, _):
        q = q_ref[0, 0, ci]
        k = k_ref[0, 0, ci]
        la = la_ref[0, 0, ci]
        kt = kt_ref[0, 0, ci]
        lat = lat_ref[0, 0, ci]
        v = v_ref[0, 0, ci]
        vb = v.astype(BF16)
        s0 = s_scr[...]
        a1, a2, a3 = _split3(la)
        cum = _dot(tril, a1) + _dot(tril, a2) + _dot(tril, a3)
        b1, b2, b3 = _split3(lat)
        cum_t = _dot(b1, triu) + _dot(b2, triu) + _dot(b3, triu)
        last_t = cum_t[:, c - 1:c]
        o = _dot((q * jnp.exp(cum)).astype(BF16), s0.astype(BF16))
        attn = jnp.zeros((c, c), F32)
        for j in range(nsub - 1):
            ce = cum[GLA_SUB * (j + 1) - 1:GLA_SUB * (j + 1), :]
            qh = jnp.where(pos >= GLA_SUB * (j + 1), q * jnp.exp(jnp.minimum(cum - ce, 0.0)), 0.0)
            in_j = jnp.where(pos >= GLA_SUB * j, jnp.where(pos < GLA_SUB * (j + 1), 1.0, 0.0), 0.0)
            kh = in_j * (k * jnp.exp(jnp.minimum(ce - cum, 0.0)))
            attn = attn + _dot_nt(qh.astype(BF16), kh.astype(BF16))
        o = o + _dot(attn.astype(BF16), vb)
        q3 = q.reshape(nsub, GLA_SUB, GLA_DK)
        k3 = k.reshape(nsub, GLA_SUB, GLA_DK)
        c3 = cum.reshape(nsub, GLA_SUB, GLA_DK)
        v3 = v.reshape(nsub, GLA_SUB, GLA_DV)
        od = jnp.zeros((nsub, GLA_SUB, GLA_DV), F32)
        for u in range(GLA_SUB):
            w = q3 * k3[:, u:u + 1, :] * jnp.exp(jnp.minimum(c3 - c3[:, u:u + 1, :], 0.0))
            a = jnp.sum(jnp.where(w_i >= u, w, 0.0), axis=-1, keepdims=True)
            od = od + a * v3[:, u:u + 1, :]
        o_ref[0, 0, ci] = o + od.reshape(c, GLA_DV)
        kd_t = kt * jnp.exp(last_t - cum_t)
        s_scr[...] = jnp.exp(last_t) * s0 + _dot(kd_t.astype(BF16), vb)
        return 0

    lax.fori_loop(0, nch, chunk, 0)

    @pl.when(ti == pl.num_programs(2) - 1)
    def _():
        s_ref[0, 0] = s_scr[...]


def _gla_prompt(q, k, la, kt, lat, v, nch):
    b, h, nc, c, dk = q.shape
    dv = v.shape[-1]
    spec = lambda d1, d2: pl.BlockSpec((1, 1, nch, d1, d2), lambda bi, hi, ti: (bi, hi, ti, 0, 0))
    return pl.pallas_call(
        functools.partial(_gla_prompt_kernel, nch=nch),
        grid=(b, h, nc // nch),
        in_specs=[spec(c, dk), spec(c, dk), spec(c, dk), spec(dk, c), spec(dk, c), spec(c, dv)],
        out_specs=[spec(c, dv), pl.BlockSpec((1, 1, dk, dv), lambda bi, hi, ti: (bi, hi, 0, 0))],
        out_shape=[jax.ShapeDtypeStruct((b, h, nc, c, dv), F32), jax.ShapeDtypeStruct((b, h, dk, dv), F32)],
        scratch_shapes=[pltpu.VMEM((dk, dv), F32)],
        compiler_params=_cparams(("parallel", "parallel", "arbitrary")),
        name="gla_prompt",
    )(q, k, la, kt, lat, v)


def _out_kernel(x_ref, gt_ref, on_ref, ogla_ref, og_ref, gn_ref, w_ref, o_ref):
    y = jnp.zeros(x_ref.shape[1:], F32)
    for h in range(NSA_HEADS):
        y = y + _dot(on_ref[0, h].astype(BF16), w_ref[h * NSA_HD:(h + 1) * NSA_HD, :])
    og = og_ref[0]
    for h in range(GLA_HEADS):
        o = ogla_ref[0, h]
        o = o * lax.rsqrt(jnp.mean(o * o, axis=-1, keepdims=True) + EPS) * gn_ref[...]
        gate = og[:, h * GLA_DV:(h + 1) * GLA_DV]
        o = o * (gate * jax.nn.sigmoid(gate))
        y = y + _dot(o.astype(BF16), w_ref[NSA_WIDTH + h * GLA_DV:NSA_WIDTH + (h + 1) * GLA_DV, :])
    o_ref[0] = x_ref[0] + gt_ref[0] * y


def _out_proj(x, gt, o_nsa, o_gla, o_g, gla_norm, w_out, tm):
    bx, tx, d = x.shape
    xs = pl.BlockSpec((1, tm, d), lambda b, i: (b, i, 0))
    return pl.pallas_call(
        _out_kernel,
        grid=(bx, tx // tm),
        in_specs=[xs, _mod_spec(gt, tm, 2),
                  pl.BlockSpec((1, NSA_HEADS, tm, NSA_HD), lambda b, i: (b, 0, i, 0)),
                  pl.BlockSpec((1, GLA_HEADS, tm, GLA_DV), lambda b, i: (b, 0, i, 0)),
                  pl.BlockSpec((1, tm, GLA_WIDTH), lambda b, i: (b, i, 0)),
                  pl.BlockSpec((1, GLA_DV), lambda b, i: (0, 0)),
                  pl.BlockSpec(w_out.shape, lambda b, i: (0, 0))],
        out_specs=xs,
        out_shape=jax.ShapeDtypeStruct(x.shape, F32),
        compiler_params=_cparams(("parallel", "parallel")),
        name="out_proj",
    )(x, gt, o_nsa, o_gla, o_g, gla_norm, w_out)


def _norm_kernel(x_ref, g_ref, o_ref):
    x = x_ref[0]
    o_ref[0] = x * lax.rsqrt(jnp.mean(x * x, axis=-1, keepdims=True) + EPS) * g_ref[...]


def _final_norm(x, g, tm):
    bx, tx, d = x.shape
    xs = pl.BlockSpec((1, tm, d), lambda b, i: (b, i, 0))
    return pl.pallas_call(
        _norm_kernel,
        grid=(bx, tx // tm),
        in_specs=[xs, pl.BlockSpec((1, d), lambda b, i: (0, 0))],
        out_specs=xs,
        out_shape=jax.ShapeDtypeStruct(x.shape, F32),
        compiler_params=_cparams(("parallel", "parallel")),
        name="final_norm",
    )(x, g)


def _nsa_sample_kernel(pt_ref, qall_ref, gate_ref, kvc_ref, kvs_ref, kvw_ref, *rest, n_pages, page, past, wlen):
    del pt_ref
    cmp_pages = rest[:n_pages]
    slc_pages = rest[n_pages:2 * n_pages]
    (cw_ref, w1lo_ref, w1hi_ref, pelo_ref, pehi_ref, b1_ref, w2_ref, ov_ref, o_ref) = rest[2 * n_pages:]
    nh = NSA_HEADS
    ncb = past // CMP_STRIDE
    nkv = 2 * LANES
    qa = qall_ref[0]
    qa_f = qa.astype(F32)
    hrow = lax.broadcasted_iota(jnp.int32, (nh, 1), 0)
    slope = jnp.exp2(-(hrow + 1).astype(F32))
    lane = lax.broadcasted_iota(jnp.int32, (1, LANES), 1)

    tail_row = jnp.concatenate([kvc_ref[0], jnp.zeros((1, CMP_STRIDE * nkv - nkv), F32)], axis=1)
    tail = jnp.where(lax.broadcasted_iota(jnp.int32, (ncb, 1), 0) == 0, tail_row, 0.0)
    x = jnp.concatenate([p[0, 0] for p in cmp_pages] + [tail], axis=0)
    h_lo = _dot_nt(w1lo_ref[...], (x + pelo_ref[...]).astype(BF16))
    h_hi = _dot_nt(w1hi_ref[...], (x + pehi_ref[...]).astype(BF16))
    h = h_lo[:, :ncb] + pltpu.roll(h_hi, 2 * ncb - 1, 1)[:, :ncb] + b1_ref[...]
    ckv_t = _dot(w2_ref[...], jax.nn.gelu(h).astype(BF16))
    ck_t = ckv_t[:LANES].astype(BF16)
    cv_t = ckv_t[LANES:].astype(BF16)
    c_end = lane * CMP_STRIDE + (CMP_BLOCK - 1)
    dist_c = past - c_end
    pc = _masked_softmax(_dot(qa[:, :LANES], ck_t) - slope * dist_c.astype(F32), dist_c >= 0)
    o_cmp = _dot_nt(pc.astype(BF16), cv_t)

    ps0 = jnp.sum(jnp.where(hrow < NSA_GROUP, pc, 0.0), axis=0, keepdims=True)
    ps1 = jnp.sum(jnp.where(hrow >= NSA_GROUP, pc, 0.0), axis=0, keepdims=True)
    psum = jnp.where(hrow == 0, ps0, jnp.where(hrow == 1, ps1, 0.0))
    p_hi = psum.astype(BF16)
    p_lo = (psum - p_hi.astype(F32)).astype(BF16)
    imp = _dot(p_hi, ov_ref[...]) + _dot(p_lo, ov_ref[...])
    cur = past // SEL_BLOCK
    forced = jnp.where(lane == 0, 1.0, 0.0) + jnp.where(lane == cur, 1.0, 0.0) + jnp.where(lane == cur - 1, 1.0, 0.0)
    score = jnp.where(lane * SEL_BLOCK <= past, imp + jnp.where(forced > 0.5, FORCE_BONUS, 0.0), NEG)
    sel = _top_k_mask(score, lane.astype(F32), SEL_TOPK)
    sel_h = jnp.where(hrow < NSA_GROUP, sel[0:1], sel[1:2])

    def attend(tiles, new_row, new_neg):
        lgs = [_dot_nt(qa, kv.astype(BF16)) + neg for kv, neg in tiles]
        lg_new = jnp.sum(qa_f[:, :LANES] * new_row[:, :LANES], axis=-1, keepdims=True) + new_neg
        m = lg_new
        for lg in lgs:
            m = jnp.maximum(m, jnp.max(lg, axis=-1, keepdims=True))
        p_new = jnp.exp(lg_new - m)
        l = p_new
        acc = p_new * new_row
        for lg, (kv, _) in zip(lgs, tiles):
            p = jnp.exp(lg - m)
            l = l + jnp.sum(p, axis=-1, keepdims=True)
            acc = acc + _dot(p.astype(BF16), kv.astype(BF16))
        return (acc / l)[:, LANES:]

    per = page // SEL_BLOCK
    tiles = []
    for pg in range(n_pages):
        kv = slc_pages[pg][0, 0]
        dist = past - (pg * page + lane)
        selx = jnp.zeros((nh, LANES), F32)
        for u in range(per):
            in_u = jnp.where(lane >= u * SEL_BLOCK, jnp.where(lane < (u + 1) * SEL_BLOCK, 1.0, 0.0), 0.0)
            selx = selx + in_u * sel_h[:, pg * per + u:pg * per + u + 1]
        neg = jnp.where(dist >= 0, (selx - 1.0) * 1e30, NEG) - slope * dist.astype(F32)
        tiles.append((kv, neg))
    o_slc = attend(tiles, kvs_ref[0], (sel_h[:, cur:cur + 1] - 1.0) * 1e30)

    tiles = []
    for j in range(wlen // LANES):
        kv = cw_ref[0, 0, j * LANES:(j + 1) * LANES, :]
        w_pos = past - wlen + j * LANES + lane
        dist = past - w_pos
        ok = jnp.where(dist >= 0, jnp.where(dist < WINDOW, jnp.where(w_pos >= 0, 1.0, 0.0), 0.0), 0.0)
        neg = (ok - 1.0) * 1e30 - slope * dist.astype(F32)
        tiles.append((kv, neg))
    o_win = attend(tiles, kvw_ref[0], jnp.zeros((nh, 1), F32))

    gates = gate_ref[0]
    pick = lambda a: jnp.where(hrow < NSA_GROUP, a[:, :NSA_HD], a[:, NSA_HD:])
    o_ref[0] = gates[:, 0:1] * pick(o_cmp) + gates[:, 1:2] * pick(o_slc) + gates[:, 2:3] * pick(o_win)


def _nsa_sample(layer, page_table, q_all, gates, kvc, kvs, kvw, cache_c, cache_s, cache_w, cw):
    nb, n_pages = page_table.shape
    depth, n_phys, page = cache_c.shape[:3]
    past = n_pages * page
    wlen = cache_w.shape[2]
    ncb = past // CMP_STRIDE
    cpp = page // CMP_STRIDE
    cc = cache_c.reshape(depth, n_phys, cpp, CMP_STRIDE * KV_COLS)
    cs = cache_s.reshape(depth, n_phys, page, KV_COLS)
    cwin = cache_w.reshape(depth, nb, wlen, KV_COLS)
    row = lambda w: pl.BlockSpec((1, 1, w), lambda b, pt: (b, 0, 0))
    full = lambda a: pl.BlockSpec(a.shape, lambda b, pt: (0,) * a.ndim)
    cmp_specs = [pl.BlockSpec((1, 1, cpp, CMP_STRIDE * KV_COLS), lambda b, pt, j=j: (layer, pt[b, j], 0, 0))
                 for j in range(n_pages)]
    slc_specs = [pl.BlockSpec((1, 1, page, KV_COLS), lambda b, pt, j=j: (layer, pt[b, j], 0, 0))
                 for j in range(n_pages)]
    ov = _overlap_matrix(ncb, LANES)
    wargs = (cw["w1lo_t"], cw["w1hi_t"], cw["pe_lo"], cw["pe_hi"], cw["b1_col"], cw["w2_t"], ov)
    grid_spec = pltpu.PrefetchScalarGridSpec(
        num_scalar_prefetch=1,
        grid=(nb,),
        in_specs=[pl.BlockSpec((1, NSA_HEADS, 2 * LANES), lambda b, pt: (b, 0, 0)),
                  pl.BlockSpec((1, NSA_HEADS, LANES), lambda b, pt: (b, 0, 0)),
                  row(KV_COLS), row(KV_COLS), row(KV_COLS)] + cmp_specs + slc_specs
                 + [pl.BlockSpec((1, 1, wlen, KV_COLS), lambda b, pt: (layer, b, 0, 0))] + [full(a) for a in wargs],
        out_specs=pl.BlockSpec((1, NSA_HEADS, NSA_HD), lambda b, pt: (b, 0, 0)),
    )
    return pl.pallas_call(
        functools.partial(_nsa_sample_kernel, n_pages=n_pages, page=page, past=past, wlen=wlen),
        grid_spec=grid_spec,
        out_shape=jax.ShapeDtypeStruct((nb, NSA_HEADS, NSA_HD), F32),
        compiler_params=_cparams(("arbitrary",)),
        name="nsa_sample",
    )(page_table, q_all, gates, kvc, kvs, kvw, *([cc] * n_pages), *([cs] * n_pages), cwin, *wargs)


def _gla_sample_kernel(col_ref, v_ref, s_ref, o_ref, sn_ref, *, bb):
    for b in range(bb):
        cols = col_ref[b]
        for h in range(GLA_HEADS):
            a = jnp.exp(cols[:, h:h + 1])
            k = cols[:, GLA_HEADS + h:GLA_HEADS + h + 1]
            q = cols[:, 2 * GLA_HEADS + h:2 * GLA_HEADS + h + 1]
            v = v_ref[b, h:h + 1, :]
            s0 = s_ref[b, h]
            qk = jnp.sum(q * k, axis=0, keepdims=True)
            o_ref[b, h:h + 1, :] = jnp.sum((q * a) * s0, axis=0, keepdims=True) + qk * v
            sn_ref[b, h] = a * s0 + k * v


def _gla_sample(cols, v, state, bb):
    nb = v.shape[0]
    return pl.pallas_call(
        functools.partial(_gla_sample_kernel, bb=bb),
        grid=(nb // bb,),
        in_specs=[pl.BlockSpec((bb, GLA_DK, 16), lambda i: (i, 0, 0)),
                  pl.BlockSpec((bb, GLA_HEADS, GLA_DV), lambda i: (i, 0, 0)),
                  pl.BlockSpec((bb, GLA_HEADS, GLA_DK, GLA_DV), lambda i: (i, 0, 0, 0))],
        out_specs=[pl.BlockSpec((bb, GLA_HEADS, GLA_DV), lambda i: (i, 0, 0)),
                   pl.BlockSpec((bb, GLA_HEADS, GLA_DK, GLA_DV), lambda i: (i, 0, 0, 0))],
        out_shape=[jax.ShapeDtypeStruct((nb, GLA_HEADS, GLA_DV), F32),
                   jax.ShapeDtypeStruct((nb, GLA_HEADS, GLA_DK, GLA_DV), F32)],
        compiler_params=_cparams(("parallel",)),
        name="gla_sample",
    )(cols, v, state)


def _pad_cols(w, sizes, pads):
    parts, off = [], 0
    for s, p in zip(sizes, pads):
        parts.append(jnp.pad(w[:, off:off + s], ((0, 0), (0, p - s))))
        off += s
    return jnp.concatenate(parts, axis=1)


def _cmp_weights(pe, w1, b1, w2):
    g = NSA_KV_HEADS
    eye = jnp.eye(2, dtype=F32)
    eg = jnp.eye(g, dtype=F32)
    w1r = w1.reshape(2, 2, CMP_STRIDE, NSA_HD, CMP_HIDDEN)
    big = jnp.einsum("krsdh,ka,gb->rskgdabh", w1r, eye, eg).reshape(2, CMP_STRIDE * KV_COLS, 2 * g * CMP_HIDDEN)
    w2big = jnp.einsum("khd,ka,gb->kghabd", w2, eye, eg).reshape(2 * g * CMP_HIDDEN, KV_COLS)
    per = pe.reshape(2, 2, CMP_STRIDE, NSA_HD)
    pet = jnp.broadcast_to(per.transpose(1, 2, 0, 3)[:, :, :, None, :], (2, CMP_STRIDE, 2, g, NSA_HD))
    pet = pet.reshape(2, 1, CMP_STRIDE * KV_COLS)
    b1big = jnp.broadcast_to(b1[:, None, :], (2, g, CMP_HIDDEN)).reshape(1, 2 * g * CMP_HIDDEN)
    return {
        "w1lo": big[0].astype(BF16), "w1hi": big[1].astype(BF16),
        "w1lo_t": big[0].T.astype(BF16), "w1hi_t": big[1].T.astype(BF16),
        "pe_lo": pet[0], "pe_hi": pet[1], "b1": b1big, "b1_col": b1big.T,
        "w2": w2big.astype(BF16), "w2_t": w2big.T.astype(BF16),
    }


def _heads_first(a, hd):
    b, t, w = a.shape
    return a.reshape(b, t, w // hd, hd).transpose(0, 2, 1, 3)


def _kv_split(a):
    b, t, _ = a.shape
    return a.reshape(b, t, 2, NSA_KV_HEADS, NSA_HD).transpose(0, 2, 3, 1, 4).astype(BF16)


def _mods(mod, rows):
    m = mod.reshape(mod.shape[0], N_MOD, D_MODEL)
    if rows == 1:
        return [m[:, j][:, None, :] for j in range(N_MOD)]
    return [m[:, j][None, :, :] for j in range(N_MOD)]


def _layer_prompt(x, mods, lw, tm, tq, tk):
    b, t, _ = x.shape
    x = _ffn(x, mods[0], mods[1], mods[2], lw["norm_g"][0:1], lw["f1_in"], lw["f1_out"], tm)
    qn, kvc, kvs, kvw, gn, qg, kg, vg, la, og = _proj(x, mods[3], mods[4], lw["norm_g"][1:2], lw["w_all"],
                                                      lw["wa2p"], lw["ba"], tm)
    ckv = _cmp_prompt(kvc.reshape(b, t // CMP_STRIDE, CMP_STRIDE * KV_COLS), lw["cmp"])
    q5 = qn.reshape(b, t, NSA_KV_HEADS, NSA_GROUP, NSA_HD).transpose(0, 2, 3, 1, 4)
    gates = gn[:, :, :3 * NSA_HEADS].reshape(b, t, NSA_KV_HEADS, 3 * NSA_GROUP).transpose(0, 2, 1, 3)
    gates = jnp.pad(gates, ((0, 0), (0, 0), (0, 0), (0, 16 - 3 * NSA_GROUP)))
    o_nsa = _nsa_prompt(q5, gates, _kv_split(ckv), _kv_split(kvs), _kv_split(kvw), tq, tk)
    o_nsa = o_nsa.reshape(b, NSA_HEADS, t, NSA_HD)
    nc = t // GLA_CHUNK
    ch = lambda a, hd: _heads_first(a, hd).reshape(b, GLA_HEADS, nc, GLA_CHUNK, hd)
    k5, la5 = ch(kg, GLA_DK), ch(la, GLA_DK)
    o_gla, s_fin = _gla_prompt(ch(qg, GLA_DK), k5, la5, k5.swapaxes(-1, -2), la5.swapaxes(-1, -2),
                               ch(vg, GLA_DV), min(8, nc))
    o_gla = o_gla.reshape(b, GLA_HEADS, t, GLA_DV)
    x = _out_proj(x, mods[5], o_nsa, o_gla, og, lw["gla_norm"], lw["w_out"], tm)
    x = _ffn(x, mods[6], mods[7], mods[8], lw["norm_g"][2:3], lw["f2_in"], lw["f2_out"], tm)
    win = min(WINDOW, t)
    state = (kvc.reshape(b, t, 2, NSA_KV_HEADS, NSA_HD), kvs.reshape(b, t, 2, NSA_KV_HEADS, NSA_HD),
             kvw[:, t - win:].reshape(b, win, 2, NSA_KV_HEADS, NSA_HD), s_fin)
    return x, state


def _layer_sample(layer, x, mods, lw, cache_c, cache_s, cache_w, state, page_table):
    nb = x.shape[1]
    x = _ffn(x, mods[0], mods[1], mods[2], lw["norm_g"][0:1], lw["f1_in"], lw["f1_out"], nb)
    qn, kvc, kvs, kvw, gn, qg, kg, vg, la, og = _proj(x, mods[3], mods[4], lw["norm_g"][1:2], lw["w_all"],
                                                      lw["wa2p"], lw["ba"], nb)
    q4 = qn[0].reshape(nb, NSA_KV_HEADS, NSA_GROUP, NSA_HD)
    q_all = (q4[:, :, :, None, :] * jnp.eye(NSA_KV_HEADS, dtype=BF16)[None, :, None, :, None]).reshape(nb, NSA_HEADS, LANES)
    q_all = jnp.pad(q_all, ((0, 0), (0, 0), (0, LANES)))
    gates = jnp.pad(gn[0][:, :3 * NSA_HEADS].reshape(nb, NSA_HEADS, 3), ((0, 0), (0, 0), (0, LANES - 3)))
    r3 = lambda a: a[0][:, None, :]
    o_nsa = _nsa_sample(layer, page_table, q_all, gates, r3(kvc), r3(kvs), r3(kvw),
                        cache_c, cache_s, cache_w, lw["cmp"])
    o_nsa = o_nsa.transpose(1, 0, 2)[None]
    col = lambda a: a[0].reshape(nb, GLA_HEADS, GLA_DK).transpose(0, 2, 1)
    cols = jnp.concatenate([col(la), col(kg), col(qg), jnp.zeros((nb, GLA_DK, 16 - 3 * GLA_HEADS), F32)], axis=-1)
    o_gla, s_new = _gla_sample(cols, vg[0].reshape(nb, GLA_HEADS, GLA_DV), state, 8)
    o_gla = o_gla.transpose(1, 0, 2)[None]
    x = _out_proj(x, mods[5], o_nsa, o_gla, og, lw["gla_norm"], lw["w_out"], nb)
    x = _ffn(x, mods[6], mods[7], mods[8], lw["norm_g"][2:3], lw["f2_in"], lw["f2_out"], nb)
    kv5 = lambda a: a[0].reshape(nb, 1, 2, NSA_KV_HEADS, NSA_HD)
    return x, (kv5(kvc), kv5(kvs), kv5(kvw), s_new)


def kernel(x_prompt, x_sample, c_prompt, c_sample, cache_kv_cmp, cache_kv_slc, cache_kv_win, state_gla, page_table, w_ada, b_ada, norm_g, ffn1_w_in, ffn1_w_out, w_in, cmp_pe, cmp_w1, cmp_b1, cmp_w2, gla_wa2, gla_ba, gla_norm, w_out, ffn2_w_in, ffn2_w_out, final_norm):
    depth = w_ada.shape[0]
    b, t, d = x_prompt.shape
    nb = x_sample.shape[0]
    bp = -(-b // 8) * 8
    c_all = jnp.concatenate([c_prompt, jnp.zeros((bp - b, d), F32), c_sample], axis=0)
    mod_all = _ada(c_all, w_ada, b_ada)

    tm = min(512, t)
    tq = min(256, t)
    tk = min(512, t)
    xp = x_prompt
    xs = x_sample.reshape(1, nb, d)
    outs_p, outs_s = [], []
    for l in range(depth):
        lw = {
            "norm_g": norm_g[l],
            "f1_in": ffn1_w_in[l].astype(BF16), "f1_out": ffn1_w_out[l].astype(BF16),
            "f2_in": ffn2_w_in[l].astype(BF16), "f2_out": ffn2_w_out[l].astype(BF16),
            "w_all": _pad_cols(w_in[l], IN_SIZES, IN_PAD).astype(BF16),
            "wa2p": jnp.pad(gla_wa2[l], ((0, 128 - GLA_RANK), (0, 0))).astype(BF16),
            "ba": gla_ba[l][None, :],
            "gla_norm": gla_norm[l][None, :],
            "w_out": w_out[l].astype(BF16),
            "cmp": _cmp_weights(cmp_pe[l], cmp_w1[l], cmp_b1[l], cmp_w2[l]),
        }
        xp, st_p = _layer_prompt(xp, _mods(mod_all[l, :b], 1), lw, tm, tq, tk)
        xs, st_s = _layer_sample(l, xs, _mods(mod_all[l, bp:], nb), lw, cache_kv_cmp, cache_kv_slc,
                                 cache_kv_win, state_gla[l], page_table)
        outs_p.append(st_p)
        outs_s.append(st_s)
    fn = final_norm[None, :]
    y_prompt = _final_norm(xp, fn, tm)
    y_sample = _final_norm(xs, fn, nb).reshape(nb, 1, d)
    stack = lambda outs, j: jnp.stack([o[j] for o in outs])
    return (y_prompt, y_sample,
            stack(outs_p, 0), stack(outs_p, 1), stack(outs_p, 2), stack(outs_p, 3),
            stack(outs_s, 0), stack(outs_s, 1), stack(outs_s, 2), stack(outs_s, 3))
```

```python
import functools

import numpy as np
import jax
import jax.numpy as jnp
from jax import lax
from jax.experimental import pallas as pl
from jax.experimental.pallas import tpu as pltpu

F32 = jnp.float32
BF16 = jnp.bfloat16

D_MODEL = 1024
NSA_HD = 64
NSA_HEADS = 8
NSA_KV_HEADS = 2
NSA_GROUP = 4
NSA_WIDTH = NSA_HEADS * NSA_HD
CMP_BLOCK = 32
CMP_STRIDE = 16
CMP_HIDDEN = 2 * NSA_HD
SEL_BLOCK = 64
SEL_TOPK = 16
WINDOW = 512
FORCE_BONUS = 1000.0
GLA_HEADS = 4
GLA_DV = 128
GLA_DK = 64
GLA_WIDTH = GLA_HEADS * GLA_DV
GLA_RANK = 16
GLA_TAU = 16.0
GLA_CHUNK = 64
GLA_SUB = 16
FFN_DIM = 2816
N_MOD = 9
EPS = 1e-6
KV_COLS = 2 * NSA_KV_HEADS * NSA_HD
IN_SIZES = (NSA_WIDTH, KV_COLS, KV_COLS, KV_COLS, 3 * NSA_HEADS,
            GLA_HEADS * GLA_DK, GLA_HEADS * GLA_DK, GLA_WIDTH, GLA_RANK, GLA_WIDTH)
IN_PAD = (NSA_WIDTH, KV_COLS, KV_COLS, KV_COLS, 128,
          GLA_HEADS * GLA_DK, GLA_HEADS * GLA_DK, GLA_WIDTH, 128, GLA_WIDTH)
IN_OFF = tuple(int(v) for v in np.cumsum((0,) + IN_PAD))

LANES = 128
NEG = -1e30
LOG2E = 1.4426950408889634
VMEM_LIMIT = 56 * 1024 * 1024
FFN_TF = FFN_DIM // 2


def _cparams(sem):
    return pltpu.CompilerParams(dimension_semantics=sem, vmem_limit_bytes=VMEM_LIMIT)


def _dot(a, b):
    return jnp.dot(a, b, preferred_element_type=F32)


def _dot_nt(a, b):
    return lax.dot_general(a, b, (((1,), (1,)), ((), ())), preferred_element_type=F32)


def _split3(x):
    x1 = x.astype(BF16)
    r = x - x1.astype(F32)
    x2 = r.astype(BF16)
    x3 = (r - x2.astype(F32)).astype(BF16)
    return x1, x2, x3


def _rms_mod(x, g, sc, sh):
    y = x * lax.rsqrt(jnp.mean(x * x, axis=-1, keepdims=True) + EPS) * g
    return y * (1.0 + sc) + sh


def _masked_softmax(lg, mask):
    lg = jnp.where(mask, lg, NEG)
    m = jnp.max(lg, axis=-1, keepdims=True)
    e = jnp.where(mask, jnp.exp(lg - m), 0.0)
    return e / jnp.maximum(jnp.sum(e, axis=-1, keepdims=True), 1e-30)


def _top_k_mask(score, blk, k, axis=-1):
    sel = jnp.zeros_like(score)
    big = float(score.shape[axis])
    for _ in range(k):
        m = jnp.max(score, axis=axis, keepdims=True)
        idx = jnp.min(jnp.where(score == m, blk, big), axis=axis, keepdims=True)
        hit = blk == idx
        sel = jnp.where(hit, 1.0, sel)
        score = jnp.where(hit, -jnp.inf, score)
    return sel


def _ada_kernel(c_ref, w_ref, b_ref, o_ref):
    c = c_ref[...]
    s = c * jax.nn.sigmoid(c)
    o_ref[0] = _dot(s.astype(BF16), w_ref[0].astype(BF16)) + b_ref[0]


def _ada(c_all, w_ada, b_ada):
    depth, d, n = w_ada.shape
    m = c_all.shape[0]
    tn = 1024
    return pl.pallas_call(
        _ada_kernel,
        grid=(depth, n // tn),
        in_specs=[pl.BlockSpec((m, d), lambda l, j: (0, 0)),
                  pl.BlockSpec((1, d, tn), lambda l, j: (l, 0, j)),
                  pl.BlockSpec((1, 1, tn), lambda l, j: (l, 0, j))],
        out_specs=pl.BlockSpec((1, m, tn), lambda l, j: (l, 0, j)),
        out_shape=jax.ShapeDtypeStruct((depth, m, n), F32),
        compiler_params=_cparams(("parallel", "parallel")),
        name="ada",
    )(c_all, w_ada, b_ada.reshape(depth, 1, n))


def _mod_spec(mod, tm, nargs):
    r = mod.shape[1]
    d = mod.shape[2]
    if nargs == 3:
        if r == 1:
            return pl.BlockSpec((1, 1, d), lambda b, i, f: (b, 0, 0))
        return pl.BlockSpec((1, tm, d), lambda b, i, f: (b, i, 0))
    if r == 1:
        return pl.BlockSpec((1, 1, d), lambda b, i: (b, 0, 0))
    return pl.BlockSpec((1, tm, d), lambda b, i: (b, i, 0))


def _ffn_kernel(x_ref, sh_ref, sc_ref, gt_ref, g_ref, wg_ref, wu_ref, wo_ref, o_ref, h_scr, acc_scr):
    f = pl.program_id(2)

    @pl.when(f == 0)
    def _():
        h = _rms_mod(x_ref[0], g_ref[...], sc_ref[0], sh_ref[0])
        h_scr[...] = h.astype(BF16)
        acc_scr[...] = jnp.zeros_like(acc_scr)

    h = h_scr[...]
    g = _dot(h, wg_ref[...])
    u = _dot(h, wu_ref[...])
    a = (g * jax.nn.sigmoid(g) * u).astype(BF16)
    acc_scr[...] += _dot(a, wo_ref[...])

    @pl.when(f == pl.num_programs(2) - 1)
    def _():
        o_ref[0] = x_ref[0] + 0.5 * gt_ref[0] * acc_scr[...]


def _ffn(x, sh, sc, gt, g, w_in, w_out, tm):
    bx, tx, d = x.shape
    nf = FFN_DIM // FFN_TF
    xs = pl.BlockSpec((1, tm, d), lambda b, i, f: (b, i, 0))
    ms = _mod_spec(sh, tm, 3)
    return pl.pallas_call(
        _ffn_kernel,
        grid=(bx, tx // tm, nf),
        in_specs=[xs, ms, ms, ms,
                  pl.BlockSpec((1, d), lambda b, i, f: (0, 0)),
                  pl.BlockSpec((d, FFN_TF), lambda b, i, f: (0, f)),
                  pl.BlockSpec((d, FFN_TF), lambda b, i, f: (0, f + nf)),
                  pl.BlockSpec((FFN_TF, d), lambda b, i, f: (f, 0))],
        out_specs=xs,
        out_shape=jax.ShapeDtypeStruct(x.shape, F32),
        scratch_shapes=[pltpu.VMEM((tm, d), BF16), pltpu.VMEM((tm, d), F32)],
        compiler_params=_cparams(("parallel", "parallel", "arbitrary")),
        name="ffn",
    )(x, sh, sc, gt, g, w_in, w_in, w_out)


def _proj_kernel(x_ref, sh_ref, sc_ref, g_ref, w_ref, wkvt_ref, wa2_ref, ba_ref,
                 qn_ref, kvc_ref, kvs_ref, kvw_ref, kvct_ref, kvst_ref, kvwt_ref, kvtb_ref,
                 gn_ref, qg_ref, kg_ref, vg_ref, la_ref, og_ref):
    h = _rms_mod(x_ref[0], g_ref[...], sc_ref[0], sh_ref[0]).astype(BF16)
    p = _dot(h, w_ref[...])
    o = IN_OFF
    qn_ref[0] = (p[:, o[0]:o[1]] * (NSA_HD ** -0.5 * LOG2E)).astype(BF16)
    kvc_ref[0] = p[:, o[1]:o[2]]
    kvs_ref[0] = p[:, o[2]:o[3]]
    kvw_ref[0] = p[:, o[3]:o[4]]
    kvt = _dot_nt(wkvt_ref[...], h)
    kvct_ref[0] = kvt[0:KV_COLS]
    kvst_ref[0] = kvt[KV_COLS:2 * KV_COLS]
    kvwt_ref[0] = kvt[2 * KV_COLS:3 * KV_COLS]
    kvtb_ref[0] = kvt[KV_COLS:3 * KV_COLS].astype(BF16)
    gn_ref[0] = jax.nn.sigmoid(p[:, o[4]:o[5]])
    qg_ref[0] = p[:, o[5]:o[6]] * (GLA_DK ** -0.5)
    kg_ref[0] = p[:, o[6]:o[7]]
    vg_ref[0] = p[:, o[7]:o[8]]
    a_pre = _dot(p[:, o[8]:o[9]].astype(BF16), wa2_ref[...]) + ba_ref[...]
    log_sig = jnp.minimum(a_pre, 0.0) - jnp.log(1.0 + jnp.exp(-jnp.abs(a_pre)))
    la_ref[0] = log_sig * (1.0 / GLA_TAU)
    og_ref[0] = p[:, o[9]:o[10]]


def _proj(x, sh, sc, g, w_all, wkvt, wa2p, ba, tm):
    bx, tx, d = x.shape
    xs = pl.BlockSpec((1, tm, d), lambda b, i: (b, i, 0))
    ms = _mod_spec(sh, tm, 2)
    full = lambda shape: pl.BlockSpec(shape, lambda b, i: (0,) * len(shape))
    rows = lambda w, dt: (pl.BlockSpec((1, tm, w), lambda b, i: (b, i, 0)), jax.ShapeDtypeStruct((bx, tx, w), dt))
    cols = lambda w, dt: (pl.BlockSpec((1, w, tm), lambda b, i: (b, 0, i)), jax.ShapeDtypeStruct((bx, w, tx), dt))
    outs = [rows(NSA_WIDTH, BF16), rows(KV_COLS, F32), rows(KV_COLS, F32), rows(KV_COLS, F32),
            cols(KV_COLS, F32), cols(KV_COLS, F32), cols(KV_COLS, F32), cols(2 * KV_COLS, BF16),
            rows(128, F32), rows(256, F32), rows(256, F32), rows(GLA_WIDTH, F32), rows(256, F32),
            rows(GLA_WIDTH, F32)]
    return pl.pallas_call(
        _proj_kernel,
        grid=(bx, tx // tm),
        in_specs=[xs, ms, ms, full((1, d)), full(w_all.shape), full(wkvt.shape), full(wa2p.shape),
                  full(ba.shape)],
        out_specs=[o[0] for o in outs],
        out_shape=[o[1] for o in outs],
        compiler_params=_cparams(("parallel", "parallel")),
        name="proj",
    )(x, sh, sc, g, w_all, wkvt, wa2p, ba)


def _cmp_kernel(x_ref, pelo_ref, pehi_ref, w1lo_ref, w1hi_ref, b1_ref, w2_ref, o_ref):
    x = x_ref[0]
    n = x.shape[0]
    h_lo = _dot_nt(w1lo_ref[...], (x + pelo_ref[...]).astype(BF16))
    h_hi = _dot_nt(w1hi_ref[...], (x + pehi_ref[...]).astype(BF16))
    h = h_lo + pltpu.roll(h_hi, n - 1, 1) + b1_ref[...]
    o_ref[0] = _dot(w2_ref[...], jax.nn.gelu(h).astype(BF16))


def _cmp_prompt(x16, cw):
    b, n, w = x16.shape
    full = lambda a: pl.BlockSpec(a.shape, lambda i: (0,) * a.ndim)
    args = (cw["pe_lo"], cw["pe_hi"], cw["w1lo_t"], cw["w1hi_t"], cw["b1_col"], cw["w2_t"])
    return pl.pallas_call(
        _cmp_kernel,
        grid=(b,),
        in_specs=[pl.BlockSpec((1, n, w), lambda i: (i, 0, 0))] + [full(a) for a in args],
        out_specs=pl.BlockSpec((1, KV_COLS, n), lambda i: (i, 0, 0)),
        out_shape=jax.ShapeDtypeStruct((b, KV_COLS, n), F32),
        compiler_params=_cparams(("parallel",)),
        name="cmp_prompt",
    )(x16, *args)


def _nsa_prompt_kernel(q_ref, gate_ref, ck_ref, cv_ref, ks_ref, vs_ref, kw_ref, vw_ref, ovt_ref,
                       o_ref, *, tq, tk, ncp, nsp):
    i = pl.program_id(2)
    q0 = i * tq
    nr = NSA_GROUP * tq
    row = q0 + jnp.bitwise_and(lax.broadcasted_iota(jnp.int32, (nr, 1), 0), tq - 1)
    q = q_ref[0, 0].reshape(nr, LANES)

    c_end = lax.broadcasted_iota(jnp.int32, (1, ncp), 1) * CMP_STRIDE + (CMP_BLOCK - 1)
    mask_c = c_end <= row
    lg = jnp.where(mask_c, _dot(q, ck_ref[0, 0]), NEG)
    e = jnp.where(mask_c, jnp.exp2(lg - jnp.max(lg, axis=-1, keepdims=True)), 0.0)
    pc = e / jnp.maximum(jnp.sum(e, axis=-1, keepdims=True), 1e-30)
    o_cmp = _dot_nt(pc.astype(BF16), cv_ref[0, 0])
    psum = pc[0:tq]
    for r in range(1, NSA_GROUP):
        psum = psum + pc[r * tq:(r + 1) * tq]
    p_hi = psum.astype(BF16)
    p_lo = (psum - p_hi.astype(F32)).astype(BF16)
    imp_t = _dot_nt(ovt_ref[...], p_hi) + _dot_nt(ovt_ref[...], p_lo)

    blk = lax.broadcasted_iota(jnp.int32, (nsp, 1), 0)
    t_lane = q0 + lax.broadcasted_iota(jnp.int32, (1, tq), 1)
    cur = jnp.right_shift(t_lane, 6)
    forced = jnp.where(blk == 0, 1.0, 0.0) + jnp.where(blk == cur, 1.0, 0.0) + jnp.where(blk == cur - 1, 1.0, 0.0)
    valid = blk * SEL_BLOCK <= t_lane
    score = jnp.where(valid, imp_t + jnp.where(forced > 0.5, FORCE_BONUS, 0.0), NEG)
    sel_t = _top_k_mask(score, blk.astype(F32), SEL_TOPK, axis=0)
    sel_bias = (jnp.where(valid, sel_t, 0.0) - 1.0) * 1e30
    sel_bias = sel_bias.T.astype(BF16)
    qa = jnp.concatenate([jnp.concatenate([sel_bias] * NSA_GROUP, axis=0), q], axis=1)

    def sel_tile(kt, carry, diag):
        m, acc = carry
        k0 = pl.multiple_of(kt * tk, tk)
        lg = _dot(qa, ks_ref[0, 0, :, pl.ds(k0, tk)])
        if diag:
            lg = jnp.where(k0 + lax.broadcasted_iota(jnp.int32, (1, tk), 1) <= row, lg, NEG)
        m_new = jnp.maximum(m, jnp.max(lg, axis=-1, keepdims=True))
        p = jnp.exp2(lg - m_new).astype(BF16)
        return m_new, jnp.exp2(m - m_new) * acc + _dot_nt(p, vs_ref[0, 0, :, pl.ds(k0, tk)])

    init = (jnp.full((nr, 1), NEG, F32), jnp.zeros((nr, LANES), F32))
    n_full = q0 // tk
    carry = lax.fori_loop(0, n_full, lambda kt, c: sel_tile(kt, c, False), init)
    _, acc = sel_tile(n_full, carry, True)
    o_slc = acc[:, :NSA_HD] / acc[:, NSA_HD:NSA_HD + 1]

    wl = WINDOW + tq
    w0 = pl.multiple_of(jnp.maximum(q0 - WINDOW, 0), tq)
    dist_w = row - (w0 + lax.broadcasted_iota(jnp.int32, (1, wl), 1))
    lg = _dot(q, kw_ref[0, 0, :, pl.ds(w0, wl)])
    lg = jnp.where(dist_w >= 0, jnp.where(dist_w < WINDOW, lg, NEG), NEG)
    p = jnp.exp2(lg - jnp.max(lg, axis=-1, keepdims=True)).astype(BF16)
    ow = _dot_nt(p, vw_ref[0, 0, :, pl.ds(w0, wl)])
    o_win = ow[:, :NSA_HD] / ow[:, NSA_HD:NSA_HD + 1]

    gates = gate_ref[0, 0]
    gcol = lambda j: jnp.concatenate([gates[:, 3 * r + j:3 * r + j + 1] for r in range(NSA_GROUP)], axis=0)
    o = gcol(0) * o_cmp + gcol(1) * o_slc + gcol(2) * o_win
    o_ref[0, 0] = o.reshape(NSA_GROUP, tq, NSA_HD)


def _overlap_matrix(ncp, nsp):
    c = np.arange(ncp)[:, None] * CMP_STRIDE
    s = np.arange(nsp)[None, :] * SEL_BLOCK
    return jnp.asarray((c <= s + SEL_BLOCK - 1) & (c + CMP_BLOCK - 1 >= s), dtype=BF16)


def _bf16_round(x):
    return np.asarray(x, dtype=BF16).astype(np.float32)


def _alibi_q_cols():
    slopes = np.exp2(-8.0 * np.arange(1, NSA_HEADS + 1) / NSA_HEADS)
    c_hi = float(_bf16_round(LOG2E))
    c_lo = float(_bf16_round(LOG2E - c_hi))
    cols = np.stack([c_hi * slopes * SEL_BLOCK, c_lo * slopes * SEL_BLOCK, c_hi * slopes, c_lo * slopes], axis=1)
    return jnp.asarray(cols, BF16)


def _alibi_k_rows(pos):
    pos = np.asarray(pos)
    return jnp.asarray(np.stack([pos // SEL_BLOCK, pos // SEL_BLOCK, pos % SEL_BLOCK, pos % SEL_BLOCK]), BF16)


def _aug_rows(parts, total, lead):
    parts = [jnp.broadcast_to(p, lead + p.shape[-2:]) for p in parts]
    used = sum(p.shape[-2] for p in parts)
    if total > used:
        parts.append(jnp.zeros(lead + (total - used, parts[0].shape[-1]), BF16))
    return jnp.concatenate(parts, axis=-2)


def _nsa_prompt(q_pad, gates, ck, cv, ks, vs, kw, vw, tq, tk):
    b, g, r, t, _ = q_pad.shape
    ncp = ck.shape[3]
    nsp = max(LANES, t // SEL_BLOCK)
    kv = lambda a: pl.BlockSpec((1, 1) + a.shape[2:], lambda bi, gi, i: (bi, gi, 0, 0))
    ovt = _overlap_matrix(ncp, nsp).T
    return pl.pallas_call(
        functools.partial(_nsa_prompt_kernel, tq=tq, tk=tk, ncp=ncp, nsp=nsp),
        grid=(b, g, t // tq),
        in_specs=[pl.BlockSpec((1, 1, r, tq, LANES), lambda bi, gi, i: (bi, gi, 0, i, 0)),
                  pl.BlockSpec((1, 1, tq, 16), lambda bi, gi, i: (bi, gi, i, 0)),
                  kv(ck), kv(cv), kv(ks), kv(vs), kv(kw), kv(vw),
                  pl.BlockSpec((nsp, ncp), lambda bi, gi, i: (0, 0))],
        out_specs=pl.BlockSpec((1, 1, r, tq, NSA_HD), lambda bi, gi, i: (bi, gi, 0, i, 0)),
        out_shape=jax.ShapeDtypeStruct((b, g, r, t, NSA_HD), F32),
        compiler_params=_cparams(("parallel", "parallel", "arbitrary")),
        name="nsa_prompt",
    )(q_pad, gates, ck, cv, ks, vs, kw, vw, ovt)


def _gla_prompt_kernel(q_ref, k_ref, la_ref, kt_ref, lat_ref, v_ref, o_ref, s_ref, s_scr, *, nch):
    c = GLA_CHUNK
    ti = pl.program_id(2)

    @pl.when(ti == 0)
    def _():
        s_scr[...] = jnp.zeros_like(s_scr)

    r_i = lax.broadcasted_iota(jnp.int32, (c, c), 0)
    c_i = lax.broadcasted_iota(jnp.int32, (c, c), 1)
    tril = jnp.where(c_i <= r_i, 1.0, 0.0).astype(BF16)
    triu = jnp.where(r_i <= c_i, 1.0, 0.0).astype(BF16)
    pos = lax.broadcasted_iota(jnp.int32, (c, 1), 0)
    nsub = c // GLA_SUB
    w_i = lax.broadcasted_iota(jnp.int32, (1, GLA_SUB, 1), 1)

    def chunk(ci, _):
        q = q_ref[0, 0, ci]
        k = k_ref[0, 0, ci]
        la = la_ref[0, 0, ci]
        kt = kt_ref[0, 0, ci]
        lat = lat_ref[0, 0, ci]
        v = v_ref[0, 0, ci]
        vb = v.astype(BF16)
        s0 = s_scr[...]
        a1, a2, a3 = _split3(la)
        cum = _dot(tril, a1) + _dot(tril, a2) + _dot(tril, a3)
        b1, b2, b3 = _split3(lat)
        cum_t = _dot(b1, triu) + _dot(b2, triu) + _dot(b3, triu)
        last_t = cum_t[:, c - 1:c]
        o = _dot((q * jnp.exp(cum)).astype(BF16), s0.astype(BF16))
        attn = jnp.zeros((c, c), F32)
        for j in range(nsub - 1):
            ce = cum[GLA_SUB * (j + 1) - 1:GLA_SUB * (j + 1), :]
            qh = jnp.where(pos >= GLA_SUB * (j + 1), q * jnp.exp(jnp.minimum(cum - ce, 0.0)), 0.0)
            in_j = jnp.where(pos >= GLA_SUB * j, jnp.where(pos < GLA_SUB * (j + 1), 1.0, 0.0), 0.0)
            kh = in_j * (k * jnp.exp(jnp.minimum(ce - cum, 0.0)))
            attn = attn + _dot_nt(qh.astype(BF16), kh.astype(BF16))
        o = o + _dot(attn.astype(BF16), vb)
        q3 = q.reshape(nsub, GLA_SUB, GLA_DK)
        k3 = k.reshape(nsub, GLA_SUB, GLA_DK)
        c3 = cum.reshape(nsub, GLA_SUB, GLA_DK)
        v3 = v.reshape(nsub, GLA_SUB, GLA_DV)
        od = jnp.zeros((nsub, GLA_SUB, GLA_DV), F32)
        for u in range(GLA_SUB):
            w = q3 * k3[:, u:u + 1, :] * jnp.exp(jnp.minimum(c3 - c3[:, u:u + 1, :], 0.0))
            a = jnp.sum(jnp.where(w_i >= u, w, 0.0), axis=-1, keepdims=True)
            od = od + a * v3[:, u:u + 1, :]
        o_ref[0, 0, ci] = o + od.reshape(c, GLA_DV)
        kd_t = kt * jnp.exp(last_t - cum_t)
        s_scr[...] = jnp.exp(last_t) * s0 + _dot(kd_t.astype(BF16), vb)
        return 0

    lax.fori_loop(0, nch, chunk, 0)

    @pl.when(ti == pl.num_programs(2) - 1)
    def _():
        s_ref[0, 0] = s_scr[...]


def _gla_prompt(q, k, la, kt, lat, v, nch):
    b, h, nc, c, dk = q.shape
    dv = v.shape[-1]
    spec = lambda d1, d2: pl.BlockSpec((1, 1, nch, d1, d2), lambda bi, hi, ti: (bi, hi, ti, 0, 0))
    return pl.pallas_call(
        functools.partial(_gla_prompt_kernel, nch=nch),
        grid=(b, h, nc // nch),
        in_specs=[spec(c, dk), spec(c, dk), spec(c, dk), spec(dk, c), spec(dk, c), spec(c, dv)],
        out_specs=[spec(c, dv), pl.BlockSpec((1, 1, dk, dv), lambda bi, hi, ti: (bi, hi, 0, 0))],
        out_shape=[jax.ShapeDtypeStruct((b, h, nc, c, dv), F32), jax.ShapeDtypeStruct((b, h, dk, dv), F32)],
        scratch_shapes=[pltpu.VMEM((dk, dv), F32)],
        compiler_params=_cparams(("parallel", "parallel", "arbitrary")),
        name="gla_prompt",
    )(q, k, la, kt, lat, v)


def _out_kernel(x_ref, gt_ref, on_ref, ogla_ref, og_ref, gn_ref, w_ref, o_ref):
    y = jnp.zeros(x_ref.shape[1:], F32)
    for h in range(NSA_HEADS):
        y = y + _dot(on_ref[0, h].astype(BF16), w_ref[h * NSA_HD:(h + 1) * NSA_HD, :])
    og = og_ref[0]
    for h in range(GLA_HEADS):
        o = ogla_ref[0, h]
        o = o * lax.rsqrt(jnp.mean(o * o, axis=-1, keepdims=True) + EPS) * gn_ref[...]
        gate = og[:, h * GLA_DV:(h + 1) * GLA_DV]
        o = o * (gate * jax.nn.sigmoid(gate))
        y = y + _dot(o.astype(BF16), w_ref[NSA_WIDTH + h * GLA_DV:NSA_WIDTH + (h + 1) * GLA_DV, :])
    o_ref[0] = x_ref[0] + gt_ref[0] * y


def _out_proj(x, gt, o_nsa, o_gla, o_g, gla_norm, w_out, tm):
    bx, tx, d = x.shape
    xs = pl.BlockSpec((1, tm, d), lambda b, i: (b, i, 0))
    return pl.pallas_call(
        _out_kernel,
        grid=(bx, tx // tm),
        in_specs=[xs, _mod_spec(gt, tm, 2),
                  pl.BlockSpec((1, NSA_HEADS, tm, NSA_HD), lambda b, i: (b, 0, i, 0)),
                  pl.BlockSpec((1, GLA_HEADS, tm, GLA_DV), lambda b, i: (b, 0, i, 0)),
                  pl.BlockSpec((1, tm, GLA_WIDTH), lambda b, i: (b, i, 0)),
                  pl.BlockSpec((1, GLA_DV), lambda b, i: (0, 0)),
                  pl.BlockSpec(w_out.shape, lambda b, i: (0, 0))],
        out_specs=xs,
        out_shape=jax.ShapeDtypeStruct(x.shape, F32),
        compiler_params=_cparams(("parallel", "parallel")),
        name="out_proj",
    )(x, gt, o_nsa, o_gla, o_g, gla_norm, w_out)


def _norm_kernel(x_ref, g_ref, o_ref):
    x = x_ref[0]
    o_ref[0] = x * lax.rsqrt(jnp.mean(x * x, axis=-1, keepdims=True) + EPS) * g_ref[...]


def _final_norm(x, g, tm):
    bx, tx, d = x.shape
    xs = pl.BlockSpec((1, tm, d), lambda b, i: (b, i, 0))
    return pl.pallas_call(
        _norm_kernel,
        grid=(bx, tx // tm),
        in_specs=[xs, pl.BlockSpec((1, d), lambda b, i: (0, 0))],
        out_specs=xs,
        out_shape=jax.ShapeDtypeStruct(x.shape, F32),
        compiler_params=_cparams(("parallel", "parallel")),
        name="final_norm",
    )(x, g)


def _nsa_sample_kernel(pt_ref, qall_ref, gate_ref, kvc_ref, kvs_ref, kvw_ref, *rest, n_pages, page, past, wlen):
    del pt_ref
    cmp_pages = rest[:n_pages]
    slc_pages = rest[n_pages:2 * n_pages]
    (cw_ref, w1lo_ref, w1hi_ref, pelo_ref, pehi_ref, b1_ref, w2_ref, ov_ref, o_ref) = rest[2 * n_pages:]
    nh = NSA_HEADS
    ncb = past // CMP_STRIDE
    nkv = 2 * LANES
    qa = qall_ref[0]
    qa_f = qa.astype(F32)
    hrow = lax.broadcasted_iota(jnp.int32, (nh, 1), 0)
    slope = jnp.exp2(-(hrow + 1).astype(F32)) * LOG2E
    lane = lax.broadcasted_iota(jnp.int32, (1, LANES), 1)

    tail_row = jnp.concatenate([kvc_ref[0], jnp.zeros((1, CMP_STRIDE * nkv - nkv), F32)], axis=1)
    tail = jnp.where(lax.broadcasted_iota(jnp.int32, (ncb, 1), 0) == 0, tail_row, 0.0)
    x = jnp.concatenate([p[0, 0] for p in cmp_pages] + [tail], axis=0)
    h_lo = _dot_nt(w1lo_ref[...], (x + pelo_ref[...]).astype(BF16))
    h_hi = _dot_nt(w1hi_ref[...], (x + pehi_ref[...]).astype(BF16))
    h = h_lo[:, :ncb] + pltpu.roll(h_hi, 2 * ncb - 1, 1)[:, :ncb] + b1_ref[...]
    ckv_t = _dot(w2_ref[...], jax.nn.gelu(h).astype(BF16))
    ck_t = ckv_t[:LANES].astype(BF16)
    cv_t = ckv_t[LANES:].astype(BF16)
    c_end = lane * CMP_STRIDE + (CMP_BLOCK - 1)
    dist_c = past - c_end
    mask_c = dist_c >= 0
    lg_c = jnp.where(mask_c, _dot(qa, ck_t) - slope * dist_c.astype(F32), NEG)
    e_c = jnp.where(mask_c, jnp.exp2(lg_c - jnp.max(lg_c, axis=-1, keepdims=True)), 0.0)
    pc = e_c / jnp.maximum(jnp.sum(e_c, axis=-1, keepdims=True), 1e-30)
    o_cmp = _dot_nt(pc.astype(BF16), cv_t)

    ps0 = jnp.sum(jnp.where(hrow < NSA_GROUP, pc, 0.0), axis=0, keepdims=True)
    ps1 = jnp.sum(jnp.where(hrow >= NSA_GROUP, pc, 0.0), axis=0, keepdims=True)
    psum = jnp.where(hrow == 0, ps0, jnp.where(hrow == 1, ps1, 0.0))
    p_hi = psum.astype(BF16)
    p_lo = (psum - p_hi.astype(F32)).astype(BF16)
    imp = _dot(p_hi, ov_ref[...]) + _dot(p_lo, ov_ref[...])
    cur = past // SEL_BLOCK
    forced = jnp.where(lane == 0, 1.0, 0.0) + jnp.where(lane == cur, 1.0, 0.0) + jnp.where(lane == cur - 1, 1.0, 0.0)
    score = jnp.where(lane * SEL_BLOCK <= past, imp + jnp.where(forced > 0.5, FORCE_BONUS, 0.0), NEG)
    sel = _top_k_mask(score, lane.astype(F32), SEL_TOPK)
    sel_h = jnp.where(hrow < NSA_GROUP, sel[0:1], sel[1:2])

    def attend(tiles, new_row, new_neg):
        lgs = [_dot(qa, k_t) + neg for k_t, _, neg in tiles]
        lg_new = jnp.sum(qa_f * new_row[:, :LANES], axis=-1, keepdims=True) + new_neg
        m = lg_new
        for lg in lgs:
            m = jnp.maximum(m, jnp.max(lg, axis=-1, keepdims=True))
        p_new = jnp.exp2(lg_new - m)
        l = p_new
        acc = p_new * new_row[:, LANES:]
        for lg, (_, v_t, _) in zip(lgs, tiles):
            p = jnp.exp2(lg - m)
            l = l + jnp.sum(p, axis=-1, keepdims=True)
            acc = acc + _dot_nt(p.astype(BF16), v_t)
        return acc / l

    per = page // SEL_BLOCK
    tiles = []
    for pg in range(n_pages):
        blkp = slc_pages[pg]
        dist = past - (pg * page + lane)
        selx = jnp.zeros((nh, LANES), F32)
        for u in range(per):
            in_u = jnp.where(lane >= u * SEL_BLOCK, jnp.where(lane < (u + 1) * SEL_BLOCK, 1.0, 0.0), 0.0)
            selx = selx + in_u * sel_h[:, pg * per + u:pg * per + u + 1]
        neg = jnp.where(dist >= 0, (selx - 1.0) * 1e30, NEG) - slope * dist.astype(F32)
        tiles.append((blkp[0, 0, 0].reshape(LANES, page).astype(BF16),
                      blkp[0, 0, 1].reshape(LANES, page).astype(BF16), neg))
    o_slc = attend(tiles, kvs_ref[0], (sel_h[:, cur:cur + 1] - 1.0) * 1e30)

    wlane = lax.broadcasted_iota(jnp.int32, (1, wlen), 1)
    w_pos = past - wlen + wlane
    dist = past - w_pos
    ok = jnp.where(dist >= 0, jnp.where(dist < WINDOW, jnp.where(w_pos >= 0, 1.0, 0.0), 0.0), 0.0)
    neg = (ok - 1.0) * 1e30 - slope * dist.astype(F32)
    tiles = [(cw_ref[0, 0, 0].reshape(LANES, wlen).astype(BF16), cw_ref[0, 0, 1].reshape(LANES, wlen).astype(BF16), neg)]
    o_win = attend(tiles, kvw_ref[0], jnp.zeros((nh, 1), F32))

    gates = gate_ref[0]
    pick = lambda a: jnp.where(hrow < NSA_GROUP, a[:, :NSA_HD], a[:, NSA_HD:])
    o_ref[0] = gates[:, 0:1] * pick(o_cmp) + gates[:, 1:2] * pick(o_slc) + gates[:, 2:3] * pick(o_win)


def _nsa_sample(layer, page_table, q_all, gates, kvc, kvs, kvw, cache_c, cache_s, cache_w, cw):
    nb, n_pages = page_table.shape
    depth, n_phys, page = cache_c.shape[:3]
    past = n_pages * page
    wlen = cache_w.shape[2]
    ncb = past // CMP_STRIDE
    cpp = page // CMP_STRIDE
    cc = cache_c.reshape(depth, n_phys, cpp, CMP_STRIDE * KV_COLS)
    cs = cache_s.transpose(0, 1, 3, 4, 5, 2)
    cwin = cache_w.transpose(0, 1, 3, 4, 5, 2)
    row = lambda w: pl.BlockSpec((1, 1, w), lambda b, pt: (b, 0, 0))
    full = lambda a: pl.BlockSpec(a.shape, lambda b, pt: (0,) * a.ndim)
    cmp_specs = [pl.BlockSpec((1, 1, cpp, CMP_STRIDE * KV_COLS), lambda b, pt, j=j: (layer, pt[b, j], 0, 0))
                 for j in range(n_pages)]
    slc_specs = [pl.BlockSpec((1, 1, 2, NSA_KV_HEADS, NSA_HD, page), lambda b, pt, j=j: (layer, pt[b, j], 0, 0, 0, 0))
                 for j in range(n_pages)]
    ov = _overlap_matrix(ncb, LANES)
    wargs = (cw["w1lo_t"], cw["w1hi_t"], cw["pe_lo"], cw["pe_hi"], cw["b1_col"], cw["w2_t"], ov)
    grid_spec = pltpu.PrefetchScalarGridSpec(
        num_scalar_prefetch=1,
        grid=(nb,),
        in_specs=[pl.BlockSpec((1, NSA_HEADS, LANES), lambda b, pt: (b, 0, 0)),
                  pl.BlockSpec((1, NSA_HEADS, LANES), lambda b, pt: (b, 0, 0)),
                  row(KV_COLS), row(KV_COLS), row(KV_COLS)] + cmp_specs + slc_specs
                 + [pl.BlockSpec((1, 1, 2, NSA_KV_HEADS, NSA_HD, wlen), lambda b, pt: (layer, b, 0, 0, 0, 0))]
                 + [full(a) for a in wargs],
        out_specs=pl.BlockSpec((1, NSA_HEADS, NSA_HD), lambda b, pt: (b, 0, 0)),
    )
    return pl.pallas_call(
        functools.partial(_nsa_sample_kernel, n_pages=n_pages, page=page, past=past, wlen=wlen),
        grid_spec=grid_spec,
        out_shape=jax.ShapeDtypeStruct((nb, NSA_HEADS, NSA_HD), F32),
        compiler_params=_cparams(("arbitrary",)),
        name="nsa_sample",
    )(page_table, q_all, gates, kvc, kvs, kvw, *([cc] * n_pages), *([cs] * n_pages), cwin, *wargs)


def _gla_sample_kernel(col_ref, v_ref, s_ref, o_ref, sn_ref, *, bb):
    for b in range(bb):
        cols = col_ref[b]
        for h in range(GLA_HEADS):
            a = jnp.exp(cols[:, h:h + 1])
            k = cols[:, GLA_HEADS + h:GLA_HEADS + h + 1]
            q = cols[:, 2 * GLA_HEADS + h:2 * GLA_HEADS + h + 1]
            v = v_ref[b, h:h + 1, :]
            s0 = s_ref[b, h]
            qk = jnp.sum(q * k, axis=0, keepdims=True)
            o_ref[b, h:h + 1, :] = jnp.sum((q * a) * s0, axis=0, keepdims=True) + qk * v
            sn_ref[b, h] = a * s0 + k * v


def _gla_sample(cols, v, state, bb):
    nb = v.shape[0]
    return pl.pallas_call(
        functools.partial(_gla_sample_kernel, bb=bb),
        grid=(nb // bb,),
        in_specs=[pl.BlockSpec((bb, GLA_DK, 16), lambda i: (i, 0, 0)),
                  pl.BlockSpec((bb, GLA_HEADS, GLA_DV), lambda i: (i, 0, 0)),
                  pl.BlockSpec((bb, GLA_HEADS, GLA_DK, GLA_DV), lambda i: (i, 0, 0, 0))],
        out_specs=[pl.BlockSpec((bb, GLA_HEADS, GLA_DV), lambda i: (i, 0, 0)),
                   pl.BlockSpec((bb, GLA_HEADS, GLA_DK, GLA_DV), lambda i: (i, 0, 0, 0))],
        out_shape=[jax.ShapeDtypeStruct((nb, GLA_HEADS, GLA_DV), F32),
                   jax.ShapeDtypeStruct((nb, GLA_HEADS, GLA_DK, GLA_DV), F32)],
        compiler_params=_cparams(("parallel",)),
        name="gla_sample",
    )(cols, v, state)


def _pad_cols(w, sizes, pads):
    parts, off = [], 0
    for s, p in zip(sizes, pads):
        parts.append(jnp.pad(w[:, off:off + s], ((0, 0), (0, p - s))))
        off += s
    return jnp.concatenate(parts, axis=1)


def _cmp_weights(pe, w1, b1, w2):
    g = NSA_KV_HEADS
    eye = jnp.eye(2, dtype=F32)
    eg = jnp.eye(g, dtype=F32)
    w1r = w1.reshape(2, 2, CMP_STRIDE, NSA_HD, CMP_HIDDEN)
    big = jnp.einsum("krsdh,ka,gb->rskgdabh", w1r, eye, eg).reshape(2, CMP_STRIDE * KV_COLS, 2 * g * CMP_HIDDEN)
    w2big = jnp.einsum("khd,ka,gb->kghabd", w2, eye, eg).reshape(2 * g * CMP_HIDDEN, KV_COLS)
    per = pe.reshape(2, 2, CMP_STRIDE, NSA_HD)
    pet = jnp.broadcast_to(per.transpose(1, 2, 0, 3)[:, :, :, None, :], (2, CMP_STRIDE, 2, g, NSA_HD))
    pet = pet.reshape(2, 1, CMP_STRIDE * KV_COLS)
    b1big = jnp.broadcast_to(b1[:, None, :], (2, g, CMP_HIDDEN)).reshape(1, 2 * g * CMP_HIDDEN)
    return {
        "w1lo": big[0].astype(BF16), "w1hi": big[1].astype(BF16),
        "w1lo_t": big[0].T.astype(BF16), "w1hi_t": big[1].T.astype(BF16),
        "pe_lo": pet[0], "pe_hi": pet[1], "b1": b1big, "b1_col": b1big.T,
        "w2": w2big.astype(BF16), "w2_t": w2big.T.astype(BF16),
    }


def _heads_first(a, hd):
    b, t, w = a.shape
    return a.reshape(b, t, w // hd, hd).transpose(0, 2, 1, 3)


def _kv_split(a):
    b, t, _ = a.shape
    return a.reshape(b, t, 2, NSA_KV_HEADS, NSA_HD).transpose(0, 2, 3, 1, 4).astype(BF16)


def _mods(mod, rows):
    m = mod.reshape(mod.shape[0], N_MOD, D_MODEL)
    if rows == 1:
        return [m[:, j][:, None, :] for j in range(N_MOD)]
    return [m[:, j][None, :, :] for j in range(N_MOD)]


def _layer_prompt(x, mods, lw, tm, tq, tk):
    b, t, _ = x.shape
    x = _ffn(x, mods[0], mods[1], mods[2], lw["norm_g"][0:1], lw["f1_in"], lw["f1_out"], tm)
    (qn, kvc, _, _, kvct, kvst, kvwt, kvtb, gn, qg, kg, vg, la, og) = _proj(
        x, mods[3], mods[4], lw["norm_g"][1:2], lw["w_all"], lw["wkvt"], lw["wa2p"], lw["ba"], tm)
    g, r = NSA_KV_HEADS, NSA_GROUP
    ncp = t // CMP_STRIDE
    ckvt = _cmp_prompt(kvc.reshape(b, ncp, CMP_STRIDE * KV_COLS), lw["cmp"]).astype(BF16)
    ckvt = ckvt.reshape(b, 2, g, NSA_HD, ncp)
    kvs5 = kvtb[:, :KV_COLS].reshape(b, 2, g, NSA_HD, t)
    kvw5 = kvtb[:, KV_COLS:].reshape(b, 2, g, NSA_HD, t)
    pos_rows = _alibi_k_rows(np.arange(t))
    ones_row = jnp.ones((1, t), BF16)
    onehot = jnp.asarray(np.arange(max(LANES, t // SEL_BLOCK))[:, None] == np.arange(t)[None, :] // SEL_BLOCK, BF16)
    lead = (b, g)
    ck = _aug_rows([ckvt[:, 0], _alibi_k_rows(np.arange(ncp) * CMP_STRIDE + CMP_BLOCK - 1)], LANES, lead)
    ks = _aug_rows([onehot, kvs5[:, 0], pos_rows], onehot.shape[0] + LANES, lead)
    vs = _aug_rows([kvs5[:, 1], ones_row], LANES, lead)
    kw = _aug_rows([kvw5[:, 0], pos_rows], LANES, lead)
    vw = _aug_rows([kvw5[:, 1], ones_row], LANES, lead)
    q5 = qn.reshape(b, t, g, r, NSA_HD).transpose(0, 2, 3, 1, 4)
    qcols = jnp.broadcast_to(_alibi_q_cols().reshape(1, g, r, 1, 4), (b, g, r, t, 4))
    q_pad = jnp.concatenate([q5, qcols, jnp.zeros((b, g, r, t, LANES - NSA_HD - 4), BF16)], axis=-1)
    gates = gn[:, :, :3 * NSA_HEADS].reshape(b, t, g, 3 * r).transpose(0, 2, 1, 3)
    gates = jnp.pad(gates, ((0, 0), (0, 0), (0, 0), (0, 16 - 3 * r)))
    o_nsa = _nsa_prompt(q_pad, gates, ck, ckvt[:, 1], ks, vs, kw, vw, tq, tk)
    o_nsa = o_nsa.reshape(b, NSA_HEADS, t, NSA_HD)
    nc = t // GLA_CHUNK
    ch = lambda a, hd: _heads_first(a, hd).reshape(b, GLA_HEADS, nc, GLA_CHUNK, hd)
    k5, la5 = ch(kg, GLA_DK), ch(la, GLA_DK)
    o_gla, s_fin = _gla_prompt(ch(qg, GLA_DK), k5, la5, k5.swapaxes(-1, -2), la5.swapaxes(-1, -2),
                               ch(vg, GLA_DV), min(8, nc))
    o_gla = o_gla.reshape(b, GLA_HEADS, t, GLA_DV)
    x = _out_proj(x, mods[5], o_nsa, o_gla, og, lw["gla_norm"], lw["w_out"], tm)
    x = _ffn(x, mods[6], mods[7], mods[8], lw["norm_g"][2:3], lw["f2_in"], lw["f2_out"], tm)
    win = min(WINDOW, t)
    back = lambda a: a.reshape(b, 2, NSA_KV_HEADS, NSA_HD, a.shape[-1]).transpose(0, 4, 1, 2, 3)
    return x, (back(kvct), back(kvst), back(kvwt[:, :, t - win:]), s_fin)


def _layer_sample(layer, x, mods, lw, cache_c, cache_s, cache_w, state, page_table):
    nb = x.shape[1]
    x = _ffn(x, mods[0], mods[1], mods[2], lw["norm_g"][0:1], lw["f1_in"], lw["f1_out"], nb)
    (qn, kvc, kvs, kvw, kvct, kvst, kvwt, _, gn, qg, kg, vg, la, og) = _proj(
        x, mods[3], mods[4], lw["norm_g"][1:2], lw["w_all"], lw["wkvt"], lw["wa2p"], lw["ba"], nb)
    q4 = qn[0].reshape(nb, NSA_KV_HEADS, NSA_GROUP, NSA_HD)
    q_all = (q4[:, :, :, None, :] * jnp.eye(NSA_KV_HEADS, dtype=BF16)[None, :, None, :, None]).reshape(nb, NSA_HEADS, LANES)
    gates = jnp.pad(gn[0][:, :3 * NSA_HEADS].reshape(nb, NSA_HEADS, 3), ((0, 0), (0, 0), (0, LANES - 3)))
    r3 = lambda a: a[0][:, None, :]
    o_nsa = _nsa_sample(layer, page_table, q_all, gates, r3(kvc), r3(kvs), r3(kvw),
                        cache_c, cache_s, cache_w, lw["cmp"])
    o_nsa = o_nsa.transpose(1, 0, 2)[None]
    col = lambda a: a[0].reshape(nb, GLA_HEADS, GLA_DK).transpose(0, 2, 1)
    cols = jnp.concatenate([col(la), col(kg), col(qg), jnp.zeros((nb, GLA_DK, 16 - 3 * GLA_HEADS), F32)], axis=-1)
    o_gla, s_new = _gla_sample(cols, vg[0].reshape(nb, GLA_HEADS, GLA_DV), state, 8)
    o_gla = o_gla.transpose(1, 0, 2)[None]
    x = _out_proj(x, mods[5], o_nsa, o_gla, og, lw["gla_norm"], lw["w_out"], nb)
    x = _ffn(x, mods[6], mods[7], mods[8], lw["norm_g"][2:3], lw["f2_in"], lw["f2_out"], nb)
    kv5 = lambda a: a[0].reshape(2, NSA_KV_HEADS, NSA_HD, nb).transpose(3, 0, 1, 2)[:, None]
    return x, (kv5(kvct), kv5(kvst), kv5(kvwt), s_new)


def kernel(x_prompt, x_sample, c_prompt, c_sample, cache_kv_cmp, cache_kv_slc, cache_kv_win, state_gla, page_table, w_ada, b_ada, norm_g, ffn1_w_in, ffn1_w_out, w_in, cmp_pe, cmp_w1, cmp_b1, cmp_w2, gla_wa2, gla_ba, gla_norm, w_out, ffn2_w_in, ffn2_w_out, final_norm):
    depth = w_ada.shape[0]
    b, t, d = x_prompt.shape
    nb = x_sample.shape[0]
    bp = -(-b // 8) * 8
    c_all = jnp.concatenate([c_prompt, jnp.zeros((bp - b, d), F32), c_sample], axis=0)
    mod_all = _ada(c_all, w_ada, b_ada)

    tm = min(512, t)
    tq = min(256, t)
    tk = min(512, t)
    xp = x_prompt
    xs = x_sample.reshape(1, nb, d)
    outs_p, outs_s = [], []
    for l in range(depth):
        lw = {
            "norm_g": norm_g[l],
            "f1_in": ffn1_w_in[l].astype(BF16), "f1_out": ffn1_w_out[l].astype(BF16),
            "f2_in": ffn2_w_in[l].astype(BF16), "f2_out": ffn2_w_out[l].astype(BF16),
            "w_all": _pad_cols(w_in[l], IN_SIZES, IN_PAD).astype(BF16),
            "wkvt": w_in[l][:, NSA_WIDTH:NSA_WIDTH + 3 * KV_COLS].T.astype(BF16),
            "wa2p": jnp.pad(gla_wa2[l], ((0, 128 - GLA_RANK), (0, 0))).astype(BF16),
            "ba": gla_ba[l][None, :],
            "gla_norm": gla_norm[l][None, :],
            "w_out": w_out[l].astype(BF16),
            "cmp": _cmp_weights(cmp_pe[l], cmp_w1[l], cmp_b1[l], cmp_w2[l]),
        }
        xp, st_p = _layer_prompt(xp, _mods(mod_all[l, :b], 1), lw, tm, tq, tk)
        xs, st_s = _layer_sample(l, xs, _mods(mod_all[l, bp:], nb), lw, cache_kv_cmp, cache_kv_slc,
                                 cache_kv_win, state_gla[l], page_table)
        outs_p.append(st_p)
        outs_s.append(st_s)
    fn = final_norm[None, :]
    y_prompt = _final_norm(xp, fn, tm)
    y_sample = _final_norm(xs, fn, nb).reshape(nb, 1, d)
    stack = lambda outs, j: jnp.stack([o[j] for o in outs])
    return (y_prompt, y_sample,
            stack(outs_p, 0), stack(outs_p, 1), stack(outs_p, 2), stack(outs_p, 3),
            stack(outs_s, 0), stack(outs_s, 1), stack(outs_s, 2), stack(outs_s, 3))
```

```python
import functools

import numpy as np
import jax
import jax.numpy as jnp
from jax import lax
from jax.experimental import pallas as pl
from jax.experimental.pallas import tpu as pltpu

F32 = jnp.float32
BF16 = jnp.bfloat16

D_MODEL = 1024
NSA_HD = 64
NSA_HEADS = 8
NSA_KV_HEADS = 2
NSA_GROUP = 4
NSA_WIDTH = NSA_HEADS * NSA_HD
CMP_BLOCK = 32
CMP_STRIDE = 16
CMP_HIDDEN = 2 * NSA_HD
SEL_BLOCK = 64
SEL_TOPK = 16
WINDOW = 512
FORCE_BONUS = 1000.0
GLA_HEADS = 4
GLA_DV = 128
GLA_DK = 64
GLA_WIDTH = GLA_HEADS * GLA_DV
GLA_RANK = 16
GLA_TAU = 16.0
GLA_CHUNK = 64
GLA_SUB = 16
FFN_DIM = 2816
N_MOD = 9
EPS = 1e-6
KV_COLS = 2 * NSA_KV_HEADS * NSA_HD
IN_SIZES = (NSA_WIDTH, KV_COLS, KV_COLS, KV_COLS, 3 * NSA_HEADS,
            GLA_HEADS * GLA_DK, GLA_HEADS * GLA_DK, GLA_WIDTH, GLA_RANK, GLA_WIDTH)
IN_PAD = (NSA_WIDTH, KV_COLS, KV_COLS, KV_COLS, 128,
          GLA_HEADS * GLA_DK, GLA_HEADS * GLA_DK, GLA_WIDTH, 128, GLA_WIDTH)
IN_OFF = tuple(int(v) for v in np.cumsum((0,) + IN_PAD))

LANES = 128
NEG = -1e30
LOG2E = 1.4426950408889634
VMEM_LIMIT = 56 * 1024 * 1024
FFN_TF = FFN_DIM // 2


def _cparams(sem):
    return pltpu.CompilerParams(dimension_semantics=sem, vmem_limit_bytes=VMEM_LIMIT)


def _dot(a, b):
    return jnp.dot(a, b, preferred_element_type=F32)


def _dot_nt(a, b):
    return lax.dot_general(a, b, (((1,), (1,)), ((), ())), preferred_element_type=F32)


def _split3(x):
    x1 = x.astype(BF16)
    r = x - x1.astype(F32)
    x2 = r.astype(BF16)
    x3 = (r - x2.astype(F32)).astype(BF16)
    return x1, x2, x3


def _rms_mod(x, g, sc, sh):
    y = x * lax.rsqrt(jnp.mean(x * x, axis=-1, keepdims=True) + EPS) * g
    return y * (1.0 + sc) + sh


def _masked_softmax(lg, mask):
    lg = jnp.where(mask, lg, NEG)
    m = jnp.max(lg, axis=-1, keepdims=True)
    e = jnp.where(mask, jnp.exp(lg - m), 0.0)
    return e / jnp.maximum(jnp.sum(e, axis=-1, keepdims=True), 1e-30)


def _top_k_mask(score, blk, k, axis=-1):
    sel = jnp.zeros_like(score)
    big = float(score.shape[axis])
    for _ in range(k):
        m = jnp.max(score, axis=axis, keepdims=True)
        idx = jnp.min(jnp.where(score == m, blk, big), axis=axis, keepdims=True)
        hit = blk == idx
        sel = jnp.where(hit, 1.0, sel)
        score = jnp.where(hit, -jnp.inf, score)
    return sel


def _ada_kernel(c_ref, w_ref, b_ref, o_ref):
    c = c_ref[...]
    s = c * jax.nn.sigmoid(c)
    o_ref[0] = _dot(s.astype(BF16), w_ref[0].astype(BF16)) + b_ref[0]


def _ada(c_all, w_ada, b_ada):
    depth, d, n = w_ada.shape
    m = c_all.shape[0]
    tn = 1024
    return pl.pallas_call(
        _ada_kernel,
        grid=(depth, n // tn),
        in_specs=[pl.BlockSpec((m, d), lambda l, j: (0, 0)),
                  pl.BlockSpec((1, d, tn), lambda l, j: (l, 0, j)),
                  pl.BlockSpec((1, 1, tn), lambda l, j: (l, 0, j))],
        out_specs=pl.BlockSpec((1, m, tn), lambda l, j: (l, 0, j)),
        out_shape=jax.ShapeDtypeStruct((depth, m, n), F32),
        compiler_params=_cparams(("parallel", "parallel")),
        name="ada",
    )(c_all, w_ada, b_ada.reshape(depth, 1, n))


def _mod_spec(mod, tm, nargs):
    r = mod.shape[1]
    d = mod.shape[2]
    if nargs == 3:
        if r == 1:
            return pl.BlockSpec((1, 1, d), lambda b, i, f: (b, 0, 0))
        return pl.BlockSpec((1, tm, d), lambda b, i, f: (b, i, 0))
    if r == 1:
        return pl.BlockSpec((1, 1, d), lambda b, i: (b, 0, 0))
    return pl.BlockSpec((1, tm, d), lambda b, i: (b, i, 0))


def _ffn_kernel(x_ref, sh_ref, sc_ref, gt_ref, g_ref, wg_ref, wu_ref, wo_ref, o_ref, h_scr, acc_scr):
    f = pl.program_id(2)

    @pl.when(f == 0)
    def _():
        h = _rms_mod(x_ref[0], g_ref[...], sc_ref[0], sh_ref[0])
        h_scr[...] = h.astype(BF16)
        acc_scr[...] = jnp.zeros_like(acc_scr)

    h = h_scr[...]
    g = _dot(h, wg_ref[...])
    u = _dot(h, wu_ref[...])
    a = (g * jax.nn.sigmoid(g) * u).astype(BF16)
    acc_scr[...] += _dot(a, wo_ref[...])

    @pl.when(f == pl.num_programs(2) - 1)
    def _():
        o_ref[0] = x_ref[0] + 0.5 * gt_ref[0] * acc_scr[...]


def _ffn(x, sh, sc, gt, g, w_in, w_out, tm):
    bx, tx, d = x.shape
    nf = FFN_DIM // FFN_TF
    xs = pl.BlockSpec((1, tm, d), lambda b, i, f: (b, i, 0))
    ms = _mod_spec(sh, tm, 3)
    return pl.pallas_call(
        _ffn_kernel,
        grid=(bx, tx // tm, nf),
        in_specs=[xs, ms, ms, ms,
                  pl.BlockSpec((1, d), lambda b, i, f: (0, 0)),
                  pl.BlockSpec((d, FFN_TF), lambda b, i, f: (0, f)),
                  pl.BlockSpec((d, FFN_TF), lambda b, i, f: (0, f + nf)),
                  pl.BlockSpec((FFN_TF, d), lambda b, i, f: (f, 0))],
        out_specs=xs,
        out_shape=jax.ShapeDtypeStruct(x.shape, F32),
        scratch_shapes=[pltpu.VMEM((tm, d), BF16), pltpu.VMEM((tm, d), F32)],
        compiler_params=_cparams(("parallel", "parallel", "arbitrary")),
        name="ffn",
    )(x, sh, sc, gt, g, w_in, w_in, w_out)


def _proj_kernel(x_ref, sh_ref, sc_ref, g_ref, w_ref, wkvt_ref, wa2_ref, ba_ref,
                 qn_ref, kvc_ref, kvs_ref, kvw_ref, kvct_ref, kvst_ref, kvwt_ref, kvtb_ref,
                 gn_ref, qg_ref, kg_ref, vg_ref, la_ref, og_ref):
    h = _rms_mod(x_ref[0], g_ref[...], sc_ref[0], sh_ref[0]).astype(BF16)
    p = _dot(h, w_ref[...])
    o = IN_OFF
    qn_ref[0] = (p[:, o[0]:o[1]] * (NSA_HD ** -0.5 * LOG2E)).astype(BF16)
    kvc_ref[0] = p[:, o[1]:o[2]]
    kvs_ref[0] = p[:, o[2]:o[3]]
    kvw_ref[0] = p[:, o[3]:o[4]]
    kvt = _dot_nt(wkvt_ref[...], h)
    kvct_ref[0] = kvt[0:KV_COLS]
    kvst_ref[0] = kvt[KV_COLS:2 * KV_COLS]
    kvwt_ref[0] = kvt[2 * KV_COLS:3 * KV_COLS]
    kvtb_ref[0] = kvt[KV_COLS:3 * KV_COLS].astype(BF16)
    gn_ref[0] = jax.nn.sigmoid(p[:, o[4]:o[5]])
    qg_ref[0] = p[:, o[5]:o[6]] * (GLA_DK ** -0.5)
    kg_ref[0] = p[:, o[6]:o[7]]
    vg_ref[0] = p[:, o[7]:o[8]]
    a_pre = _dot(p[:, o[8]:o[9]].astype(BF16), wa2_ref[...]) + ba_ref[...]
    log_sig = jnp.minimum(a_pre, 0.0) - jnp.log(1.0 + jnp.exp(-jnp.abs(a_pre)))
    la_ref[0] = log_sig * (1.0 / GLA_TAU)
    og_ref[0] = p[:, o[9]:o[10]]


def _proj(x, sh, sc, g, w_all, wkvt, wa2p, ba, tm):
    bx, tx, d = x.shape
    xs = pl.BlockSpec((1, tm, d), lambda b, i: (b, i, 0))
    ms = _mod_spec(sh, tm, 2)
    full = lambda shape: pl.BlockSpec(shape, lambda b, i: (0,) * len(shape))
    rows = lambda w, dt: (pl.BlockSpec((1, tm, w), lambda b, i: (b, i, 0)), jax.ShapeDtypeStruct((bx, tx, w), dt))
    cols = lambda w, dt: (pl.BlockSpec((1, w, tm), lambda b, i: (b, 0, i)), jax.ShapeDtypeStruct((bx, w, tx), dt))
    outs = [rows(NSA_WIDTH, BF16), rows(KV_COLS, F32), rows(KV_COLS, F32), rows(KV_COLS, F32),
            cols(KV_COLS, F32), cols(KV_COLS, F32), cols(KV_COLS, F32), cols(2 * KV_COLS, BF16),
            rows(128, F32), rows(256, F32), rows(256, F32), rows(GLA_WIDTH, F32), rows(256, F32),
            rows(GLA_WIDTH, F32)]
    return pl.pallas_call(
        _proj_kernel,
        grid=(bx, tx // tm),
        in_specs=[xs, ms, ms, full((1, d)), full(w_all.shape), full(wkvt.shape), full(wa2p.shape),
                  full(ba.shape)],
        out_specs=[o[0] for o in outs],
        out_shape=[o[1] for o in outs],
        compiler_params=_cparams(("parallel", "parallel")),
        name="proj",
    )(x, sh, sc, g, w_all, wkvt, wa2p, ba)


def _cmp_kernel(x_ref, pelo_ref, pehi_ref, w1lo_ref, w1hi_ref, b1_ref, w2_ref, o_ref):
    x = x_ref[0]
    n = x.shape[0]
    h_lo = _dot_nt(w1lo_ref[...], (x + pelo_ref[...]).astype(BF16))
    h_hi = _dot_nt(w1hi_ref[...], (x + pehi_ref[...]).astype(BF16))
    h = h_lo + pltpu.roll(h_hi, n - 1, 1) + b1_ref[...]
    o_ref[0] = _dot(w2_ref[...], jax.nn.gelu(h).astype(BF16))


def _cmp_prompt(x16, cw):
    b, n, w = x16.shape
    full = lambda a: pl.BlockSpec(a.shape, lambda i: (0,) * a.ndim)
    args = (cw["pe_lo"], cw["pe_hi"], cw["w1lo_t"], cw["w1hi_t"], cw["b1_col"], cw["w2_t"])
    return pl.pallas_call(
        _cmp_kernel,
        grid=(b,),
        in_specs=[pl.BlockSpec((1, n, w), lambda i: (i, 0, 0))] + [full(a) for a in args],
        out_specs=pl.BlockSpec((1, KV_COLS, n), lambda i: (i, 0, 0)),
        out_shape=jax.ShapeDtypeStruct((b, KV_COLS, n), F32),
        compiler_params=_cparams(("parallel",)),
        name="cmp_prompt",
    )(x16, *args)


def _nsa_prompt_kernel(q_ref, gate_ref, ck_ref, cv_ref, ks_ref, vs_ref, kw_ref, vw_ref, ovt_ref,
                       o_ref, ids_scr, *, tq, tk, ncp, nsp):
    i = pl.program_id(2)
    q0 = i * tq
    nr = NSA_GROUP * tq
    row = q0 + jnp.bitwise_and(lax.broadcasted_iota(jnp.int32, (nr, 1), 0), tq - 1)
    q = q_ref[0, 0].reshape(nr, LANES)

    c_end = lax.broadcasted_iota(jnp.int32, (1, ncp), 1) * CMP_STRIDE + (CMP_BLOCK - 1)
    mask_c = c_end <= row
    lg = jnp.where(mask_c, _dot(q, ck_ref[0, 0]), NEG)
    e = jnp.where(mask_c, jnp.exp2(lg - jnp.max(lg, axis=-1, keepdims=True)), 0.0)
    pc = e / jnp.maximum(jnp.sum(e, axis=-1, keepdims=True), 1e-30)
    o_cmp = _dot_nt(pc.astype(BF16), cv_ref[0, 0])
    psum = pc[0:tq]
    for r in range(1, NSA_GROUP):
        psum = psum + pc[r * tq:(r + 1) * tq]
    p_hi = psum.astype(BF16)
    p_lo = (psum - p_hi.astype(F32)).astype(BF16)
    imp_t = _dot_nt(ovt_ref[...], p_hi) + _dot_nt(ovt_ref[...], p_lo)

    blk = lax.broadcasted_iota(jnp.int32, (nsp, 1), 0)
    t_lane = q0 + lax.broadcasted_iota(jnp.int32, (1, tq), 1)
    cur = jnp.right_shift(t_lane, 6)
    forced = jnp.where(blk == 0, 1.0, 0.0) + jnp.where(blk == cur, 1.0, 0.0) + jnp.where(blk == cur - 1, 1.0, 0.0)
    valid = blk * SEL_BLOCK <= t_lane
    score = jnp.where(valid, imp_t + jnp.where(forced > 0.5, FORCE_BONUS, 0.0), NEG)
    sel_t = _top_k_mask(score, blk.astype(F32), SEL_TOPK, axis=0)
    sel_t = jnp.where(valid, sel_t, 0.0)
    sel_bias = ((sel_t - 1.0) * 1e30).T.astype(BF16)
    qa = jnp.concatenate([jnp.concatenate([sel_bias] * NSA_GROUP, axis=0), q], axis=1)

    n_full = q0 // tk
    per_tile = tk // SEL_BLOCK
    n_tiles = nsp // per_tile
    blk_any = jnp.max(sel_t, axis=1, keepdims=True)
    tile_any = jnp.max(blk_any.reshape(n_tiles, per_tile, 1), axis=1)
    bit = jnp.left_shift(1, lax.broadcasted_iota(jnp.int32, (n_tiles, 1), 0))
    tile_bits = jnp.sum(jnp.where(tile_any > 0.0, bit, 0))
    n_used = jnp.int32(0)
    for kt in range(min(n_tiles, ks_ref.shape[3] // tk)):
        use = jnp.logical_and(jnp.bitwise_and(jnp.right_shift(tile_bits, kt), 1) == 1, kt < n_full)
        ids_scr[n_used] = kt
        n_used = n_used + use.astype(jnp.int32)

    def sel_tile(kt, carry, diag):
        m, acc = carry
        k0 = pl.multiple_of(kt * tk, tk)
        lg = _dot(qa, ks_ref[0, 0, :, pl.ds(k0, tk)])
        if diag:
            lg = jnp.where(k0 + lax.broadcasted_iota(jnp.int32, (1, tk), 1) <= row, lg, NEG)
        m_new = jnp.maximum(m, jnp.max(lg, axis=-1, keepdims=True))
        p = jnp.exp2(lg - m_new).astype(BF16)
        return m_new, jnp.exp2(m - m_new) * acc + _dot_nt(p, vs_ref[0, 0, :, pl.ds(k0, tk)])

    carry = sel_tile(n_full, (jnp.full((nr, 1), NEG, F32), jnp.zeros((nr, LANES), F32)), True)
    _, acc = lax.fori_loop(0, n_used, lambda j, c: sel_tile(ids_scr[j], c, False), carry)
    o_slc = acc[:, :NSA_HD] / acc[:, NSA_HD:NSA_HD + 1]

    wl = WINDOW + tq
    w0 = pl.multiple_of(jnp.maximum(q0 - WINDOW, 0), tq)
    dist_w = row - (w0 + lax.broadcasted_iota(jnp.int32, (1, wl), 1))
    lg = _dot(q, kw_ref[0, 0, :, pl.ds(w0, wl)])
    lg = jnp.where(dist_w >= 0, jnp.where(dist_w < WINDOW, lg, NEG), NEG)
    p = jnp.exp2(lg - jnp.max(lg, axis=-1, keepdims=True)).astype(BF16)
    ow = _dot_nt(p, vw_ref[0, 0, :, pl.ds(w0, wl)])
    o_win = ow[:, :NSA_HD] / ow[:, NSA_HD:NSA_HD + 1]

    gates = gate_ref[0, 0]
    gcol = lambda j: jnp.concatenate([gates[:, 3 * r + j:3 * r + j + 1] for r in range(NSA_GROUP)], axis=0)
    o = gcol(0) * o_cmp + gcol(1) * o_slc + gcol(2) * o_win
    o_ref[0, 0] = o.reshape(NSA_GROUP, tq, NSA_HD)


def _overlap_matrix(ncp, nsp):
    c = np.arange(ncp)[:, None] * CMP_STRIDE
    s = np.arange(nsp)[None, :] * SEL_BLOCK
    return jnp.asarray((c <= s + SEL_BLOCK - 1) & (c + CMP_BLOCK - 1 >= s), dtype=BF16)


def _bf16_round(x):
    return np.asarray(x, dtype=BF16).astype(np.float32)


def _alibi_q_cols():
    slopes = np.exp2(-8.0 * np.arange(1, NSA_HEADS + 1) / NSA_HEADS)
    c_hi = float(_bf16_round(LOG2E))
    c_lo = float(_bf16_round(LOG2E - c_hi))
    cols = np.stack([c_hi * slopes * SEL_BLOCK, c_lo * slopes * SEL_BLOCK, c_hi * slopes, c_lo * slopes], axis=1)
    return jnp.asarray(cols, BF16)


def _alibi_k_rows(pos):
    pos = np.asarray(pos)
    return jnp.asarray(np.stack([pos // SEL_BLOCK, pos // SEL_BLOCK, pos % SEL_BLOCK, pos % SEL_BLOCK]), BF16)


def _aug_rows(parts, total, lead):
    parts = [jnp.broadcast_to(p, lead + p.shape[-2:]) for p in parts]
    used = sum(p.shape[-2] for p in parts)
    if total > used:
        parts.append(jnp.zeros(lead + (total - used, parts[0].shape[-1]), BF16))
    return jnp.concatenate(parts, axis=-2)


def _nsa_prompt(q_pad, gates, ck, cv, ks, vs, kw, vw, tq, tk):
    b, g, r, t, _ = q_pad.shape
    ncp = ck.shape[3]
    nsp = max(LANES, t // SEL_BLOCK)
    kv = lambda a: pl.BlockSpec((1, 1) + a.shape[2:], lambda bi, gi, i: (bi, gi, 0, 0))
    ovt = _overlap_matrix(ncp, nsp).T
    return pl.pallas_call(
        functools.partial(_nsa_prompt_kernel, tq=tq, tk=tk, ncp=ncp, nsp=nsp),
        grid=(b, g, t // tq),
        in_specs=[pl.BlockSpec((1, 1, r, tq, LANES), lambda bi, gi, i: (bi, gi, 0, i, 0)),
                  pl.BlockSpec((1, 1, tq, 16), lambda bi, gi, i: (bi, gi, i, 0)),
                  kv(ck), kv(cv), kv(ks), kv(vs), kv(kw), kv(vw),
                  pl.BlockSpec((nsp, ncp), lambda bi, gi, i: (0, 0))],
        out_specs=pl.BlockSpec((1, 1, r, tq, NSA_HD), lambda bi, gi, i: (bi, gi, 0, i, 0)),
        out_shape=jax.ShapeDtypeStruct((b, g, r, t, NSA_HD), F32),
        scratch_shapes=[pltpu.SMEM((nsp // (tk // SEL_BLOCK),), jnp.int32)],
        compiler_params=_cparams(("parallel", "parallel", "arbitrary")),
        name="nsa_prompt",
    )(q_pad, gates, ck, cv, ks, vs, kw, vw, ovt)


def _gla_prompt_kernel(q_ref, k_ref, la_ref, kt_ref, lat_ref, v_ref, o_ref, s_ref, s_scr, *, nch):
    c = GLA_CHUNK
    ti = pl.program_id(2)

    @pl.when(ti == 0)
    def _():
        s_scr[...] = jnp.zeros_like(s_scr)

    r_i = lax.broadcasted_iota(jnp.int32, (c, c), 0)
    c_i = lax.broadcasted_iota(jnp.int32, (c, c), 1)
    tril = jnp.where(c_i <= r_i, 1.0, 0.0).astype(BF16)
    triu = jnp.where(r_i <= c_i, 1.0, 0.0).astype(BF16)
    pos = lax.broadcasted_iota(jnp.int32, (c, 1), 0)
    nsub = c // GLA_SUB
    w_i = lax.broadcasted_iota(jnp.int32, (1, GLA_SUB, 1), 1)

    def chunk(ci, _):
        q = q_ref[0, 0, ci]
        k = k_ref[0, 0, ci]
        la = la_ref[0, 0, ci]
        kt = kt_ref[0, 0, ci]
        lat = lat_ref[0, 0, ci]
        v = v_ref[0, 0, ci]
        vb = v.astype(BF16)
        s0 = s_scr[...]
        a1, a2, a3 = _split3(la)
        cum = _dot(tril, a1) + _dot(tril, a2) + _dot(tril, a3)
        b1, b2, b3 = _split3(lat)
        cum_t = _dot(b1, triu) + _dot(b2, triu) + _dot(b3, triu)
        last_t = cum_t[:, c - 1:c]
        o = _dot((q * jnp.exp(cum)).astype(BF16), s0.astype(BF16))
        attn = jnp.zeros((c, c), F32)
        for j in range(nsub - 1):
            ce = cum[GLA_SUB * (j + 1) - 1:GLA_SUB * (j + 1), :]
            qh = jnp.where(pos >= GLA_SUB * (j + 1), q * jnp.exp(jnp.minimum(cum - ce, 0.0)), 0.0)
            in_j = jnp.where(pos >= GLA_SUB * j, jnp.where(pos < GLA_SUB * (j + 1), 1.0, 0.0), 0.0)
            kh = in_j * (k * jnp.exp(jnp.minimum(ce - cum, 0.0)))
            attn = attn + _dot_nt(qh.astype(BF16), kh.astype(BF16))
        o = o + _dot(attn.astype(BF16), vb)
        q3 = q.reshape(nsub, GLA_SUB, GLA_DK)
        k3 = k.reshape(nsub, GLA_SUB, GLA_DK)
        c3 = cum.reshape(nsub, GLA_SUB, GLA_DK)
        v3 = v.reshape(nsub, GLA_SUB, GLA_DV)
        od = jnp.zeros((nsub, GLA_SUB, GLA_DV), F32)
        for u in range(GLA_SUB):
            w = q3 * k3[:, u:u + 1, :] * jnp.exp(jnp.minimum(c3 - c3[:, u:u + 1, :], 0.0))
            a = jnp.sum(jnp.where(w_i >= u, w, 0.0), axis=-1, keepdims=True)
            od = od + a * v3[:, u:u + 1, :]
        o_ref[0, 0, ci] = o + od.reshape(c, GLA_DV)
        kd_t = kt * jnp.exp(last_t - cum_t)
        s_scr[...] = jnp.exp(last_t) * s0 + _dot(kd_t.astype(BF16), vb)
        return 0

    lax.fori_loop(0, nch, chunk, 0)

    @pl.when(ti == pl.num_programs(2) - 1)
    def _():
        s_ref[0, 0] = s_scr[...]


def _gla_prompt(q, k, la, kt, lat, v, nch):
    b, h, nc, c, dk = q.shape
    dv = v.shape[-1]
    spec = lambda d1, d2: pl.BlockSpec((1, 1, nch, d1, d2), lambda bi, hi, ti: (bi, hi, ti, 0, 0))
    return pl.pallas_call(
        functools.partial(_gla_prompt_kernel, nch=nch),
        grid=(b, h, nc // nch),
        in_specs=[spec(c, dk), spec(c, dk), spec(c, dk), spec(dk, c), spec(dk, c), spec(c, dv)],
        out_specs=[spec(c, dv), pl.BlockSpec((1, 1, dk, dv), lambda bi, hi, ti: (bi, hi, 0, 0))],
        out_shape=[jax.ShapeDtypeStruct((b, h, nc, c, dv), F32), jax.ShapeDtypeStruct((b, h, dk, dv), F32)],
        scratch_shapes=[pltpu.VMEM((dk, dv), F32)],
        compiler_params=_cparams(("parallel", "parallel", "arbitrary")),
        name="gla_prompt",
    )(q, k, la, kt, lat, v)


def _out_kernel(x_ref, gt_ref, on_ref, ogla_ref, og_ref, gn_ref, w_ref, o_ref):
    y = jnp.zeros(x_ref.shape[1:], F32)
    for h in range(NSA_HEADS):
        y = y + _dot(on_ref[0, h].astype(BF16), w_ref[h * NSA_HD:(h + 1) * NSA_HD, :])
    og = og_ref[0]
    for h in range(GLA_HEADS):
        o = ogla_ref[0, h]
        o = o * lax.rsqrt(jnp.mean(o * o, axis=-1, keepdims=True) + EPS) * gn_ref[...]
        gate = og[:, h * GLA_DV:(h + 1) * GLA_DV]
        o = o * (gate * jax.nn.sigmoid(gate))
        y = y + _dot(o.astype(BF16), w_ref[NSA_WIDTH + h * GLA_DV:NSA_WIDTH + (h + 1) * GLA_DV, :])
    o_ref[0] = x_ref[0] + gt_ref[0] * y


def _out_proj(x, gt, o_nsa, o_gla, o_g, gla_norm, w_out, tm):
    bx, tx, d = x.shape
    xs = pl.BlockSpec((1, tm, d), lambda b, i: (b, i, 0))
    return pl.pallas_call(
        _out_kernel,
        grid=(bx, tx // tm),
        in_specs=[xs, _mod_spec(gt, tm, 2),
                  pl.BlockSpec((1, NSA_HEADS, tm, NSA_HD), lambda b, i: (b, 0, i, 0)),
                  pl.BlockSpec((1, GLA_HEADS, tm, GLA_DV), lambda b, i: (b, 0, i, 0)),
                  pl.BlockSpec((1, tm, GLA_WIDTH), lambda b, i: (b, i, 0)),
                  pl.BlockSpec((1, GLA_DV), lambda b, i: (0, 0)),
                  pl.BlockSpec(w_out.shape, lambda b, i: (0, 0))],
        out_specs=xs,
        out_shape=jax.ShapeDtypeStruct(x.shape, F32),
        compiler_params=_cparams(("parallel", "parallel")),
        name="out_proj",
    )(x, gt, o_nsa, o_gla, o_g, gla_norm, w_out)


def _norm_kernel(x_ref, g_ref, o_ref):
    x = x_ref[0]
    o_ref[0] = x * lax.rsqrt(jnp.mean(x * x, axis=-1, keepdims=True) + EPS) * g_ref[...]


def _final_norm(x, g, tm):
    bx, tx, d = x.shape
    xs = pl.BlockSpec((1, tm, d), lambda b, i: (b, i, 0))
    return pl.pallas_call(
        _norm_kernel,
        grid=(bx, tx // tm),
        in_specs=[xs, pl.BlockSpec((1, d), lambda b, i: (0, 0))],
        out_specs=xs,
        out_shape=jax.ShapeDtypeStruct(x.shape, F32),
        compiler_params=_cparams(("parallel", "parallel")),
        name="final_norm",
    )(x, g)


def _nsa_sample_kernel(pt_ref, qall_ref, gate_ref, kvc_ref, kvs_ref, kvw_ref, *rest, n_pages, page, past, wlen):
    del pt_ref
    cmp_pages = rest[:n_pages]
    slc_pages = rest[n_pages:2 * n_pages]
    (cw_ref, w1_ref, pe_ref, b1_ref, w2_ref, ovt_ref, o_ref, x_scr, bias_scr) = rest[2 * n_pages:]
    nh = NSA_HEADS
    ncb = past // CMP_STRIDE
    qa = qall_ref[0]
    qa_f = qa.astype(F32)
    hrow = lax.broadcasted_iota(jnp.int32, (nh, 1), 0)
    slope = jnp.exp2(-(hrow + 1).astype(F32)) * LOG2E
    lane = lax.broadcasted_iota(jnp.int32, (1, LANES), 1)

    @pl.when(pl.program_id(0) == 0)
    def _():
        for kv in range(2):
            acc = jnp.zeros((8, 2 * LANES), F32)
            for s2 in range(CMP_BLOCK // 2):
                pe2 = jnp.broadcast_to(pe_ref[kv, s2:s2 + 1, :], (8, 2 * LANES))
                acc = acc + _dot(pe2.astype(BF16), w1_ref[kv, s2])
            bias_scr[kv] = acc + b1_ref[kv]

    ckv = []
    for kv in range(2):
        for pg in range(n_pages):
            x_scr[kv, pg * page:(pg + 1) * page, :] = cmp_pages[pg][0, 0, kv].reshape(LANES, page).T
        new = kvc_ref[0][:, kv * LANES:(kv + 1) * LANES]
        x_scr[kv, past:past + CMP_STRIDE, :] = jnp.where(
            lax.broadcasted_iota(jnp.int32, (CMP_STRIDE, 1), 0) == 0, new, 0.0)
        acc = jnp.zeros((ncb, 2 * LANES), F32)
        for s2 in range(CMP_BLOCK // 2):
            lhs = jnp.concatenate([x_scr[kv, pl.ds(2 * s2, ncb, stride=CMP_STRIDE), :],
                                   x_scr[kv, pl.ds(2 * s2 + 1, ncb, stride=CMP_STRIDE), :]], axis=1)
            acc = acc + _dot(lhs.astype(BF16), w1_ref[kv, s2])
        h = jax.nn.gelu(acc + bias_scr[kv][0:1])
        ckv.append(_dot(h.astype(BF16), w2_ref[kv]).astype(BF16))
    c_end = lane * CMP_STRIDE + (CMP_BLOCK - 1)
    dist_c = past - c_end
    mask_c = dist_c >= 0
    lg_c = jnp.where(mask_c, _dot_nt(qa, ckv[0]) - slope * dist_c.astype(F32), NEG)
    e_c = jnp.where(mask_c, jnp.exp2(lg_c - jnp.max(lg_c, axis=-1, keepdims=True)), 0.0)
    pc = e_c / jnp.maximum(jnp.sum(e_c, axis=-1, keepdims=True), 1e-30)
    o_cmp = _dot(pc.astype(BF16), ckv[1])

    ps0 = jnp.sum(jnp.where(hrow < NSA_GROUP, pc, 0.0), axis=0, keepdims=True)
    ps1 = jnp.sum(jnp.where(hrow >= NSA_GROUP, pc, 0.0), axis=0, keepdims=True)
    psum = jnp.where(hrow == 0, ps0, jnp.where(hrow == 1, ps1, 0.0))
    psum = jnp.concatenate([psum, jnp.zeros((LANES - nh, LANES), F32)], axis=0)
    p_hi = psum.astype(BF16)
    p_lo = (psum - p_hi.astype(F32)).astype(BF16)
    imp_t = _dot_nt(ovt_ref[...], p_hi) + _dot_nt(ovt_ref[...], p_lo)
    blk = lax.broadcasted_iota(jnp.int32, (LANES, 1), 0)
    cur = past // SEL_BLOCK
    forced = jnp.where(blk == 0, 1.0, 0.0) + jnp.where(blk == cur, 1.0, 0.0) + jnp.where(blk == cur - 1, 1.0, 0.0)
    score = jnp.where(blk * SEL_BLOCK <= past, imp_t + jnp.where(forced > 0.5, FORCE_BONUS, 0.0), NEG)
    sel = _top_k_mask(score, blk.astype(F32), SEL_TOPK, axis=0).T
    sel_h = jnp.where(hrow < NSA_GROUP, sel[0:1], sel[1:2])

    def attend(tiles, new_row, new_neg):
        lgs = [_dot(qa, k_t) + neg for k_t, _, neg in tiles]
        lg_new = jnp.sum(qa_f * new_row[:, :LANES], axis=-1, keepdims=True) + new_neg
        m = lg_new
        for lg in lgs:
            m = jnp.maximum(m, jnp.max(lg, axis=-1, keepdims=True))
        p_new = jnp.exp2(lg_new - m)
        l = p_new
        acc = p_new * new_row[:, LANES:]
        for lg, (_, v_t, _) in zip(lgs, tiles):
            p = jnp.exp2(lg - m)
            l = l + jnp.sum(p, axis=-1, keepdims=True)
            acc = acc + _dot_nt(p.astype(BF16), v_t)
        return acc / l

    per = page // SEL_BLOCK
    tiles = []
    for pg in range(n_pages):
        blkp = slc_pages[pg]
        dist = past - (pg * page + lane)
        selx = jnp.zeros((nh, LANES), F32)
        for u in range(per):
            in_u = jnp.where(lane >= u * SEL_BLOCK, jnp.where(lane < (u + 1) * SEL_BLOCK, 1.0, 0.0), 0.0)
            selx = selx + in_u * sel_h[:, pg * per + u:pg * per + u + 1]
        neg = jnp.where(dist >= 0, (selx - 1.0) * 1e30, NEG) - slope * dist.astype(F32)
        tiles.append((blkp[0, 0, 0].reshape(LANES, page).astype(BF16),
                      blkp[0, 0, 1].reshape(LANES, page).astype(BF16), neg))
    o_slc = attend(tiles, kvs_ref[0], (sel_h[:, cur:cur + 1] - 1.0) * 1e30)

    wlane = lax.broadcasted_iota(jnp.int32, (1, wlen), 1)
    w_pos = past - wlen + wlane
    dist = past - w_pos
    ok = jnp.where(dist >= 0, jnp.where(dist < WINDOW, jnp.where(w_pos >= 0, 1.0, 0.0), 0.0), 0.0)
    neg = (ok - 1.0) * 1e30 - slope * dist.astype(F32)
    tiles = [(cw_ref[0, 0, 0].reshape(LANES, wlen).astype(BF16), cw_ref[0, 0, 1].reshape(LANES, wlen).astype(BF16), neg)]
    o_win = attend(tiles, kvw_ref[0], jnp.zeros((nh, 1), F32))

    gates = gate_ref[0]
    pick = lambda a: jnp.where(hrow < NSA_GROUP, a[:, :NSA_HD], a[:, NSA_HD:])
    o_ref[0] = gates[:, 0:1] * pick(o_cmp) + gates[:, 1:2] * pick(o_slc) + gates[:, 2:3] * pick(o_win)


def _nsa_sample(layer, page_table, q_all, gates, kvc, kvs, kvw, cache_c, cache_s, cache_w, cw):
    nb, n_pages = page_table.shape
    depth, n_phys, page = cache_c.shape[:3]
    past = n_pages * page
    wlen = cache_w.shape[2]
    ncb = past // CMP_STRIDE
    cc = cache_c.transpose(0, 1, 3, 4, 5, 2)
    cs = cache_s.transpose(0, 1, 3, 4, 5, 2)
    cwin = cache_w.transpose(0, 1, 3, 4, 5, 2)
    row = lambda w: pl.BlockSpec((1, 1, w), lambda b, pt: (b, 0, 0))
    full = lambda a: pl.BlockSpec(a.shape, lambda b, pt: (0,) * a.ndim)
    page_specs = [pl.BlockSpec((1, 1, 2, NSA_KV_HEADS, NSA_HD, page), lambda b, pt, j=j: (layer, pt[b, j], 0, 0, 0, 0))
                  for j in range(n_pages)]
    ovt = _overlap_matrix(ncb, LANES).T
    wargs = (cw["w1s"], cw["pes"], cw["b1s"], cw["w2s"], ovt)
    grid_spec = pltpu.PrefetchScalarGridSpec(
        num_scalar_prefetch=1,
        grid=(nb,),
        in_specs=[pl.BlockSpec((1, NSA_HEADS, LANES), lambda b, pt: (b, 0, 0)),
                  pl.BlockSpec((1, NSA_HEADS, LANES), lambda b, pt: (b, 0, 0)),
                  row(KV_COLS), row(KV_COLS), row(KV_COLS)] + page_specs + page_specs
                 + [pl.BlockSpec((1, 1, 2, NSA_KV_HEADS, NSA_HD, wlen), lambda b, pt: (layer, b, 0, 0, 0, 0))]
                 + [full(a) for a in wargs],
        out_specs=pl.BlockSpec((1, NSA_HEADS, NSA_HD), lambda b, pt: (b, 0, 0)),
        scratch_shapes=[pltpu.VMEM((2, past + CMP_STRIDE, LANES), F32), pltpu.VMEM((2, 8, 2 * LANES), F32)],
    )
    return pl.pallas_call(
        functools.partial(_nsa_sample_kernel, n_pages=n_pages, page=page, past=past, wlen=wlen),
        grid_spec=grid_spec,
        out_shape=jax.ShapeDtypeStruct((nb, NSA_HEADS, NSA_HD), F32),
        compiler_params=_cparams(("arbitrary",)),
        name="nsa_sample",
    )(page_table, q_all, gates, kvc, kvs, kvw, *([cc] * n_pages), *([cs] * n_pages), cwin, *wargs)


def _gla_sample_kernel(col_ref, v_ref, s_ref, o_ref, sn_ref, *, bb):
    for b in range(bb):
        cols = col_ref[b]
        for h in range(GLA_HEADS):
            a = jnp.exp(cols[:, h:h + 1])
            k = cols[:, GLA_HEADS + h:GLA_HEADS + h + 1]
            q = cols[:, 2 * GLA_HEADS + h:2 * GLA_HEADS + h + 1]
            v = v_ref[b, h:h + 1, :]
            s0 = s_ref[b, h]
            qk = jnp.sum(q * k, axis=0, keepdims=True)
            o_ref[b, h:h + 1, :] = jnp.sum((q * a) * s0, axis=0, keepdims=True) + qk * v
            sn_ref[b, h] = a * s0 + k * v


def _gla_sample(cols, v, state, bb):
    nb = v.shape[0]
    return pl.pallas_call(
        functools.partial(_gla_sample_kernel, bb=bb),
        grid=(nb // bb,),
        in_specs=[pl.BlockSpec((bb, GLA_DK, 16), lambda i: (i, 0, 0)),
                  pl.BlockSpec((bb, GLA_HEADS, GLA_DV), lambda i: (i, 0, 0)),
                  pl.BlockSpec((bb, GLA_HEADS, GLA_DK, GLA_DV), lambda i: (i, 0, 0, 0))],
        out_specs=[pl.BlockSpec((bb, GLA_HEADS, GLA_DV), lambda i: (i, 0, 0)),
                   pl.BlockSpec((bb, GLA_HEADS, GLA_DK, GLA_DV), lambda i: (i, 0, 0, 0))],
        out_shape=[jax.ShapeDtypeStruct((nb, GLA_HEADS, GLA_DV), F32),
                   jax.ShapeDtypeStruct((nb, GLA_HEADS, GLA_DK, GLA_DV), F32)],
        compiler_params=_cparams(("parallel",)),
        name="gla_sample",
    )(cols, v, state)


def _pad_cols(w, sizes, pads):
    parts, off = [], 0
    for s, p in zip(sizes, pads):
        parts.append(jnp.pad(w[:, off:off + s], ((0, 0), (0, p - s))))
        off += s
    return jnp.concatenate(parts, axis=1)


def _cmp_weights(pe, w1, b1, w2):
    g = NSA_KV_HEADS
    eye = jnp.eye(2, dtype=F32)
    eg = jnp.eye(g, dtype=F32)
    w1r = w1.reshape(2, 2, CMP_STRIDE, NSA_HD, CMP_HIDDEN)
    big = jnp.einsum("krsdh,ka,gb->rskgdabh", w1r, eye, eg).reshape(2, CMP_STRIDE * KV_COLS, 2 * g * CMP_HIDDEN)
    w2big = jnp.einsum("khd,ka,gb->kghabd", w2, eye, eg).reshape(2 * g * CMP_HIDDEN, KV_COLS)
    per = pe.reshape(2, 2, CMP_STRIDE, NSA_HD)
    pet = jnp.broadcast_to(per.transpose(1, 2, 0, 3)[:, :, :, None, :], (2, CMP_STRIDE, 2, g, NSA_HD))
    pet = pet.reshape(2, 1, CMP_STRIDE * KV_COLS)
    b1big = jnp.broadcast_to(b1[:, None, :], (2, g, CMP_HIDDEN)).reshape(1, 2 * g * CMP_HIDDEN)
    w1s = jnp.einsum("ksdh,gb->ksgdbh", w1.reshape(2, CMP_BLOCK, NSA_HD, CMP_HIDDEN), eg)
    w1s = w1s.reshape(2, CMP_BLOCK // 2, 2 * g * NSA_HD, g * CMP_HIDDEN)
    w2s = jnp.einsum("khd,gb->kghbd", w2, eg).reshape(2, g * CMP_HIDDEN, g * NSA_HD)
    pes = jnp.broadcast_to(pe[:, :, None, :], (2, CMP_BLOCK, g, NSA_HD)).reshape(2, CMP_BLOCK // 2, 2 * g * NSA_HD)
    b1s = jnp.broadcast_to(b1[:, None, None, :], (2, 1, g, CMP_HIDDEN)).reshape(2, 1, g * CMP_HIDDEN)
    return {
        "w1s": w1s.astype(BF16), "w2s": w2s.astype(BF16), "pes": pes, "b1s": b1s,
        "w1lo_t": big[0].T.astype(BF16), "w1hi_t": big[1].T.astype(BF16),
        "pe_lo": pet[0], "pe_hi": pet[1], "b1_col": b1big.T, "w2_t": w2big.T.astype(BF16),
    }


def _heads_first(a, hd):
    b, t, w = a.shape
    return a.reshape(b, t, w // hd, hd).transpose(0, 2, 1, 3)


def _kv_split(a):
    b, t, _ = a.shape
    return a.reshape(b, t, 2, NSA_KV_HEADS, NSA_HD).transpose(0, 2, 3, 1, 4).astype(BF16)


def _mods(mod, rows):
    m = mod.reshape(mod.shape[0], N_MOD, D_MODEL)
    if rows == 1:
        return [m[:, j][:, None, :] for j in range(N_MOD)]
    return [m[:, j][None, :, :] for j in range(N_MOD)]


def _layer_prompt(x, mods, lw, tm, tq, tk):
    b, t, _ = x.shape
    x = _ffn(x, mods[0], mods[1], mods[2], lw["norm_g"][0:1], lw["f1_in"], lw["f1_out"], tm)
    (qn, kvc, _, _, kvct, kvst, kvwt, kvtb, gn, qg, kg, vg, la, og) = _proj(
        x, mods[3], mods[4], lw["norm_g"][1:2], lw["w_all"], lw["wkvt"], lw["wa2p"], lw["ba"], tm)
    g, r = NSA_KV_HEADS, NSA_GROUP
    ncp = t // CMP_STRIDE
    ckvt = _cmp_prompt(kvc.reshape(b, ncp, CMP_STRIDE * KV_COLS), lw["cmp"]).astype(BF16)
    ckvt = ckvt.reshape(b, 2, g, NSA_HD, ncp)
    kvs5 = kvtb[:, :KV_COLS].reshape(b, 2, g, NSA_HD, t)
    kvw5 = kvtb[:, KV_COLS:].reshape(b, 2, g, NSA_HD, t)
    pos_rows = _alibi_k_rows(np.arange(t))
    ones_row = jnp.ones((1, t), BF16)
    onehot = jnp.asarray(np.arange(max(LANES, t // SEL_BLOCK))[:, None] == np.arange(t)[None, :] // SEL_BLOCK, BF16)
    lead = (b, g)
    ck = _aug_rows([ckvt[:, 0], _alibi_k_rows(np.arange(ncp) * CMP_STRIDE + CMP_BLOCK - 1)], LANES, lead)
    ks = _aug_rows([onehot, kvs5[:, 0], pos_rows], onehot.shape[0] + LANES, lead)
    vs = _aug_rows([kvs5[:, 1], ones_row], LANES, lead)
    kw = _aug_rows([kvw5[:, 0], pos_rows], LANES, lead)
    vw = _aug_rows([kvw5[:, 1], ones_row], LANES, lead)
    q5 = qn.reshape(b, t, g, r, NSA_HD).transpose(0, 2, 3, 1, 4)
    qcols = jnp.broadcast_to(_alibi_q_cols().reshape(1, g, r, 1, 4), (b, g, r, t, 4))
    q_pad = jnp.concatenate([q5, qcols, jnp.zeros((b, g, r, t, LANES - NSA_HD - 4), BF16)], axis=-1)
    gates = gn[:, :, :3 * NSA_HEADS].reshape(b, t, g, 3 * r).transpose(0, 2, 1, 3)
    gates = jnp.pad(gates, ((0, 0), (0, 0), (0, 0), (0, 16 - 3 * r)))
    o_nsa = _nsa_prompt(q_pad, gates, ck, ckvt[:, 1], ks, vs, kw, vw, tq, tk)
    o_nsa = o_nsa.reshape(b, NSA_HEADS, t, NSA_HD)
    nc = t // GLA_CHUNK
    ch = lambda a, hd: _heads_first(a, hd).reshape(b, GLA_HEADS, nc, GLA_CHUNK, hd)
    k5, la5 = ch(kg, GLA_DK), ch(la, GLA_DK)
    o_gla, s_fin = _gla_prompt(ch(qg, GLA_DK), k5, la5, k5.swapaxes(-1, -2), la5.swapaxes(-1, -2),
                               ch(vg, GLA_DV), min(8, nc))
    o_gla = o_gla.reshape(b, GLA_HEADS, t, GLA_DV)
    x = _out_proj(x, mods[5], o_nsa, o_gla, og, lw["gla_norm"], lw["w_out"], tm)
    x = _ffn(x, mods[6], mods[7], mods[8], lw["norm_g"][2:3], lw["f2_in"], lw["f2_out"], tm)
    win = min(WINDOW, t)
    back = lambda a: a.reshape(b, 2, NSA_KV_HEADS, NSA_HD, a.shape[-1]).transpose(0, 4, 1, 2, 3)
    return x, (back(kvct), back(kvst), back(kvwt[:, :, t - win:]), s_fin)


def _layer_sample(layer, x, mods, lw, cache_c, cache_s, cache_w, state, page_table):
    nb = x.shape[1]
    x = _ffn(x, mods[0], mods[1], mods[2], lw["norm_g"][0:1], lw["f1_in"], lw["f1_out"], nb)
    (qn, kvc, kvs, kvw, kvct, kvst, kvwt, _, gn, qg, kg, vg, la, og) = _proj(
        x, mods[3], mods[4], lw["norm_g"][1:2], lw["w_all"], lw["wkvt"], lw["wa2p"], lw["ba"], nb)
    q4 = qn[0].reshape(nb, NSA_KV_HEADS, NSA_GROUP, NSA_HD)
    q_all = (q4[:, :, :, None, :] * jnp.eye(NSA_KV_HEADS, dtype=BF16)[None, :, None, :, None]).reshape(nb, NSA_HEADS, LANES)
    gates = jnp.pad(gn[0][:, :3 * NSA_HEADS].reshape(nb, NSA_HEADS, 3), ((0, 0), (0, 0), (0, LANES - 3)))
    r3 = lambda a: a[0][:, None, :]
    o_nsa = _nsa_sample(layer, page_table, q_all, gates, r3(kvc), r3(kvs), r3(kvw),
                        cache_c, cache_s, cache_w, lw["cmp"])
    o_nsa = o_nsa.transpose(1, 0, 2)[None]
    col = lambda a: a[0].reshape(nb, GLA_HEADS, GLA_DK).transpose(0, 2, 1)
    cols = jnp.concatenate([col(la), col(kg), col(qg), jnp.zeros((nb, GLA_DK, 16 - 3 * GLA_HEADS), F32)], axis=-1)
    o_gla, s_new = _gla_sample(cols, vg[0].reshape(nb, GLA_HEADS, GLA_DV), state, 8)
    o_gla = o_gla.transpose(1, 0, 2)[None]
    x = _out_proj(x, mods[5], o_nsa, o_gla, og, lw["gla_norm"], lw["w_out"], nb)
    x = _ffn(x, mods[6], mods[7], mods[8], lw["norm_g"][2:3], lw["f2_in"], lw["f2_out"], nb)
    kv5 = lambda a: a[0].reshape(2, NSA_KV_HEADS, NSA_HD, nb).transpose(3, 0, 1, 2)[:, None]
    return x, (kv5(kvct), kv5(kvst), kv5(kvwt), s_new)


def kernel(x_prompt, x_sample, c_prompt, c_sample, cache_kv_cmp, cache_kv_slc, cache_kv_win, state_gla, page_table, w_ada, b_ada, norm_g, ffn1_w_in, ffn1_w_out, w_in, cmp_pe, cmp_w1, cmp_b1, cmp_w2, gla_wa2, gla_ba, gla_norm, w_out, ffn2_w_in, ffn2_w_out, final_norm):
    depth = w_ada.shape[0]
    b, t, d = x_prompt.shape
    nb = x_sample.shape[0]
    bp = -(-b // 8) * 8
    c_all = jnp.concatenate([c_prompt, jnp.zeros((bp - b, d), F32), c_sample], axis=0)
    mod_all = _ada(c_all, w_ada, b_ada)

    tm = min(512, t)
    tq = min(256, t)
    tk = min(512, t)
    xp = x_prompt
    xs = x_sample.reshape(1, nb, d)
    outs_p, outs_s = [], []
    for l in range(depth):
        lw = {
            "norm_g": norm_g[l],
            "f1_in": ffn1_w_in[l].astype(BF16), "f1_out": ffn1_w_out[l].astype(BF16),
            "f2_in": ffn2_w_in[l].astype(BF16), "f2_out": ffn2_w_out[l].astype(BF16),
            "w_all": _pad_cols(w_in[l], IN_SIZES, IN_PAD).astype(BF16),
            "wkvt": w_in[l][:, NSA_WIDTH:NSA_WIDTH + 3 * KV_COLS].T.astype(BF16),
            "wa2p": jnp.pad(gla_wa2[l], ((0, 128 - GLA_RANK), (0, 0))).astype(BF16),
            "ba": gla_ba[l][None, :],
            "gla_norm": gla_norm[l][None, :],
            "w_out": w_out[l].astype(BF16),
            "cmp": _cmp_weights(cmp_pe[l], cmp_w1[l], cmp_b1[l], cmp_w2[l]),
        }
        xp, st_p = _layer_prompt(xp, _mods(mod_all[l, :b], 1), lw, tm, tq, tk)
        xs, st_s = _layer_sample(l, xs, _mods(mod_all[l, bp:], nb), lw, cache_kv_cmp, cache_kv_slc,
                                 cache_kv_win, state_gla[l], page_table)
        outs_p.append(st_p)
        outs_s.append(st_s)
    fn = final_norm[None, :]
    y_prompt = _final_norm(xp, fn, tm)
    y_sample = _final_norm(xs, fn, nb).reshape(nb, 1, d)
    stack = lambda outs, j: jnp.stack([o[j] for o in outs])
    return (y_prompt, y_sample,
            stack(outs_p, 0), stack(outs_p, 1), stack(outs_p, 2), stack(outs_p, 3),
            stack(outs_s, 0), stack(outs_s, 1), stack(outs_s, 2), stack(outs_s, 3))
```

```python
import functools

import numpy as np
import jax
import jax.numpy as jnp
from jax import lax
from jax.experimental import pallas as pl
from jax.experimental.pallas import tpu as pltpu

F32 = jnp.float32
BF16 = jnp.bfloat16

D_MODEL = 1024
NSA_HD = 64
NSA_HEADS = 8
NSA_KV_HEADS = 2
NSA_GROUP = 4
NSA_WIDTH = NSA_HEADS * NSA_HD
CMP_BLOCK = 32
CMP_STRIDE = 16
CMP_HIDDEN = 2 * NSA_HD
SEL_BLOCK = 64
SEL_TOPK = 16
WINDOW = 512
FORCE_BONUS = 1000.0
GLA_HEADS = 4
GLA_DV = 128
GLA_DK = 64
GLA_WIDTH = GLA_HEADS * GLA_DV
GLA_RANK = 16
GLA_TAU = 16.0
GLA_CHUNK = 64
GLA_SUB = 16
FFN_DIM = 2816
N_MOD = 9
EPS = 1e-6
KV_COLS = 2 * NSA_KV_HEADS * NSA_HD
IN_SIZES = (NSA_WIDTH, KV_COLS, KV_COLS, KV_COLS, 3 * NSA_HEADS,
            GLA_HEADS * GLA_DK, GLA_HEADS * GLA_DK, GLA_WIDTH, GLA_RANK, GLA_WIDTH)
IN_PAD = (NSA_HEADS * 128, KV_COLS, KV_COLS, KV_COLS, NSA_KV_HEADS * 128,
          GLA_HEADS * GLA_DK, GLA_HEADS * GLA_DK, GLA_WIDTH, 128, GLA_WIDTH)
IN_OFF = tuple(int(v) for v in np.cumsum((0,) + IN_PAD))

LANES = 128
NEG = -1e30
LOG2E = 1.4426950408889634
VMEM_LIMIT = 56 * 1024 * 1024
FFN_TF = FFN_DIM // 2


def _cparams(sem):
    return pltpu.CompilerParams(dimension_semantics=sem, vmem_limit_bytes=VMEM_LIMIT)


def _dot(a, b):
    return jnp.dot(a, b, preferred_element_type=F32)


def _dot_nt(a, b):
    return lax.dot_general(a, b, (((1,), (1,)), ((), ())), preferred_element_type=F32)


def _split3(x):
    x1 = x.astype(BF16)
    r = x - x1.astype(F32)
    x2 = r.astype(BF16)
    x3 = (r - x2.astype(F32)).astype(BF16)
    return x1, x2, x3


def _rms_mod(x, g, sc, sh):
    y = x * lax.rsqrt(jnp.mean(x * x, axis=-1, keepdims=True) + EPS) * g
    return y * (1.0 + sc) + sh


def _masked_softmax(lg, mask):
    lg = jnp.where(mask, lg, NEG)
    m = jnp.max(lg, axis=-1, keepdims=True)
    e = jnp.where(mask, jnp.exp(lg - m), 0.0)
    return e / jnp.maximum(jnp.sum(e, axis=-1, keepdims=True), 1e-30)


def _top_k_mask(score, blk, k, axis=-1):
    sel = jnp.zeros_like(score)
    big = float(score.shape[axis])
    for _ in range(k):
        m = jnp.max(score, axis=axis, keepdims=True)
        idx = jnp.min(jnp.where(score == m, blk, big), axis=axis, keepdims=True)
        hit = blk == idx
        sel = jnp.where(hit, 1.0, sel)
        score = jnp.where(hit, -jnp.inf, score)
    return sel


def _ada_kernel(c_ref, w_ref, b_ref, o_ref):
    c = c_ref[...]
    s = c * jax.nn.sigmoid(c)
    o_ref[0] = _dot(s.astype(BF16), w_ref[0].astype(BF16)) + b_ref[0]


def _ada(c_all, w_ada, b_ada):
    depth, d, n = w_ada.shape
    m = c_all.shape[0]
    tn = 1024
    return pl.pallas_call(
        _ada_kernel,
        grid=(depth, n // tn),
        in_specs=[pl.BlockSpec((m, d), lambda l, j: (0, 0)),
                  pl.BlockSpec((1, d, tn), lambda l, j: (l, 0, j)),
                  pl.BlockSpec((1, 1, tn), lambda l, j: (l, 0, j))],
        out_specs=pl.BlockSpec((1, m, tn), lambda l, j: (l, 0, j)),
        out_shape=jax.ShapeDtypeStruct((depth, m, n), F32),
        compiler_params=_cparams(("parallel", "parallel")),
        name="ada",
    )(c_all, w_ada, b_ada.reshape(depth, 1, n))


def _mod_spec(mod, tm, nargs):
    r = mod.shape[1]
    d = mod.shape[2]
    if nargs == 3:
        if r == 1:
            return pl.BlockSpec((1, 1, d), lambda b, i, f: (b, 0, 0))
        return pl.BlockSpec((1, tm, d), lambda b, i, f: (b, i, 0))
    if r == 1:
        return pl.BlockSpec((1, 1, d), lambda b, i: (b, 0, 0))
    return pl.BlockSpec((1, tm, d), lambda b, i: (b, i, 0))


def _ffn_kernel(x_ref, sh_ref, sc_ref, gt_ref, g_ref, wg_ref, wu_ref, wo_ref, o_ref, h_scr, acc_scr):
    f = pl.program_id(2)

    @pl.when(f == 0)
    def _():
        h = _rms_mod(x_ref[0], g_ref[...], sc_ref[0], sh_ref[0])
        h_scr[...] = h.astype(BF16)
        acc_scr[...] = jnp.zeros_like(acc_scr)

    h = h_scr[...]
    g = _dot(h, wg_ref[...])
    u = _dot(h, wu_ref[...])
    a = (g * jax.nn.sigmoid(g) * u).astype(BF16)
    acc_scr[...] += _dot(a, wo_ref[...])

    @pl.when(f == pl.num_programs(2) - 1)
    def _():
        o_ref[0] = x_ref[0] + 0.5 * gt_ref[0] * acc_scr[...]


def _ffn(x, sh, sc, gt, g, w_in, w_out, tm):
    bx, tx, d = x.shape
    nf = FFN_DIM // FFN_TF
    xs = pl.BlockSpec((1, tm, d), lambda b, i, f: (b, i, 0))
    ms = _mod_spec(sh, tm, 3)
    return pl.pallas_call(
        _ffn_kernel,
        grid=(bx, tx // tm, nf),
        in_specs=[xs, ms, ms, ms,
                  pl.BlockSpec((1, d), lambda b, i, f: (0, 0)),
                  pl.BlockSpec((d, FFN_TF), lambda b, i, f: (0, f)),
                  pl.BlockSpec((d, FFN_TF), lambda b, i, f: (0, f + nf)),
                  pl.BlockSpec((FFN_TF, d), lambda b, i, f: (f, 0))],
        out_specs=xs,
        out_shape=jax.ShapeDtypeStruct(x.shape, F32),
        scratch_shapes=[pltpu.VMEM((tm, d), BF16), pltpu.VMEM((tm, d), F32)],
        compiler_params=_cparams(("parallel", "parallel", "arbitrary")),
        name="ffn",
    )(x, sh, sc, gt, g, w_in, w_in, w_out)


def _proj_kernel(x_ref, sh_ref, sc_ref, g_ref, w_ref, wkvt_ref, wa2_ref, ba_ref, qb_ref,
                 qn_ref, kvc_ref, kvs_ref, kvw_ref, kvct_ref, kvst_ref, kvwt_ref, kvtb_ref,
                 gn_ref, qg_ref, kg_ref, vg_ref, la_ref, og_ref):
    h = _rms_mod(x_ref[0], g_ref[...], sc_ref[0], sh_ref[0]).astype(BF16)
    p = _dot(h, w_ref[...])
    o = IN_OFF
    qn_ref[0] = (p[:, o[0]:o[1]] * (NSA_HD ** -0.5 * LOG2E) + qb_ref[...]).astype(BF16)
    kvc_ref[0] = p[:, o[1]:o[2]]
    kvs_ref[0] = p[:, o[2]:o[3]]
    kvw_ref[0] = p[:, o[3]:o[4]]
    kvt = _dot_nt(wkvt_ref[...], h)
    kvct_ref[0] = kvt[0:KV_COLS]
    kvst_ref[0] = kvt[KV_COLS:2 * KV_COLS]
    kvwt_ref[0] = kvt[2 * KV_COLS:3 * KV_COLS]
    kvtb_ref[0] = kvt[KV_COLS:3 * KV_COLS].astype(BF16)
    gn_ref[0] = jax.nn.sigmoid(p[:, o[4]:o[5]])
    qg_ref[0] = p[:, o[5]:o[6]] * (GLA_DK ** -0.5)
    kg_ref[0] = p[:, o[6]:o[7]]
    vg_ref[0] = p[:, o[7]:o[8]]
    a_pre = _dot(p[:, o[8]:o[9]].astype(BF16), wa2_ref[...]) + ba_ref[...]
    log_sig = jnp.minimum(a_pre, 0.0) - jnp.log(1.0 + jnp.exp(-jnp.abs(a_pre)))
    la_ref[0] = log_sig * (1.0 / GLA_TAU)
    og_ref[0] = p[:, o[9]:o[10]]


def _proj(x, sh, sc, g, w_all, wkvt, wa2p, ba, qb, tm):
    bx, tx, d = x.shape
    xs = pl.BlockSpec((1, tm, d), lambda b, i: (b, i, 0))
    ms = _mod_spec(sh, tm, 2)
    full = lambda shape: pl.BlockSpec(shape, lambda b, i: (0,) * len(shape))
    rows = lambda w, dt: (pl.BlockSpec((1, tm, w), lambda b, i: (b, i, 0)), jax.ShapeDtypeStruct((bx, tx, w), dt))
    cols = lambda w, dt: (pl.BlockSpec((1, w, tm), lambda b, i: (b, 0, i)), jax.ShapeDtypeStruct((bx, w, tx), dt))
    outs = [rows(IN_PAD[0], BF16), rows(KV_COLS, F32), rows(KV_COLS, F32), rows(KV_COLS, F32),
            cols(KV_COLS, F32), cols(KV_COLS, F32), cols(KV_COLS, F32), cols(2 * KV_COLS, BF16),
            rows(IN_PAD[4], F32), rows(256, F32), rows(256, F32), rows(GLA_WIDTH, F32), rows(256, F32),
            rows(GLA_WIDTH, F32)]
    return pl.pallas_call(
        _proj_kernel,
        grid=(bx, tx // tm),
        in_specs=[xs, ms, ms, full((1, d)), full(w_all.shape), full(wkvt.shape), full(wa2p.shape),
                  full(ba.shape), full(qb.shape)],
        out_specs=[o[0] for o in outs],
        out_shape=[o[1] for o in outs],
        compiler_params=_cparams(("parallel", "parallel")),
        name="proj",
    )(x, sh, sc, g, w_all, wkvt, wa2p, ba, qb)


def _cmp_kernel(x_ref, pelo_ref, pehi_ref, w1lo_ref, w1hi_ref, b1_ref, w2_ref, o_ref):
    x = x_ref[0]
    n = x.shape[0]
    h_lo = _dot_nt(w1lo_ref[...], (x + pelo_ref[...]).astype(BF16))
    h_hi = _dot_nt(w1hi_ref[...], (x + pehi_ref[...]).astype(BF16))
    h = h_lo + pltpu.roll(h_hi, n - 1, 1) + b1_ref[...]
    o_ref[0] = _dot(w2_ref[...], jax.nn.gelu(h).astype(BF16))


def _cmp_prompt(x16, cw):
    b, n, w = x16.shape
    full = lambda a: pl.BlockSpec(a.shape, lambda i: (0,) * a.ndim)
    args = (cw["pe_lo"], cw["pe_hi"], cw["w1lo_t"], cw["w1hi_t"], cw["b1_col"], cw["w2_t"])
    return pl.pallas_call(
        _cmp_kernel,
        grid=(b,),
        in_specs=[pl.BlockSpec((1, n, w), lambda i: (i, 0, 0))] + [full(a) for a in args],
        out_specs=pl.BlockSpec((1, KV_COLS, n), lambda i: (i, 0, 0)),
        out_shape=jax.ShapeDtypeStruct((b, KV_COLS, n), F32),
        compiler_params=_cparams(("parallel",)),
        name="cmp_prompt",
    )(x16, *args)


def _nsa_prompt_kernel(q_ref, gate_ref, ck_ref, cv_ref, ks_ref, vs_ref, kw_ref, vw_ref, ovt_ref,
                       o_ref, ids_scr, *, tq, tk, ncp, nsp):
    i = pl.program_id(2)
    q0 = i * tq
    nr = NSA_GROUP * tq
    row = q0 + jnp.bitwise_and(lax.broadcasted_iota(jnp.int32, (nr, 1), 0), tq - 1)
    row_t = q0 + lax.broadcasted_iota(jnp.int32, (tq, 1), 0)
    qblk = q_ref[0]
    q = jnp.concatenate([qblk[:, r * LANES:(r + 1) * LANES] for r in range(NSA_GROUP)], axis=0)

    def per_head(x, mask):
        return (x.reshape(NSA_GROUP, tq, x.shape[-1]) + mask[None]).reshape(x.shape)

    c_end = lax.broadcasted_iota(jnp.int32, (1, ncp), 1) * CMP_STRIDE + (CMP_BLOCK - 1)
    lg = per_head(_dot(q, ck_ref[0, 0]), jnp.where(c_end <= row_t, 0.0, NEG))
    e = jnp.exp2(lg - jnp.max(lg, axis=-1, keepdims=True))
    pc = e * jnp.where(row >= CMP_BLOCK - 1, 1.0 / jnp.sum(e, axis=-1, keepdims=True), 0.0)
    o_cmp = _dot_nt(pc.astype(BF16), cv_ref[0, 0])
    psum = pc[0:tq]
    for r in range(1, NSA_GROUP):
        psum = psum + pc[r * tq:(r + 1) * tq]
    p_hi = psum.astype(BF16)
    p_lo = (psum - p_hi.astype(F32)).astype(BF16)
    imp_t = _dot_nt(ovt_ref[...], p_hi) + _dot_nt(ovt_ref[...], p_lo)

    blk = lax.broadcasted_iota(jnp.int32, (nsp, 1), 0)
    t_lane = q0 + lax.broadcasted_iota(jnp.int32, (1, tq), 1)
    cur = jnp.right_shift(t_lane, 6)
    forced = jnp.where(blk == 0, 1.0, 0.0) + jnp.where(blk == cur, 1.0, 0.0) + jnp.where(blk == cur - 1, 1.0, 0.0)
    valid = blk * SEL_BLOCK <= t_lane
    score = jnp.where(valid, imp_t + jnp.where(forced > 0.5, FORCE_BONUS, 0.0), NEG)
    sel_t = _top_k_mask(score, blk.astype(F32), SEL_TOPK, axis=0)
    sel_t = jnp.where(valid, sel_t, 0.0)
    sel_bias = ((sel_t - 1.0) * 1e30).T.astype(BF16)
    qa = jnp.concatenate([jnp.concatenate([sel_bias] * NSA_GROUP, axis=0), q], axis=1)

    n_full = q0 // tk
    per_tile = tk // SEL_BLOCK
    n_tiles = nsp // per_tile
    blk_any = jnp.max(sel_t, axis=1, keepdims=True)
    tile_any = jnp.max(blk_any.reshape(n_tiles, per_tile, 1), axis=1)
    bit = jnp.left_shift(1, lax.broadcasted_iota(jnp.int32, (n_tiles, 1), 0))
    tile_bits = jnp.sum(jnp.where(tile_any > 0.0, bit, 0))
    n_used = jnp.int32(0)
    for kt in range(min(n_tiles, ks_ref.shape[3] // tk)):
        use = jnp.logical_and(jnp.bitwise_and(jnp.right_shift(tile_bits, kt), 1) == 1, kt < n_full)
        ids_scr[n_used] = kt
        n_used = n_used + use.astype(jnp.int32)

    def sel_tile(kt, carry, diag):
        m, acc = carry
        k0 = pl.multiple_of(kt * tk, tk)
        lg = _dot(qa, ks_ref[0, 0, :, pl.ds(k0, tk)])
        if diag:
            lg = jnp.where(k0 + lax.broadcasted_iota(jnp.int32, (1, tk), 1) <= row, lg, NEG)
        m_new = jnp.maximum(m, jnp.max(lg, axis=-1, keepdims=True))
        p = jnp.exp2(lg - m_new).astype(BF16)
        return m_new, jnp.exp2(m - m_new) * acc + _dot_nt(p, vs_ref[0, 0, :, pl.ds(k0, tk)])

    carry = sel_tile(n_full, (jnp.full((nr, 1), NEG, F32), jnp.zeros((nr, LANES), F32)), True)
    _, acc = lax.fori_loop(0, n_used, lambda j, c: sel_tile(ids_scr[j], c, False), carry)
    o_slc = acc[:, :NSA_HD] / acc[:, NSA_HD:NSA_HD + 1]

    wl = WINDOW + tq
    w0 = pl.multiple_of(jnp.maximum(q0 - WINDOW, 0), tq)
    dist_w = row_t - (w0 + lax.broadcasted_iota(jnp.int32, (1, wl), 1))
    neg_w = jnp.where(dist_w >= 0, jnp.where(dist_w < WINDOW, 0.0, NEG), NEG)
    lg = per_head(_dot(q, kw_ref[0, 0, :, pl.ds(w0, wl)]), neg_w)
    p = jnp.exp2(lg - jnp.max(lg, axis=-1, keepdims=True)).astype(BF16)
    ow = _dot_nt(p, vw_ref[0, 0, :, pl.ds(w0, wl)])
    o_win = ow[:, :NSA_HD] / ow[:, NSA_HD:NSA_HD + 1]

    gates = gate_ref[0]
    gcol = lambda j: jnp.concatenate([gates[:, 3 * r + j:3 * r + j + 1] for r in range(NSA_GROUP)], axis=0)
    o = gcol(0) * o_cmp + gcol(1) * o_slc + gcol(2) * o_win
    o_ref[0, 0] = o.reshape(NSA_GROUP, tq, NSA_HD)


def _overlap_matrix(ncp, nsp):
    c = np.arange(ncp)[:, None] * CMP_STRIDE
    s = np.arange(nsp)[None, :] * SEL_BLOCK
    return jnp.asarray((c <= s + SEL_BLOCK - 1) & (c + CMP_BLOCK - 1 >= s), dtype=BF16)


def _bf16_round(x):
    return np.asarray(x, dtype=BF16).astype(np.float32)


def _q_bias():
    slopes = np.exp2(-8.0 * np.arange(1, NSA_HEADS + 1) / NSA_HEADS)
    c_hi = float(_bf16_round(LOG2E))
    c_lo = float(_bf16_round(LOG2E - c_hi))
    qb = np.zeros((NSA_HEADS, LANES), np.float32)
    qb[:, NSA_HD:NSA_HD + 4] = np.stack([c_hi * slopes * SEL_BLOCK, c_lo * slopes * SEL_BLOCK,
                                         c_hi * slopes, c_lo * slopes], axis=1)
    return jnp.asarray(qb.reshape(1, NSA_HEADS * LANES))


def _alibi_k_rows(pos):
    pos = np.asarray(pos)
    return jnp.asarray(np.stack([pos // SEL_BLOCK, pos // SEL_BLOCK, pos % SEL_BLOCK, pos % SEL_BLOCK]), BF16)


def _aug_rows(parts, total, lead):
    parts = [jnp.broadcast_to(p, lead + p.shape[-2:]) for p in parts]
    used = sum(p.shape[-2] for p in parts)
    if total > used:
        parts.append(jnp.zeros(lead + (total - used, parts[0].shape[-1]), BF16))
    return jnp.concatenate(parts, axis=-2)


def _nsa_prompt(q_pad, gates, ck, cv, ks, vs, kw, vw, tq, tk):
    b, t, _ = q_pad.shape
    g, r = NSA_KV_HEADS, NSA_GROUP
    ncp = ck.shape[3]
    nsp = max(LANES, t // SEL_BLOCK)
    kv = lambda a: pl.BlockSpec((1, 1) + a.shape[2:], lambda bi, gi, i: (bi, gi, 0, 0))
    ovt = _overlap_matrix(ncp, nsp).T
    return pl.pallas_call(
        functools.partial(_nsa_prompt_kernel, tq=tq, tk=tk, ncp=ncp, nsp=nsp),
        grid=(b, g, t // tq),
        in_specs=[pl.BlockSpec((1, tq, r * LANES), lambda bi, gi, i: (bi, i, gi)),
                  pl.BlockSpec((1, tq, LANES), lambda bi, gi, i: (bi, i, gi)),
                  kv(ck), kv(cv), kv(ks), kv(vs), kv(kw), kv(vw),
                  pl.BlockSpec((nsp, ncp), lambda bi, gi, i: (0, 0))],
        out_specs=pl.BlockSpec((1, 1, r, tq, NSA_HD), lambda bi, gi, i: (bi, gi, 0, i, 0)),
        out_shape=jax.ShapeDtypeStruct((b, g, r, t, NSA_HD), F32),
        scratch_shapes=[pltpu.SMEM((nsp // (tk // SEL_BLOCK),), jnp.int32)],
        compiler_params=_cparams(("parallel", "parallel", "arbitrary")),
        name="nsa_prompt",
    )(q_pad, gates, ck, cv, ks, vs, kw, vw, ovt)


def _gla_prompt_kernel(q_ref, k_ref, la_ref, v_ref, o_ref, s_ref, s_scr, *, nch):
    c = GLA_CHUNK
    ti = pl.program_id(2)

    @pl.when(ti == 0)
    def _():
        s_scr[...] = jnp.zeros_like(s_scr)

    two_dk, two_dv = 2 * GLA_DK, 2 * GLA_DV
    r_i = lax.broadcasted_iota(jnp.int32, (c, c), 0)
    c_i = lax.broadcasted_iota(jnp.int32, (c, c), 1)
    tril = jnp.where(c_i <= r_i, 1.0, 0.0).astype(BF16)
    e_r = lax.broadcasted_iota(jnp.int32, (two_dk, two_dk), 0)
    e_c = lax.broadcasted_iota(jnp.int32, (two_dk, two_dk), 1)
    eye = jnp.where(e_r == e_c, 1.0, 0.0).astype(BF16)
    b_r = lax.broadcasted_iota(jnp.int32, (two_dk, two_dv), 0) // GLA_DK
    b_c = lax.broadcasted_iota(jnp.int32, (two_dk, two_dv), 1) // GLA_DV
    same_head = jnp.where(b_r == b_c, 1.0, 0.0)
    ones_bd = same_head.astype(BF16)
    head0 = lax.broadcasted_iota(jnp.int32, (1, two_dk), 1) < GLA_DK
    pos = lax.broadcasted_iota(jnp.int32, (c, 1), 0)
    nsub = c // GLA_SUB
    w_i = lax.broadcasted_iota(jnp.int32, (1, GLA_SUB, 1), 1)

    def chunk(ci):
        rows = pl.ds(pl.multiple_of(ci * c, c), c)
        q = q_ref[0, rows, :]
        k = k_ref[0, rows, :]
        la = la_ref[0, rows, :]
        v = v_ref[0, rows, :]
        vb = v.astype(BF16)
        s0 = s_scr[...]
        a1, a2, a3 = _split3(la)
        cum = _dot(tril, a1) + _dot(tril, a2) + _dot(tril, a3)
        last = cum[c - 1:c, :]
        o = _dot((q * jnp.exp(cum)).astype(BF16), s0.astype(BF16))
        attn = jnp.zeros((2 * c, c), F32)
        for j in range(nsub - 1):
            ce = cum[GLA_SUB * (j + 1) - 1:GLA_SUB * (j + 1), :]
            qh = jnp.where(pos >= GLA_SUB * (j + 1), q * jnp.exp(jnp.minimum(cum - ce, 0.0)), 0.0)
            in_j = jnp.where(pos >= GLA_SUB * j, jnp.where(pos < GLA_SUB * (j + 1), 1.0, 0.0), 0.0)
            kh = in_j * (k * jnp.exp(jnp.minimum(ce - cum, 0.0)))
            qh2 = jnp.concatenate([jnp.where(head0, qh, 0.0), jnp.where(head0, 0.0, qh)], axis=0)
            attn = attn + _dot_nt(qh2.astype(BF16), kh.astype(BF16))
        o_off = _dot(attn.astype(BF16), vb)
        o = o + jnp.concatenate([o_off[:c, :GLA_DV], o_off[c:, GLA_DV:]], axis=1)
        q3 = q.reshape(nsub, GLA_SUB, two_dk)
        k3 = k.reshape(nsub, GLA_SUB, two_dk)
        c3 = cum.reshape(nsub, GLA_SUB, two_dk)
        v3 = v.reshape(nsub, GLA_SUB, two_dv)
        ws = []
        for u in range(GLA_SUB):
            w = q3 * k3[:, u:u + 1, :] * jnp.exp(jnp.minimum(c3 - c3[:, u:u + 1, :], 0.0))
            ws.append(jnp.where(w_i >= u, w, 0.0).reshape(c, two_dk).astype(BF16))
        a_rep = _dot(jnp.concatenate(ws, axis=0), ones_bd)
        od = jnp.zeros((nsub, GLA_SUB, two_dv), F32)
        for u in range(GLA_SUB):
            od = od + a_rep[u * c:(u + 1) * c].reshape(nsub, GLA_SUB, two_dv) * v3[:, u:u + 1, :]
        o_ref[0, rows, :] = o + od.reshape(c, two_dv)
        kd = (k * jnp.exp(last - cum)).astype(BF16)
        kd_t = _dot_nt(eye, kd).astype(BF16)
        l1, l2, l3 = _split3(jnp.broadcast_to(last, (8, two_dk)))
        last_col = (_dot_nt(eye, l1) + _dot_nt(eye, l2) + _dot_nt(eye, l3))[:, 0:1]
        s_scr[...] = jnp.exp(last_col) * s0 + same_head * _dot(kd_t, vb)

    def pair(i, _):
        chunk(2 * i)
        chunk(2 * i + 1)
        return 0

    lax.fori_loop(0, nch // 2, pair, 0)

    @pl.when(ti == pl.num_programs(2) - 1)
    def _():
        s = s_scr[...]
        s_ref[0, 0] = s[:GLA_DK, :GLA_DV]
        s_ref[0, 1] = s[GLA_DK:, GLA_DV:]


def _gla_prompt(q, k, la, v, nch):
    b, t, _ = q.shape
    tc = nch * GLA_CHUNK
    hp = GLA_HEADS // 2
    qs = pl.BlockSpec((1, tc, 2 * GLA_DK), lambda bi, pi, ti: (bi, ti, pi))
    vs = pl.BlockSpec((1, tc, 2 * GLA_DV), lambda bi, pi, ti: (bi, ti, pi))
    return pl.pallas_call(
        functools.partial(_gla_prompt_kernel, nch=nch),
        grid=(b, hp, t // tc),
        in_specs=[qs, qs, qs, vs],
        out_specs=[vs, pl.BlockSpec((1, 2, GLA_DK, GLA_DV), lambda bi, pi, ti: (bi, pi, 0, 0))],
        out_shape=[jax.ShapeDtypeStruct((b, t, GLA_WIDTH), F32),
                   jax.ShapeDtypeStruct((b, GLA_HEADS, GLA_DK, GLA_DV), F32)],
        scratch_shapes=[pltpu.VMEM((2 * GLA_DK, 2 * GLA_DV), F32)],
        compiler_params=_cparams(("parallel", "parallel", "arbitrary")),
        name="gla_prompt",
    )(q, k, la, v)


def _out_kernel(x_ref, gt_ref, on_ref, ogla_ref, og_ref, gn_ref, w_ref, o_ref):
    y = jnp.zeros(x_ref.shape[1:], F32)
    for h in range(NSA_HEADS):
        y = y + _dot(on_ref[0, h].astype(BF16), w_ref[h * NSA_HD:(h + 1) * NSA_HD, :])
    og = og_ref[0]
    ogla = ogla_ref[0]
    for h in range(GLA_HEADS):
        o = ogla[:, h * GLA_DV:(h + 1) * GLA_DV]
        o = o * lax.rsqrt(jnp.mean(o * o, axis=-1, keepdims=True) + EPS) * gn_ref[...]
        gate = og[:, h * GLA_DV:(h + 1) * GLA_DV]
        o = o * (gate * jax.nn.sigmoid(gate))
        y = y + _dot(o.astype(BF16), w_ref[NSA_WIDTH + h * GLA_DV:NSA_WIDTH + (h + 1) * GLA_DV, :])
    o_ref[0] = x_ref[0] + gt_ref[0] * y


def _out_proj(x, gt, o_nsa, o_gla, o_g, gla_norm, w_out, tm):
    bx, tx, d = x.shape
    xs = pl.BlockSpec((1, tm, d), lambda b, i: (b, i, 0))
    return pl.pallas_call(
        _out_kernel,
        grid=(bx, tx // tm),
        in_specs=[xs, _mod_spec(gt, tm, 2),
                  pl.BlockSpec((1, NSA_HEADS, tm, NSA_HD), lambda b, i: (b, 0, i, 0)),
                  pl.BlockSpec((1, tm, GLA_WIDTH), lambda b, i: (b, i, 0)),
                  pl.BlockSpec((1, tm, GLA_WIDTH), lambda b, i: (b, i, 0)),
                  pl.BlockSpec((1, GLA_DV), lambda b, i: (0, 0)),
                  pl.BlockSpec(w_out.shape, lambda b, i: (0, 0))],
        out_specs=xs,
        out_shape=jax.ShapeDtypeStruct(x.shape, F32),
        compiler_params=_cparams(("parallel", "parallel")),
        name="out_proj",
    )(x, gt, o_nsa, o_gla, o_g, gla_norm, w_out)


def _norm_kernel(x_ref, g_ref, o_ref):
    x = x_ref[0]
    o_ref[0] = x * lax.rsqrt(jnp.mean(x * x, axis=-1, keepdims=True) + EPS) * g_ref[...]


def _final_norm(x, g, tm):
    bx, tx, d = x.shape
    xs = pl.BlockSpec((1, tm, d), lambda b, i: (b, i, 0))
    return pl.pallas_call(
        _norm_kernel,
        grid=(bx, tx // tm),
        in_specs=[xs, pl.BlockSpec((1, d), lambda b, i: (0, 0))],
        out_specs=xs,
        out_shape=jax.ShapeDtypeStruct(x.shape, F32),
        compiler_params=_cparams(("parallel", "parallel")),
        name="final_norm",
    )(x, g)


def _nsa_sample_kernel(pt_ref, qall_ref, gate_ref, kvc_ref, kvs_ref, kvw_ref, *rest, n_pages, page, past, wlen):
    del pt_ref
    cmp_pages = rest[:n_pages]
    slc_pages = rest[n_pages:2 * n_pages]
    (cw_ref, w1_ref, pe_ref, b1_ref, w2_ref, ovt_ref, o_ref, x_scr, bias_scr) = rest[2 * n_pages:]
    nh = NSA_HEADS
    ncb = past // CMP_STRIDE
    qa = qall_ref[0]
    qa_f = qa.astype(F32)
    hrow = lax.broadcasted_iota(jnp.int32, (nh, 1), 0)
    slope = jnp.exp2(-(hrow + 1).astype(F32)) * LOG2E
    lane = lax.broadcasted_iota(jnp.int32, (1, LANES), 1)

    @pl.when(pl.program_id(0) == 0)
    def _():
        for kv in range(2):
            acc = jnp.zeros((8, 2 * LANES), F32)
            for s2 in range(CMP_BLOCK // 2):
                pe2 = jnp.broadcast_to(pe_ref[kv, s2:s2 + 1, :], (8, 2 * LANES))
                acc = acc + _dot(pe2.astype(BF16), w1_ref[kv, s2])
            bias_scr[kv] = acc + b1_ref[kv]

    ckv = []
    for kv in range(2):
        for pg in range(n_pages):
            x_scr[kv, pg * page:(pg + 1) * page, :] = cmp_pages[pg][0, 0, kv].reshape(LANES, page).T
        new = kvc_ref[0][:, kv * LANES:(kv + 1) * LANES]
        x_scr[kv, past:past + CMP_STRIDE, :] = jnp.where(
            lax.broadcasted_iota(jnp.int32, (CMP_STRIDE, 1), 0) == 0, new, 0.0)
        acc = jnp.zeros((ncb, 2 * LANES), F32)
        for s2 in range(CMP_BLOCK // 2):
            lhs = jnp.concatenate([x_scr[kv, pl.ds(2 * s2, ncb, stride=CMP_STRIDE), :],
                                   x_scr[kv, pl.ds(2 * s2 + 1, ncb, stride=CMP_STRIDE), :]], axis=1)
            acc = acc + _dot(lhs.astype(BF16), w1_ref[kv, s2])
        h = jax.nn.gelu(acc + bias_scr[kv][0:1])
        ckv.append(_dot(h.astype(BF16), w2_ref[kv]).astype(BF16))
    c_end = lane * CMP_STRIDE + (CMP_BLOCK - 1)
    dist_c = past - c_end
    mask_c = dist_c >= 0
    lg_c = jnp.where(mask_c, _dot_nt(qa, ckv[0]) - slope * dist_c.astype(F32), NEG)
    e_c = jnp.where(mask_c, jnp.exp2(lg_c - jnp.max(lg_c, axis=-1, keepdims=True)), 0.0)
    pc = e_c / jnp.maximum(jnp.sum(e_c, axis=-1, keepdims=True), 1e-30)
    o_cmp = _dot(pc.astype(BF16), ckv[1])

    ps0 = jnp.sum(jnp.where(hrow < NSA_GROUP, pc, 0.0), axis=0, keepdims=True)
    ps1 = jnp.sum(jnp.where(hrow >= NSA_GROUP, pc, 0.0), axis=0, keepdims=True)
    psum = jnp.where(hrow == 0, ps0, jnp.where(hrow == 1, ps1, 0.0))
    psum = jnp.concatenate([psum, jnp.zeros((LANES - nh, LANES), F32)], axis=0)
    p_hi = psum.astype(BF16)
    p_lo = (psum - p_hi.astype(F32)).astype(BF16)
    imp_t = _dot_nt(ovt_ref[...], p_hi) + _dot_nt(ovt_ref[...], p_lo)
    blk = lax.broadcasted_iota(jnp.int32, (LANES, 1), 0)
    cur = past // SEL_BLOCK
    forced = jnp.where(blk == 0, 1.0, 0.0) + jnp.where(blk == cur, 1.0, 0.0) + jnp.where(blk == cur - 1, 1.0, 0.0)
    score = jnp.where(blk * SEL_BLOCK <= past, imp_t + jnp.where(forced > 0.5, FORCE_BONUS, 0.0), NEG)
    sel = _top_k_mask(score, blk.astype(F32), SEL_TOPK, axis=0).T
    sel_h = jnp.where(hrow < NSA_GROUP, sel[0:1], sel[1:2])

    def attend(tiles, new_row, new_neg):
        lgs = [_dot(qa, k_t) + neg for k_t, _, neg in tiles]
        lg_new = jnp.sum(qa_f * new_row[:, :LANES], axis=-1, keepdims=True) + new_neg
        m = lg_new
        for lg in lgs:
            m = jnp.maximum(m, jnp.max(lg, axis=-1, keepdims=True))
        p_new = jnp.exp2(lg_new - m)
        l = p_new
        acc = p_new * new_row[:, LANES:]
        for lg, (_, v_t, _) in zip(lgs, tiles):
            p = jnp.exp2(lg - m)
            l = l + jnp.sum(p, axis=-1, keepdims=True)
            acc = acc + _dot_nt(p.astype(BF16), v_t)
        return acc / l

    per = page // SEL_BLOCK
    tiles = []
    for pg in range(n_pages):
        blkp = slc_pages[pg]
        dist = past - (pg * page + lane)
        selx = jnp.zeros((nh, LANES), F32)
        for u in range(per):
            in_u = jnp.where(lane >= u * SEL_BLOCK, jnp.where(lane < (u + 1) * SEL_BLOCK, 1.0, 0.0), 0.0)
            selx = selx + in_u * sel_h[:, pg * per + u:pg * per + u + 1]
        neg = jnp.where(dist >= 0, (selx - 1.0) * 1e30, NEG) - slope * dist.astype(F32)
        tiles.append((blkp[0, 0, 0].reshape(LANES, page).astype(BF16),
                      blkp[0, 0, 1].reshape(LANES, page).astype(BF16), neg))
    o_slc = attend(tiles, kvs_ref[0], (sel_h[:, cur:cur + 1] - 1.0) * 1e30)

    wlane = lax.broadcasted_iota(jnp.int32, (1, wlen), 1)
    w_pos = past - wlen + wlane
    dist = past - w_pos
    ok = jnp.where(dist >= 0, jnp.where(dist < WINDOW, jnp.where(w_pos >= 0, 1.0, 0.0), 0.0), 0.0)
    neg = (ok - 1.0) * 1e30 - slope * dist.astype(F32)
    tiles = [(cw_ref[0, 0, 0].reshape(LANES, wlen).astype(BF16), cw_ref[0, 0, 1].reshape(LANES, wlen).astype(BF16), neg)]
    o_win = attend(tiles, kvw_ref[0], jnp.zeros((nh, 1), F32))

    gates = gate_ref[0]
    pick = lambda a: jnp.where(hrow < NSA_GROUP, a[:, :NSA_HD], a[:, NSA_HD:])
    o_ref[0] = gates[:, 0:1] * pick(o_cmp) + gates[:, 1:2] * pick(o_slc) + gates[:, 2:3] * pick(o_win)


def _nsa_sample(layer, page_table, q_all, gates, kvc, kvs, kvw, cache_c, cache_s, cache_w, cw):
    nb, n_pages = page_table.shape
    depth, n_phys, page = cache_c.shape[:3]
    past = n_pages * page
    wlen = cache_w.shape[2]
    ncb = past // CMP_STRIDE
    cc = cache_c.transpose(0, 1, 3, 4, 5, 2)
    cs = cache_s.transpose(0, 1, 3, 4, 5, 2)
    cwin = cache_w.transpose(0, 1, 3, 4, 5, 2)
    row = lambda w: pl.BlockSpec((1, 1, w), lambda b, pt: (b, 0, 0))
    full = lambda a: pl.BlockSpec(a.shape, lambda b, pt: (0,) * a.ndim)
    page_specs = [pl.BlockSpec((1, 1, 2, NSA_KV_HEADS, NSA_HD, page), lambda b, pt, j=j: (layer, pt[b, j], 0, 0, 0, 0))
                  for j in range(n_pages)]
    ovt = _overlap_matrix(ncb, LANES).T
    wargs = (cw["w1s"], cw["pes"], cw["b1s"], cw["w2s"], ovt)
    grid_spec = pltpu.PrefetchScalarGridSpec(
        num_scalar_prefetch=1,
        grid=(nb,),
        in_specs=[pl.BlockSpec((1, NSA_HEADS, LANES), lambda b, pt: (b, 0, 0)),
                  pl.BlockSpec((1, NSA_HEADS, LANES), lambda b, pt: (b, 0, 0)),
                  row(KV_COLS), row(KV_COLS), row(KV_COLS)] + page_specs + page_specs
                 + [pl.BlockSpec((1, 1, 2, NSA_KV_HEADS, NSA_HD, wlen), lambda b, pt: (layer, b, 0, 0, 0, 0))]
                 + [full(a) for a in wargs],
        out_specs=pl.BlockSpec((1, NSA_HEADS, NSA_HD), lambda b, pt: (b, 0, 0)),
        scratch_shapes=[pltpu.VMEM((2, past + CMP_STRIDE, LANES), F32), pltpu.VMEM((2, 8, 2 * LANES), F32)],
    )
    return pl.pallas_call(
        functools.partial(_nsa_sample_kernel, n_pages=n_pages, page=page, past=past, wlen=wlen),
        grid_spec=grid_spec,
        out_shape=jax.ShapeDtypeStruct((nb, NSA_HEADS, NSA_HD), F32),
        compiler_params=_cparams(("arbitrary",)),
        name="nsa_sample",
    )(page_table, q_all, gates, kvc, kvs, kvw, *([cc] * n_pages), *([cs] * n_pages), cwin, *wargs)


def _gla_sample_kernel(col_ref, v_ref, s_ref, o_ref, sn_ref, *, bb):
    for b in range(bb):
        cols = col_ref[b]
        for h in range(GLA_HEADS):
            a = jnp.exp(cols[:, h:h + 1])
            k = cols[:, GLA_HEADS + h:GLA_HEADS + h + 1]
            q = cols[:, 2 * GLA_HEADS + h:2 * GLA_HEADS + h + 1]
            v = v_ref[b, h:h + 1, :]
            s0 = s_ref[b, h]
            qk = jnp.sum(q * k, axis=0, keepdims=True)
            o_ref[b, h:h + 1, :] = jnp.sum((q * a) * s0, axis=0, keepdims=True) + qk * v
            sn_ref[b, h] = a * s0 + k * v


def _gla_sample(cols, v, state, bb):
    nb = v.shape[0]
    return pl.pallas_call(
        functools.partial(_gla_sample_kernel, bb=bb),
        grid=(nb // bb,),
        in_specs=[pl.BlockSpec((bb, GLA_DK, 16), lambda i: (i, 0, 0)),
                  pl.BlockSpec((bb, GLA_HEADS, GLA_DV), lambda i: (i, 0, 0)),
                  pl.BlockSpec((bb, GLA_HEADS, GLA_DK, GLA_DV), lambda i: (i, 0, 0, 0))],
        out_specs=[pl.BlockSpec((bb, GLA_HEADS, GLA_DV), lambda i: (i, 0, 0)),
                   pl.BlockSpec((bb, GLA_HEADS, GLA_DK, GLA_DV), lambda i: (i, 0, 0, 0))],
        out_shape=[jax.ShapeDtypeStruct((nb, GLA_HEADS, GLA_DV), F32),
                   jax.ShapeDtypeStruct((nb, GLA_HEADS, GLA_DK, GLA_DV), F32)],
        compiler_params=_cparams(("parallel",)),
        name="gla_sample",
    )(cols, v, state)


def _proj_weight(w):
    d = w.shape[0]
    offs = np.cumsum((0,) + IN_SIZES)
    seg = lambda i: w[:, offs[i]:offs[i + 1]]

    def slots(a, n):
        per = a.shape[1] // n
        return jnp.pad(a.reshape(d, n, per), ((0, 0), (0, 0), (0, LANES - per))).reshape(d, n * LANES)

    return jnp.concatenate([slots(seg(0), NSA_HEADS), seg(1), seg(2), seg(3), slots(seg(4), NSA_KV_HEADS),
                            seg(5), seg(6), seg(7), slots(seg(8), 1), seg(9)], axis=1)


def _cmp_weights(pe, w1, b1, w2):
    g = NSA_KV_HEADS
    eye = jnp.eye(2, dtype=F32)
    eg = jnp.eye(g, dtype=F32)
    w1r = w1.reshape(2, 2, CMP_STRIDE, NSA_HD, CMP_HIDDEN)
    big = jnp.einsum("krsdh,ka,gb->rskgdabh", w1r, eye, eg).reshape(2, CMP_STRIDE * KV_COLS, 2 * g * CMP_HIDDEN)
    w2big = jnp.einsum("khd,ka,gb->kghabd", w2, eye, eg).reshape(2 * g * CMP_HIDDEN, KV_COLS)
    per = pe.reshape(2, 2, CMP_STRIDE, NSA_HD)
    pet = jnp.broadcast_to(per.transpose(1, 2, 0, 3)[:, :, :, None, :], (2, CMP_STRIDE, 2, g, NSA_HD))
    pet = pet.reshape(2, 1, CMP_STRIDE * KV_COLS)
    b1big = jnp.broadcast_to(b1[:, None, :], (2, g, CMP_HIDDEN)).reshape(1, 2 * g * CMP_HIDDEN)
    w1s = jnp.einsum("ksdh,gb->ksgdbh", w1.reshape(2, CMP_BLOCK, NSA_HD, CMP_HIDDEN), eg)
    w1s = w1s.reshape(2, CMP_BLOCK // 2, 2 * g * NSA_HD, g * CMP_HIDDEN)
    w2s = jnp.einsum("khd,gb->kghbd", w2, eg).reshape(2, g * CMP_HIDDEN, g * NSA_HD)
    pes = jnp.broadcast_to(pe[:, :, None, :], (2, CMP_BLOCK, g, NSA_HD)).reshape(2, CMP_BLOCK // 2, 2 * g * NSA_HD)
    b1s = jnp.broadcast_to(b1[:, None, None, :], (2, 1, g, CMP_HIDDEN)).reshape(2, 1, g * CMP_HIDDEN)
    return {
        "w1s": w1s.astype(BF16), "w2s": w2s.astype(BF16), "pes": pes, "b1s": b1s,
        "w1lo_t": big[0].T.astype(BF16), "w1hi_t": big[1].T.astype(BF16),
        "pe_lo": pet[0], "pe_hi": pet[1], "b1_col": b1big.T, "w2_t": w2big.T.astype(BF16),
    }


def _mods(mod, rows):
    m = mod.reshape(mod.shape[0], N_MOD, D_MODEL)
    if rows == 1:
        return [m[:, j][:, None, :] for j in range(N_MOD)]
    return [m[:, j][None, :, :] for j in range(N_MOD)]


def _layer_prompt(x, mods, lw, tm, tq, tk):
    b, t, _ = x.shape
    x = _ffn(x, mods[0], mods[1], mods[2], lw["norm_g"][0:1], lw["f1_in"], lw["f1_out"], tm)
    (qn, kvc, _, _, kvct, kvst, kvwt, kvtb, gn, qg, kg, vg, la, og) = _proj(
        x, mods[3], mods[4], lw["norm_g"][1:2], lw["w_all"], lw["wkvt"], lw["wa2p"], lw["ba"], lw["qb"], tm)
    g = NSA_KV_HEADS
    ncp = t // CMP_STRIDE
    ckvt = _cmp_prompt(kvc.reshape(b, ncp, CMP_STRIDE * KV_COLS), lw["cmp"]).astype(BF16)
    ckvt = ckvt.reshape(b, 2, g, NSA_HD, ncp)
    kvs5 = kvtb[:, :KV_COLS].reshape(b, 2, g, NSA_HD, t)
    kvw5 = kvtb[:, KV_COLS:].reshape(b, 2, g, NSA_HD, t)
    pos_rows = _alibi_k_rows(np.arange(t))
    ones_row = jnp.ones((1, t), BF16)
    onehot = jnp.asarray(np.arange(max(LANES, t // SEL_BLOCK))[:, None] == np.arange(t)[None, :] // SEL_BLOCK, BF16)
    lead = (b, g)
    ck = _aug_rows([ckvt[:, 0], _alibi_k_rows(np.arange(ncp) * CMP_STRIDE + CMP_BLOCK - 1)], LANES, lead)
    ks = _aug_rows([onehot, kvs5[:, 0], pos_rows], onehot.shape[0] + LANES, lead)
    vs = _aug_rows([kvs5[:, 1], ones_row], LANES, lead)
    kw = _aug_rows([kvw5[:, 0], pos_rows], LANES, lead)
    vw = _aug_rows([kvw5[:, 1], ones_row], LANES, lead)
    o_nsa = _nsa_prompt(qn, gn, ck, ckvt[:, 1], ks, vs, kw, vw, tq, tk)
    o_nsa = o_nsa.reshape(b, NSA_HEADS, t, NSA_HD)
    o_gla, s_fin = _gla_prompt(qg, kg, la, vg, min(8, t // GLA_CHUNK))
    x = _out_proj(x, mods[5], o_nsa, o_gla, og, lw["gla_norm"], lw["w_out"], tm)
    x = _ffn(x, mods[6], mods[7], mods[8], lw["norm_g"][2:3], lw["f2_in"], lw["f2_out"], tm)
    win = min(WINDOW, t)
    back = lambda a: a.reshape(b, 2, NSA_KV_HEADS, NSA_HD, a.shape[-1]).transpose(0, 4, 1, 2, 3)
    return x, (back(kvct), back(kvst), back(kvwt[:, :, t - win:]), s_fin)


def _layer_sample(layer, x, mods, lw, cache_c, cache_s, cache_w, state, page_table):
    nb = x.shape[1]
    x = _ffn(x, mods[0], mods[1], mods[2], lw["norm_g"][0:1], lw["f1_in"], lw["f1_out"], nb)
    (qn, kvc, kvs, kvw, kvct, kvst, kvwt, _, gn, qg, kg, vg, la, og) = _proj(
        x, mods[3], mods[4], lw["norm_g"][1:2], lw["w_all"], lw["wkvt"], lw["wa2p"], lw["ba"], lw["qb"], nb)
    q4 = qn[0].reshape(nb, NSA_KV_HEADS, NSA_GROUP, LANES)[..., :NSA_HD]
    q_all = (q4[:, :, :, None, :] * jnp.eye(NSA_KV_HEADS, dtype=BF16)[None, :, None, :, None]).reshape(nb, NSA_HEADS, LANES)
    gates = gn[0].reshape(nb, NSA_KV_HEADS, LANES)[..., :3 * NSA_GROUP].reshape(nb, NSA_HEADS, 3)
    gates = jnp.pad(gates, ((0, 0), (0, 0), (0, LANES - 3)))
    r3 = lambda a: a[0][:, None, :]
    o_nsa = _nsa_sample(layer, page_table, q_all, gates, r3(kvc), r3(kvs), r3(kvw),
                        cache_c, cache_s, cache_w, lw["cmp"])
    o_nsa = o_nsa.transpose(1, 0, 2)[None]
    col = lambda a: a[0].reshape(nb, GLA_HEADS, GLA_DK).transpose(0, 2, 1)
    cols = jnp.concatenate([col(la), col(kg), col(qg), jnp.zeros((nb, GLA_DK, 16 - 3 * GLA_HEADS), F32)], axis=-1)
    o_gla, s_new = _gla_sample(cols, vg[0].reshape(nb, GLA_HEADS, GLA_DV), state, 8)
    o_gla = o_gla.reshape(1, nb, GLA_WIDTH)
    x = _out_proj(x, mods[5], o_nsa, o_gla, og, lw["gla_norm"], lw["w_out"], nb)
    x = _ffn(x, mods[6], mods[7], mods[8], lw["norm_g"][2:3], lw["f2_in"], lw["f2_out"], nb)
    kv5 = lambda a: a[0].reshape(2, NSA_KV_HEADS, NSA_HD, nb).transpose(3, 0, 1, 2)[:, None]
    return x, (kv5(kvct), kv5(kvst), kv5(kvwt), s_new)


def kernel(x_prompt, x_sample, c_prompt, c_sample, cache_kv_cmp, cache_kv_slc, cache_kv_win, state_gla, page_table, w_ada, b_ada, norm_g, ffn1_w_in, ffn1_w_out, w_in, cmp_pe, cmp_w1, cmp_b1, cmp_w2, gla_wa2, gla_ba, gla_norm, w_out, ffn2_w_in, ffn2_w_out, final_norm):
    depth = w_ada.shape[0]
    b, t, d = x_prompt.shape
    nb = x_sample.shape[0]
    bp = -(-b // 8) * 8
    c_all = jnp.concatenate([c_prompt, jnp.zeros((bp - b, d), F32), c_sample], axis=0)
    mod_all = _ada(c_all, w_ada, b_ada)

    tm = min(512, t)
    tq = min(256, t)
    tk = min(512, t)
    xp = x_prompt
    xs = x_sample.reshape(1, nb, d)
    outs_p, outs_s = [], []
    for l in range(depth):
        lw = {
            "norm_g": norm_g[l],
            "f1_in": ffn1_w_in[l].astype(BF16), "f1_out": ffn1_w_out[l].astype(BF16),
            "f2_in": ffn2_w_in[l].astype(BF16), "f2_out": ffn2_w_out[l].astype(BF16),
            "w_all": _proj_weight(w_in[l]).astype(BF16),
            "qb": _q_bias(),
            "wkvt": w_in[l][:, NSA_WIDTH:NSA_WIDTH + 3 * KV_COLS].T.astype(BF16),
            "wa2p": jnp.pad(gla_wa2[l], ((0, 128 - GLA_RANK), (0, 0))).astype(BF16),
            "ba": gla_ba[l][None, :],
            "gla_norm": gla_norm[l][None, :],
            "w_out": w_out[l].astype(BF16),
            "cmp": _cmp_weights(cmp_pe[l], cmp_w1[l], cmp_b1[l], cmp_w2[l]),
        }
        xp, st_p = _layer_prompt(xp, _mods(mod_all[l, :b], 1), lw, tm, tq, tk)
        xs, st_s = _layer_sample(l, xs, _mods(mod_all[l, bp:], nb), lw, cache_kv_cmp, cache_kv_slc,
                                 cache_kv_win, state_gla[l], page_table)
        outs_p.append(st_p)
        outs_s.append(st_s)
    fn = final_norm[None, :]
    y_prompt = _final_norm(xp, fn, tm)
    y_sample = _final_norm(xs, fn, nb).reshape(nb, 1, d)
    stack = lambda outs, j: jnp.stack([o[j] for o in outs])
    return (y_prompt, y_sample,
            stack(outs_p, 0), stack(outs_p, 1), stack(outs_p, 2), stack(outs_p, 3),
            stack(outs_s, 0), stack(outs_s, 1), stack(outs_s, 2), stack(outs_s, 3))
```

```python
import functools

import numpy as np
import jax
import jax.numpy as jnp
from jax import lax
from jax.experimental import pallas as pl
from jax.experimental.pallas import tpu as pltpu

F32 = jnp.float32
BF16 = jnp.bfloat16

D_MODEL = 1024
NSA_HD = 64
NSA_HEADS = 8
NSA_KV_HEADS = 2
NSA_GROUP = 4
NSA_WIDTH = NSA_HEADS * NSA_HD
CMP_BLOCK = 32
CMP_STRIDE = 16
CMP_HIDDEN = 2 * NSA_HD
SEL_BLOCK = 64
SEL_TOPK = 16
WINDOW = 512
FORCE_BONUS = 1000.0
GLA_HEADS = 4
GLA_DV = 128
GLA_DK = 64
GLA_WIDTH = GLA_HEADS * GLA_DV
GLA_RANK = 16
GLA_TAU = 16.0
GLA_CHUNK = 64
GLA_SUB = 16
FFN_DIM = 2816
N_MOD = 9
EPS = 1e-6
KV_COLS = 2 * NSA_KV_HEADS * NSA_HD
IN_SIZES = (NSA_WIDTH, KV_COLS, KV_COLS, KV_COLS, 3 * NSA_HEADS,
            GLA_HEADS * GLA_DK, GLA_HEADS * GLA_DK, GLA_WIDTH, GLA_RANK, GLA_WIDTH)
IN_PAD = (NSA_HEADS * 128, KV_COLS, KV_COLS, KV_COLS, NSA_KV_HEADS * 128,
          GLA_HEADS * GLA_DK, GLA_HEADS * GLA_DK, GLA_WIDTH, 128, GLA_WIDTH)
IN_OFF = tuple(int(v) for v in np.cumsum((0,) + IN_PAD))

LANES = 128
NEG = -1e30
LOG2E = 1.4426950408889634
VMEM_LIMIT = 56 * 1024 * 1024
FFN_TF = FFN_DIM // 2
NSA_SAMPLE_SEQS = 2


def _cparams(sem):
    return pltpu.CompilerParams(dimension_semantics=sem, vmem_limit_bytes=VMEM_LIMIT)


def _dot(a, b):
    return jnp.dot(a, b, preferred_element_type=F32)


def _dot_nt(a, b):
    return lax.dot_general(a, b, (((1,), (1,)), ((), ())), preferred_element_type=F32)


def _split3(x):
    x1 = x.astype(BF16)
    r = x - x1.astype(F32)
    x2 = r.astype(BF16)
    x3 = (r - x2.astype(F32)).astype(BF16)
    return x1, x2, x3


def _rms_mod(x, g, sc, sh):
    y = x * lax.rsqrt(jnp.mean(x * x, axis=-1, keepdims=True) + EPS) * g
    return y * (1.0 + sc) + sh


def _masked_softmax(lg, mask):
    lg = jnp.where(mask, lg, NEG)
    m = jnp.max(lg, axis=-1, keepdims=True)
    e = jnp.where(mask, jnp.exp(lg - m), 0.0)
    return e / jnp.maximum(jnp.sum(e, axis=-1, keepdims=True), 1e-30)


def _top_k_mask(score, blk, k, axis=-1):
    sel = jnp.zeros_like(score)
    big = float(score.shape[axis])
    for _ in range(k):
        m = jnp.max(score, axis=axis, keepdims=True)
        idx = jnp.min(jnp.where(score == m, blk, big), axis=axis, keepdims=True)
        hit = blk == idx
        sel = jnp.where(hit, 1.0, sel)
        score = jnp.where(hit, -jnp.inf, score)
    return sel


def _ada_kernel(c_ref, w_ref, b_ref, o_ref):
    c = c_ref[...]
    s = c * jax.nn.sigmoid(c)
    o_ref[0] = _dot(s.astype(BF16), w_ref[0].astype(BF16)) + b_ref[0]


def _ada(c_all, w_ada, b_ada):
    depth, d, n = w_ada.shape
    m = c_all.shape[0]
    tn = 1024
    return pl.pallas_call(
        _ada_kernel,
        grid=(depth, n // tn),
        in_specs=[pl.BlockSpec((m, d), lambda l, j: (0, 0)),
                  pl.BlockSpec((1, d, tn), lambda l, j: (l, 0, j)),
                  pl.BlockSpec((1, 1, tn), lambda l, j: (l, 0, j))],
        out_specs=pl.BlockSpec((1, m, tn), lambda l, j: (l, 0, j)),
        out_shape=jax.ShapeDtypeStruct((depth, m, n), F32),
        compiler_params=_cparams(("parallel", "parallel")),
        name="ada",
    )(c_all, w_ada, b_ada.reshape(depth, 1, n))


def _mod_spec(mod, tm, nargs):
    r = mod.shape[1]
    d = mod.shape[2]
    if nargs == 3:
        if r == 1:
            return pl.BlockSpec((1, 1, d), lambda b, i, f: (b, 0, 0))
        return pl.BlockSpec((1, tm, d), lambda b, i, f: (b, i, 0))
    if r == 1:
        return pl.BlockSpec((1, 1, d), lambda b, i: (b, 0, 0))
    return pl.BlockSpec((1, tm, d), lambda b, i: (b, i, 0))


def _ffn_kernel(x_ref, sh_ref, sc_ref, gt_ref, g_ref, wi_ref, wo_ref, fg_ref, o_ref, *, final):
    x = x_ref[0]
    h = _rms_mod(x, g_ref[...], sc_ref[0], sh_ref[0]).astype(BF16)
    y = jnp.zeros(x.shape, F32)
    for f in range(FFN_DIM // FFN_TF):
        g = _dot(h, wi_ref[:, f * FFN_TF:(f + 1) * FFN_TF])
        u = _dot(h, wi_ref[:, FFN_DIM + f * FFN_TF:FFN_DIM + (f + 1) * FFN_TF])
        a = (g * jax.nn.sigmoid(g) * u).astype(BF16)
        y = y + _dot(a, wo_ref[f * FFN_TF:(f + 1) * FFN_TF, :])
    x = x + 0.5 * gt_ref[0] * y
    if final:
        x = x * lax.rsqrt(jnp.mean(x * x, axis=-1, keepdims=True) + EPS) * fg_ref[...]
    o_ref[0] = x


def _ffn(x, sh, sc, gt, g, w_in, w_out, tm, final_gain=None):
    bx, tx, d = x.shape
    xs = pl.BlockSpec((1, tm, d), lambda b, i: (b, i, 0))
    ms = _mod_spec(sh, tm, 2)
    once = lambda a: pl.BlockSpec(a.shape, lambda b, i: (0,) * a.ndim, pipeline_mode=pl.Buffered(1))
    fg = g if final_gain is None else final_gain
    return pl.pallas_call(
        functools.partial(_ffn_kernel, final=final_gain is not None),
        grid=(bx, tx // tm),
        in_specs=[xs, ms, ms, ms, once(g), once(w_in), once(w_out), once(fg)],
        out_specs=xs,
        out_shape=jax.ShapeDtypeStruct(x.shape, F32),
        compiler_params=_cparams(("parallel", "parallel")),
        name="ffn",
    )(x, sh, sc, gt, g, w_in, w_out, fg)


def _proj_kernel(x_ref, sh_ref, sc_ref, g_ref, w_ref, wkvt_ref, wa2_ref, ba_ref, qb_ref,
                 qn_ref, kvc_ref, kvs_ref, kvw_ref, kvct_ref, kvst_ref, kvwt_ref, kvtb_ref,
                 gn_ref, qg_ref, kg_ref, vg_ref, la_ref, og_ref):
    h = _rms_mod(x_ref[0], g_ref[...], sc_ref[0], sh_ref[0]).astype(BF16)
    p = _dot(h, w_ref[...])
    o = IN_OFF
    qn_ref[0] = (p[:, o[0]:o[1]] * (NSA_HD ** -0.5 * LOG2E) + qb_ref[...]).astype(BF16)
    kvc_ref[0] = p[:, o[1]:o[2]]
    kvs_ref[0] = p[:, o[2]:o[3]]
    kvw_ref[0] = p[:, o[3]:o[4]]
    kvt = _dot_nt(wkvt_ref[...], h)
    kvct_ref[0] = kvt[0:KV_COLS]
    kvst_ref[0] = kvt[KV_COLS:2 * KV_COLS]
    kvwt_ref[0] = kvt[2 * KV_COLS:3 * KV_COLS]
    kvtb_ref[0] = kvt[KV_COLS:3 * KV_COLS].astype(BF16)
    gn_ref[0] = jax.nn.sigmoid(p[:, o[4]:o[5]])
    qg_ref[0] = p[:, o[5]:o[6]] * (GLA_DK ** -0.5)
    kg_ref[0] = p[:, o[6]:o[7]]
    vg_ref[0] = p[:, o[7]:o[8]]
    a_pre = _dot(p[:, o[8]:o[9]].astype(BF16), wa2_ref[...]) + ba_ref[...]
    log_sig = jnp.minimum(a_pre, 0.0) - jnp.log(1.0 + jnp.exp(-jnp.abs(a_pre)))
    la_ref[0] = log_sig * (1.0 / GLA_TAU)
    og_ref[0] = p[:, o[9]:o[10]]


def _proj(x, sh, sc, g, w_all, wkvt, wa2p, ba, qb, tm):
    bx, tx, d = x.shape
    xs = pl.BlockSpec((1, tm, d), lambda b, i: (b, i, 0))
    ms = _mod_spec(sh, tm, 2)
    full = lambda shape: pl.BlockSpec(shape, lambda b, i: (0,) * len(shape))
    rows = lambda w, dt: (pl.BlockSpec((1, tm, w), lambda b, i: (b, i, 0)), jax.ShapeDtypeStruct((bx, tx, w), dt))
    cols = lambda w, dt: (pl.BlockSpec((1, w, tm), lambda b, i: (b, 0, i)), jax.ShapeDtypeStruct((bx, w, tx), dt))
    outs = [rows(IN_PAD[0], BF16), rows(KV_COLS, F32), rows(KV_COLS, F32), rows(KV_COLS, F32),
            cols(KV_COLS, F32), cols(KV_COLS, F32), cols(KV_COLS, F32), cols(2 * KV_COLS, BF16),
            rows(IN_PAD[4], F32), rows(256, F32), rows(256, F32), rows(GLA_WIDTH, F32), rows(256, F32),
            rows(GLA_WIDTH, F32)]
    return pl.pallas_call(
        _proj_kernel,
        grid=(bx, tx // tm),
        in_specs=[xs, ms, ms, full((1, d)), full(w_all.shape), full(wkvt.shape), full(wa2p.shape),
                  full(ba.shape), full(qb.shape)],
        out_specs=[o[0] for o in outs],
        out_shape=[o[1] for o in outs],
        compiler_params=_cparams(("parallel", "parallel")),
        name="proj",
    )(x, sh, sc, g, w_all, wkvt, wa2p, ba, qb)


def _cmp_kernel(x_ref, pelo_ref, pehi_ref, w1lo_ref, w1hi_ref, b1_ref, w2_ref, o_ref):
    x = x_ref[0]
    n = x.shape[0]
    h_lo = _dot_nt(w1lo_ref[...], (x + pelo_ref[...]).astype(BF16))
    h_hi = _dot_nt(w1hi_ref[...], (x + pehi_ref[...]).astype(BF16))
    h = h_lo + pltpu.roll(h_hi, n - 1, 1) + b1_ref[...]
    o_ref[0] = _dot(w2_ref[...], jax.nn.gelu(h).astype(BF16))


def _cmp_prompt(x16, cw):
    b, n, w = x16.shape
    full = lambda a: pl.BlockSpec(a.shape, lambda i: (0,) * a.ndim)
    args = (cw["pe_lo"], cw["pe_hi"], cw["w1lo_t"], cw["w1hi_t"], cw["b1_col"], cw["w2_t"])
    return pl.pallas_call(
        _cmp_kernel,
        grid=(b,),
        in_specs=[pl.BlockSpec((1, n, w), lambda i: (i, 0, 0))] + [full(a) for a in args],
        out_specs=pl.BlockSpec((1, KV_COLS, n), lambda i: (i, 0, 0)),
        out_shape=jax.ShapeDtypeStruct((b, KV_COLS, n), F32),
        compiler_params=_cparams(("parallel",)),
        name="cmp_prompt",
    )(x16, *args)


def _nsa_prompt_kernel(q_ref, gate_ref, ck_ref, cv_ref, ks_ref, vs_ref, kw_ref, vw_ref, ovt_ref,
                       o_ref, ids_scr, *, tq, tk, ncp, nsp):
    i = pl.program_id(2)
    q0 = i * tq
    nr = NSA_GROUP * tq
    row = q0 + jnp.bitwise_and(lax.broadcasted_iota(jnp.int32, (nr, 1), 0), tq - 1)
    row_t = q0 + lax.broadcasted_iota(jnp.int32, (tq, 1), 0)
    qblk = q_ref[0]
    q = jnp.concatenate([qblk[:, r * LANES:(r + 1) * LANES] for r in range(NSA_GROUP)], axis=0)

    def per_head(x, mask):
        return (x.reshape(NSA_GROUP, tq, x.shape[-1]) + mask[None]).reshape(x.shape)

    c_end = lax.broadcasted_iota(jnp.int32, (1, ncp), 1) * CMP_STRIDE + (CMP_BLOCK - 1)
    lg = per_head(_dot(q, ck_ref[0, 0]), jnp.where(c_end <= row_t, 0.0, NEG))
    e = jnp.exp2(lg - jnp.max(lg, axis=-1, keepdims=True))
    pc = e * jnp.where(row >= CMP_BLOCK - 1, 1.0 / jnp.sum(e, axis=-1, keepdims=True), 0.0)
    o_cmp = _dot_nt(pc.astype(BF16), cv_ref[0, 0])
    psum = pc[0:tq]
    for r in range(1, NSA_GROUP):
        psum = psum + pc[r * tq:(r + 1) * tq]
    p_hi = psum.astype(BF16)
    p_lo = (psum - p_hi.astype(F32)).astype(BF16)
    imp_t = _dot_nt(ovt_ref[...], p_hi) + _dot_nt(ovt_ref[...], p_lo)

    blk = lax.broadcasted_iota(jnp.int32, (nsp, 1), 0)
    t_lane = q0 + lax.broadcasted_iota(jnp.int32, (1, tq), 1)
    cur = jnp.right_shift(t_lane, 6)
    forced = jnp.where(blk == 0, 1.0, 0.0) + jnp.where(blk == cur, 1.0, 0.0) + jnp.where(blk == cur - 1, 1.0, 0.0)
    valid = blk * SEL_BLOCK <= t_lane
    score = jnp.where(valid, imp_t + jnp.where(forced > 0.5, FORCE_BONUS, 0.0), NEG)
    sel_t = _top_k_mask(score, blk.astype(F32), SEL_TOPK, axis=0)
    sel_t = jnp.where(valid, sel_t, 0.0)
    sel_bias = ((sel_t - 1.0) * 1e30).T.astype(BF16)
    qa = jnp.concatenate([jnp.concatenate([sel_bias] * NSA_GROUP, axis=0), q], axis=1)

    n_full = q0 // tk
    per_tile = tk // SEL_BLOCK
    n_tiles = nsp // per_tile
    blk_any = jnp.max(sel_t, axis=1, keepdims=True)
    tile_any = jnp.max(blk_any.reshape(n_tiles, per_tile, 1), axis=1)
    bit = jnp.left_shift(1, lax.broadcasted_iota(jnp.int32, (n_tiles, 1), 0))
    tile_bits = jnp.sum(jnp.where(tile_any > 0.0, bit, 0))
    n_used = jnp.int32(0)
    for kt in range(min(n_tiles, ks_ref.shape[3] // tk)):
        use = jnp.logical_and(jnp.bitwise_and(jnp.right_shift(tile_bits, kt), 1) == 1, kt < n_full)
        ids_scr[n_used] = kt
        n_used = n_used + use.astype(jnp.int32)

    def sel_tile(kt, carry, diag):
        m, acc = carry
        k0 = pl.multiple_of(kt * tk, tk)
        lg = _dot(qa, ks_ref[0, 0, :, pl.ds(k0, tk)])
        if diag:
            lg = jnp.where(k0 + lax.broadcasted_iota(jnp.int32, (1, tk), 1) <= row, lg, NEG)
        m_new = jnp.maximum(m, jnp.max(lg, axis=-1, keepdims=True))
        p = jnp.exp2(lg - m_new).astype(BF16)
        return m_new, jnp.exp2(m - m_new) * acc + _dot_nt(p, vs_ref[0, 0, :, pl.ds(k0, tk)])

    carry = sel_tile(n_full, (jnp.full((nr, 1), NEG, F32), jnp.zeros((nr, LANES), F32)), True)
    _, acc = lax.fori_loop(0, n_used, lambda j, c: sel_tile(ids_scr[j], c, False), carry)
    o_slc = acc[:, :NSA_HD] / acc[:, NSA_HD:NSA_HD + 1]

    wl = WINDOW + tq
    w0 = pl.multiple_of(jnp.maximum(q0 - WINDOW, 0), tq)
    dist_w = row_t - (w0 + lax.broadcasted_iota(jnp.int32, (1, wl), 1))
    neg_w = jnp.where(dist_w >= 0, jnp.where(dist_w < WINDOW, 0.0, NEG), NEG)
    lg = per_head(_dot(q, kw_ref[0, 0, :, pl.ds(w0, wl)]), neg_w)
    p = jnp.exp2(lg - jnp.max(lg, axis=-1, keepdims=True)).astype(BF16)
    ow = _dot_nt(p, vw_ref[0, 0, :, pl.ds(w0, wl)])
    o_win = ow[:, :NSA_HD] / ow[:, NSA_HD:NSA_HD + 1]

    gates = gate_ref[0]
    gcol = lambda j: jnp.concatenate([gates[:, 3 * r + j:3 * r + j + 1] for r in range(NSA_GROUP)], axis=0)
    o = gcol(0) * o_cmp + gcol(1) * o_slc + gcol(2) * o_win
    o_ref[0, 0] = o.reshape(NSA_GROUP, tq, NSA_HD)


def _overlap_matrix(ncp, nsp):
    c = np.arange(ncp)[:, None] * CMP_STRIDE
    s = np.arange(nsp)[None, :] * SEL_BLOCK
    return jnp.asarray((c <= s + SEL_BLOCK - 1) & (c + CMP_BLOCK - 1 >= s), dtype=BF16)


def _bf16_round(x):
    return np.asarray(x, dtype=BF16).astype(np.float32)


def _q_bias():
    slopes = np.exp2(-8.0 * np.arange(1, NSA_HEADS + 1) / NSA_HEADS)
    c_hi = float(_bf16_round(LOG2E))
    c_lo = float(_bf16_round(LOG2E - c_hi))
    qb = np.zeros((NSA_HEADS, LANES), np.float32)
    qb[:, NSA_HD:NSA_HD + 4] = np.stack([c_hi * slopes * SEL_BLOCK, c_lo * slopes * SEL_BLOCK,
                                         c_hi * slopes, c_lo * slopes], axis=1)
    return jnp.asarray(qb.reshape(1, NSA_HEADS * LANES))


def _alibi_k_rows(pos):
    pos = np.asarray(pos)
    return jnp.asarray(np.stack([pos // SEL_BLOCK, pos // SEL_BLOCK, pos % SEL_BLOCK, pos % SEL_BLOCK]), BF16)


def _aug_rows(parts, total, lead):
    parts = [jnp.broadcast_to(p, lead + p.shape[-2:]) for p in parts]
    used = sum(p.shape[-2] for p in parts)
    if total > used:
        parts.append(jnp.zeros(lead + (total - used, parts[0].shape[-1]), BF16))
    return jnp.concatenate(parts, axis=-2)


def _nsa_prompt(q_pad, gates, ck, cv, ks, vs, kw, vw, tq, tk):
    b, t, _ = q_pad.shape
    g, r = NSA_KV_HEADS, NSA_GROUP
    ncp = ck.shape[3]
    nsp = max(LANES, t // SEL_BLOCK)
    kv = lambda a: pl.BlockSpec((1, 1) + a.shape[2:], lambda bi, gi, i: (bi, gi, 0, 0))
    ovt = _overlap_matrix(ncp, nsp).T
    return pl.pallas_call(
        functools.partial(_nsa_prompt_kernel, tq=tq, tk=tk, ncp=ncp, nsp=nsp),
        grid=(b, g, t // tq),
        in_specs=[pl.BlockSpec((1, tq, r * LANES), lambda bi, gi, i: (bi, i, gi)),
                  pl.BlockSpec((1, tq, LANES), lambda bi, gi, i: (bi, i, gi)),
                  kv(ck), kv(cv), kv(ks), kv(vs), kv(kw), kv(vw),
                  pl.BlockSpec((nsp, ncp), lambda bi, gi, i: (0, 0))],
        out_specs=pl.BlockSpec((1, 1, r, tq, NSA_HD), lambda bi, gi, i: (bi, gi, 0, i, 0)),
        out_shape=jax.ShapeDtypeStruct((b, g, r, t, NSA_HD), F32),
        scratch_shapes=[pltpu.SMEM((nsp // (tk // SEL_BLOCK),), jnp.int32)],
        compiler_params=_cparams(("parallel", "parallel", "arbitrary")),
        name="nsa_prompt",
    )(q_pad, gates, ck, cv, ks, vs, kw, vw, ovt)


def _gla_prompt_kernel(q_ref, k_ref, la_ref, v_ref, o_ref, s_ref, s_scr, *, nch):
    c = GLA_CHUNK
    ti = pl.program_id(2)

    @pl.when(ti == 0)
    def _():
        s_scr[...] = jnp.zeros_like(s_scr)

    two_dk, two_dv = 2 * GLA_DK, 2 * GLA_DV
    r_i = lax.broadcasted_iota(jnp.int32, (c, c), 0)
    c_i = lax.broadcasted_iota(jnp.int32, (c, c), 1)
    tril = jnp.where(c_i <= r_i, 1.0, 0.0).astype(BF16)
    e_r = lax.broadcasted_iota(jnp.int32, (two_dk, two_dk), 0)
    e_c = lax.broadcasted_iota(jnp.int32, (two_dk, two_dk), 1)
    eye = jnp.where(e_r == e_c, 1.0, 0.0).astype(BF16)
    b_r = lax.broadcasted_iota(jnp.int32, (two_dk, two_dv), 0) // GLA_DK
    b_c = lax.broadcasted_iota(jnp.int32, (two_dk, two_dv), 1) // GLA_DV
    same_head = jnp.where(b_r == b_c, 1.0, 0.0)
    ones_bd = same_head.astype(BF16)
    head0 = lax.broadcasted_iota(jnp.int32, (1, two_dk), 1) < GLA_DK
    pos = lax.broadcasted_iota(jnp.int32, (c, 1), 0)
    nsub = c // GLA_SUB
    w_i = lax.broadcasted_iota(jnp.int32, (1, GLA_SUB, 1), 1)

    def chunk(ci):
        rows = pl.ds(pl.multiple_of(ci * c, c), c)
        q = q_ref[0, rows, :]
        k = k_ref[0, rows, :]
        la = la_ref[0, rows, :]
        v = v_ref[0, rows, :]
        vb = v.astype(BF16)
        s0 = s_scr[...]
        a1, a2, a3 = _split3(la)
        cum = _dot(tril, a1) + _dot(tril, a2) + _dot(tril, a3)
        last = cum[c - 1:c, :]
        o = _dot((q * jnp.exp(cum)).astype(BF16), s0.astype(BF16))
        attn = jnp.zeros((2 * c, c), F32)
        for j in range(nsub - 1):
            ce = cum[GLA_SUB * (j + 1) - 1:GLA_SUB * (j + 1), :]
            qh = jnp.where(pos >= GLA_SUB * (j + 1), q * jnp.exp(jnp.minimum(cum - ce, 0.0)), 0.0)
            in_j = jnp.where(pos >= GLA_SUB * j, jnp.where(pos < GLA_SUB * (j + 1), 1.0, 0.0), 0.0)
            kh = in_j * (k * jnp.exp(jnp.minimum(ce - cum, 0.0)))
            qh2 = jnp.concatenate([jnp.where(head0, qh, 0.0), jnp.where(head0, 0.0, qh)], axis=0)
            attn = attn + _dot_nt(qh2.astype(BF16), kh.astype(BF16))
        o_off = _dot(attn.astype(BF16), vb)
        o = o + jnp.concatenate([o_off[:c, :GLA_DV], o_off[c:, GLA_DV:]], axis=1)
        q3 = q.reshape(nsub, GLA_SUB, two_dk)
        k3 = k.reshape(nsub, GLA_SUB, two_dk)
        c3 = cum.reshape(nsub, GLA_SUB, two_dk)
        v3 = v.reshape(nsub, GLA_SUB, two_dv)
        ws = []
        for u in range(GLA_SUB):
            w = q3 * k3[:, u:u + 1, :] * jnp.exp(jnp.minimum(c3 - c3[:, u:u + 1, :], 0.0))
            ws.append(jnp.where(w_i >= u, w, 0.0).reshape(c, two_dk).astype(BF16))
        a_rep = _dot(jnp.concatenate(ws, axis=0), ones_bd)
        od = jnp.zeros((nsub, GLA_SUB, two_dv), F32)
        for u in range(GLA_SUB):
            od = od + a_rep[u * c:(u + 1) * c].reshape(nsub, GLA_SUB, two_dv) * v3[:, u:u + 1, :]
        o_ref[0, rows, :] = o + od.reshape(c, two_dv)
        kd = (k * jnp.exp(last - cum)).astype(BF16)
        kd_t = _dot_nt(eye, kd).astype(BF16)
        l1, l2, l3 = _split3(jnp.broadcast_to(last, (8, two_dk)))
        last_col = (_dot_nt(eye, l1) + _dot_nt(eye, l2) + _dot_nt(eye, l3))[:, 0:1]
        s_scr[...] = jnp.exp(last_col) * s0 + same_head * _dot(kd_t, vb)

    def pair(i, _):
        chunk(2 * i)
        chunk(2 * i + 1)
        return 0

    lax.fori_loop(0, nch // 2, pair, 0)

    @pl.when(ti == pl.num_programs(2) - 1)
    def _():
        s = s_scr[...]
        s_ref[0, 0] = s[:GLA_DK, :GLA_DV]
        s_ref[0, 1] = s[GLA_DK:, GLA_DV:]


def _gla_prompt(q, k, la, v, nch):
    b, t, _ = q.shape
    tc = nch * GLA_CHUNK
    hp = GLA_HEADS // 2
    qs = pl.BlockSpec((1, tc, 2 * GLA_DK), lambda bi, pi, ti: (bi, ti, pi))
    vs = pl.BlockSpec((1, tc, 2 * GLA_DV), lambda bi, pi, ti: (bi, ti, pi))
    return pl.pallas_call(
        functools.partial(_gla_prompt_kernel, nch=nch),
        grid=(b, hp, t // tc),
        in_specs=[qs, qs, qs, vs],
        out_specs=[vs, pl.BlockSpec((1, 2, GLA_DK, GLA_DV), lambda bi, pi, ti: (bi, pi, 0, 0))],
        out_shape=[jax.ShapeDtypeStruct((b, t, GLA_WIDTH), F32),
                   jax.ShapeDtypeStruct((b, GLA_HEADS, GLA_DK, GLA_DV), F32)],
        scratch_shapes=[pltpu.VMEM((2 * GLA_DK, 2 * GLA_DV), F32)],
        compiler_params=_cparams(("parallel", "parallel", "arbitrary")),
        name="gla_prompt",
    )(q, k, la, v)


def _out_kernel(x_ref, gt_ref, on_ref, ogla_ref, og_ref, gn_ref, w_ref, o_ref):
    y = jnp.zeros(x_ref.shape[1:], F32)
    for h in range(NSA_HEADS):
        y = y + _dot(on_ref[0, h].astype(BF16), w_ref[h * NSA_HD:(h + 1) * NSA_HD, :])
    og = og_ref[0]
    ogla = ogla_ref[0]
    for h in range(GLA_HEADS):
        o = ogla[:, h * GLA_DV:(h + 1) * GLA_DV]
        o = o * lax.rsqrt(jnp.mean(o * o, axis=-1, keepdims=True) + EPS) * gn_ref[...]
        gate = og[:, h * GLA_DV:(h + 1) * GLA_DV]
        o = o * (gate * jax.nn.sigmoid(gate))
        y = y + _dot(o.astype(BF16), w_ref[NSA_WIDTH + h * GLA_DV:NSA_WIDTH + (h + 1) * GLA_DV, :])
    o_ref[0] = x_ref[0] + gt_ref[0] * y


def _out_proj(x, gt, o_nsa, o_gla, o_g, gla_norm, w_out, tm):
    bx, tx, d = x.shape
    xs = pl.BlockSpec((1, tm, d), lambda b, i: (b, i, 0))
    return pl.pallas_call(
        _out_kernel,
        grid=(bx, tx // tm),
        in_specs=[xs, _mod_spec(gt, tm, 2),
                  pl.BlockSpec((1, NSA_HEADS, tm, NSA_HD), lambda b, i: (b, 0, i, 0)),
                  pl.BlockSpec((1, tm, GLA_WIDTH), lambda b, i: (b, i, 0)),
                  pl.BlockSpec((1, tm, GLA_WIDTH), lambda b, i: (b, i, 0)),
                  pl.BlockSpec((1, GLA_DV), lambda b, i: (0, 0)),
                  pl.BlockSpec(w_out.shape, lambda b, i: (0, 0))],
        out_specs=xs,
        out_shape=jax.ShapeDtypeStruct(x.shape, F32),
        compiler_params=_cparams(("parallel", "parallel")),
        name="out_proj",
    )(x, gt, o_nsa, o_gla, o_g, gla_norm, w_out)


def _nsa_sample_kernel(pt_ref, *rest, n_pages, page, past, wlen, nseq):
    del pt_ref
    per = 6 + 2 * n_pages
    (w1_ref, pe_ref, b1_ref, w2_ref, ovt_ref, o_ref, x_scr, bias_scr) = rest[nseq * per:]

    @pl.when(pl.program_id(0) == 0)
    def _():
        for kv in range(2):
            acc = jnp.zeros((8, 2 * LANES), F32)
            for s2 in range(CMP_BLOCK // 2):
                pe2 = jnp.broadcast_to(pe_ref[kv, s2:s2 + 1, :], (8, 2 * LANES))
                acc = acc + _dot(pe2.astype(BF16), w1_ref[kv, s2])
            bias_scr[kv] = acc + b1_ref[kv]

    for s in range(nseq):
        refs = rest[s * per:(s + 1) * per]
        _nsa_sample_seq(refs[:5], refs[5:5 + n_pages], refs[5 + n_pages:5 + 2 * n_pages], refs[5 + 2 * n_pages],
                        (w1_ref, w2_ref, ovt_ref, bias_scr), x_scr.at[s], o_ref.at[s],
                        n_pages=n_pages, page=page, past=past, wlen=wlen)


def _nsa_sample_seq(row_refs, cmp_pages, slc_pages, cw_ref, shared, x_scr, o_ref, *, n_pages, page, past, wlen):
    qall_ref, gate_ref, kvc_ref, kvs_ref, kvw_ref = row_refs
    w1_ref, w2_ref, ovt_ref, bias_scr = shared
    nh = NSA_HEADS
    ncb = past // CMP_STRIDE
    qa = qall_ref[0]
    qa_f = qa.astype(F32)
    hrow = lax.broadcasted_iota(jnp.int32, (nh, 1), 0)
    slope = jnp.exp2(-(hrow + 1).astype(F32)) * LOG2E
    lane = lax.broadcasted_iota(jnp.int32, (1, LANES), 1)

    cpp = page // CMP_STRIDE
    r_i = lax.broadcasted_iota(jnp.int32, (page, page), 0)
    p_i = lax.broadcasted_iota(jnp.int32, (page, page), 1)
    in_chunk = jnp.bitwise_and(p_i, CMP_STRIDE - 1)
    perm_t = jnp.where(r_i * CMP_STRIDE == in_chunk * (cpp * CMP_STRIDE) + (p_i - in_chunk), 1.0, 0.0).astype(BF16)
    nrow = ncb + 8
    ckv = []
    for kv in range(2):
        for pg in range(n_pages):
            rows = _dot_nt(perm_t, cmp_pages[pg][0, 0, kv].reshape(LANES, page).astype(BF16))
            for s in range(CMP_STRIDE):
                x_scr[kv, s, pg * cpp:(pg + 1) * cpp, :] = rows[s * cpp:(s + 1) * cpp, :]
        new = kvc_ref[0][:, kv * LANES:(kv + 1) * LANES]
        first = lax.broadcasted_iota(jnp.int32, (8, 1), 0) == 0
        for s in range(CMP_STRIDE):
            x_scr[kv, s, ncb:nrow, :] = jnp.where(first, new, 0.0) if s == 0 else jnp.zeros((8, LANES), F32)
        acc_lo = jnp.zeros((nrow, 2 * LANES), F32)
        acc_hi = jnp.zeros((nrow, 2 * LANES), F32)
        half = CMP_STRIDE // 2
        for j in range(half):
            lhs = jnp.concatenate([x_scr[kv, 2 * j], x_scr[kv, 2 * j + 1]], axis=1).astype(BF16)
            acc_lo = acc_lo + _dot(lhs, w1_ref[kv, j])
            acc_hi = acc_hi + _dot(lhs, w1_ref[kv, half + j])
        h = jax.nn.gelu(acc_lo[:ncb] + pltpu.roll(acc_hi, nrow - 1, 0)[:ncb] + bias_scr[kv][0:1])
        ckv.append(_dot(h.astype(BF16), w2_ref[kv]).astype(BF16))
    c_end = lane * CMP_STRIDE + (CMP_BLOCK - 1)
    dist_c = past - c_end
    mask_c = dist_c >= 0
    lg_c = jnp.where(mask_c, _dot_nt(qa, ckv[0]) - slope * dist_c.astype(F32), NEG)
    e_c = jnp.where(mask_c, jnp.exp2(lg_c - jnp.max(lg_c, axis=-1, keepdims=True)), 0.0)
    pc = e_c / jnp.maximum(jnp.sum(e_c, axis=-1, keepdims=True), 1e-30)
    o_cmp = _dot(pc.astype(BF16), ckv[1])

    ps0 = jnp.sum(jnp.where(hrow < NSA_GROUP, pc, 0.0), axis=0, keepdims=True)
    ps1 = jnp.sum(jnp.where(hrow >= NSA_GROUP, pc, 0.0), axis=0, keepdims=True)
    psum = jnp.where(hrow == 0, ps0, jnp.where(hrow == 1, ps1, 0.0))
    psum = jnp.concatenate([psum, jnp.zeros((LANES - nh, LANES), F32)], axis=0)
    p_hi = psum.astype(BF16)
    p_lo = (psum - p_hi.astype(F32)).astype(BF16)
    imp_t = _dot_nt(ovt_ref[...], p_hi) + _dot_nt(ovt_ref[...], p_lo)
    blk = lax.broadcasted_iota(jnp.int32, (LANES, 1), 0)
    cur = past // SEL_BLOCK
    forced = jnp.where(blk == 0, 1.0, 0.0) + jnp.where(blk == cur, 1.0, 0.0) + jnp.where(blk == cur - 1, 1.0, 0.0)
    score = jnp.where(blk * SEL_BLOCK <= past, imp_t + jnp.where(forced > 0.5, FORCE_BONUS, 0.0), NEG)
    sel = _top_k_mask(score, blk.astype(F32), SEL_TOPK, axis=0).T
    sel_h = jnp.where(hrow < NSA_GROUP, sel[0:1], sel[1:2])

    def attend(tiles, new_row, new_neg):
        lgs = [_dot(qa, k_t) + neg for k_t, _, neg in tiles]
        lg_new = jnp.sum(qa_f * new_row[:, :LANES], axis=-1, keepdims=True) + new_neg
        m = lg_new
        for lg in lgs:
            m = jnp.maximum(m, jnp.max(lg, axis=-1, keepdims=True))
        p_new = jnp.exp2(lg_new - m)
        l = p_new
        acc = p_new * new_row[:, LANES:]
        for lg, (_, v_t, _) in zip(lgs, tiles):
            p = jnp.exp2(lg - m)
            l = l + jnp.sum(p, axis=-1, keepdims=True)
            acc = acc + _dot_nt(p.astype(BF16), v_t)
        return acc / l

    per = page // SEL_BLOCK
    tiles = []
    for pg in range(n_pages):
        blkp = slc_pages[pg]
        dist = past - (pg * page + lane)
        selx = jnp.zeros((nh, LANES), F32)
        for u in range(per):
            in_u = jnp.where(lane >= u * SEL_BLOCK, jnp.where(lane < (u + 1) * SEL_BLOCK, 1.0, 0.0), 0.0)
            selx = selx + in_u * sel_h[:, pg * per + u:pg * per + u + 1]
        neg = jnp.where(dist >= 0, (selx - 1.0) * 1e30, NEG) - slope * dist.astype(F32)
        tiles.append((blkp[0, 0, 0].reshape(LANES, page).astype(BF16),
                      blkp[0, 0, 1].reshape(LANES, page).astype(BF16), neg))
    o_slc = attend(tiles, kvs_ref[0], (sel_h[:, cur:cur + 1] - 1.0) * 1e30)

    wlane = lax.broadcasted_iota(jnp.int32, (1, wlen), 1)
    w_pos = past - wlen + wlane
    dist = past - w_pos
    ok = jnp.where(dist >= 0, jnp.where(dist < WINDOW, jnp.where(w_pos >= 0, 1.0, 0.0), 0.0), 0.0)
    neg = (ok - 1.0) * 1e30 - slope * dist.astype(F32)
    tiles = [(cw_ref[0, 0, 0].reshape(LANES, wlen).astype(BF16), cw_ref[0, 0, 1].reshape(LANES, wlen).astype(BF16), neg)]
    o_win = attend(tiles, kvw_ref[0], jnp.zeros((nh, 1), F32))

    gates = gate_ref[0]
    pick = lambda a: jnp.where(hrow < NSA_GROUP, a[:, :NSA_HD], a[:, NSA_HD:])
    o_ref[...] = gates[:, 0:1] * pick(o_cmp) + gates[:, 1:2] * pick(o_slc) + gates[:, 2:3] * pick(o_win)


def _nsa_sample(layer, page_table, q_all, gates, kvc, kvs, kvw, cache_c, cache_s, cache_w, cw):
    nb, n_pages = page_table.shape
    depth, n_phys, page = cache_c.shape[:3]
    past = n_pages * page
    wlen = cache_w.shape[2]
    ncb = past // CMP_STRIDE
    cc = cache_c.transpose(0, 1, 3, 4, 5, 2)
    cs = cache_s.transpose(0, 1, 3, 4, 5, 2)
    cwin = cache_w.transpose(0, 1, 3, 4, 5, 2)
    nseq = NSA_SAMPLE_SEQS if nb % NSA_SAMPLE_SEQS == 0 else 1
    full = lambda a: pl.BlockSpec(a.shape, lambda b, pt: (0,) * a.ndim)
    ovt = _overlap_matrix(ncb, LANES).T
    wargs = (cw["w1s"], cw["pes"], cw["b1s"], cw["w2s"], ovt)
    in_specs, args = [], []
    for s in range(nseq):
        seq = lambda b, s=s: nseq * b + s
        heads = pl.BlockSpec((1, NSA_HEADS, LANES), lambda b, pt, seq=seq: (seq(b), 0, 0))
        row = pl.BlockSpec((1, 1, KV_COLS), lambda b, pt, seq=seq: (seq(b), 0, 0))
        pages = [pl.BlockSpec((1, 1, 2, NSA_KV_HEADS, NSA_HD, page),
                              lambda b, pt, seq=seq, j=j: (layer, pt[seq(b), j], 0, 0, 0, 0)) for j in range(n_pages)]
        win = pl.BlockSpec((1, 1, 2, NSA_KV_HEADS, NSA_HD, wlen), lambda b, pt, seq=seq: (layer, seq(b), 0, 0, 0, 0))
        in_specs += [heads, heads, row, row, row] + pages + pages + [win]
        args += [q_all, gates, kvc, kvs, kvw] + [cc] * n_pages + [cs] * n_pages + [cwin]
    grid_spec = pltpu.PrefetchScalarGridSpec(
        num_scalar_prefetch=1,
        grid=(nb // nseq,),
        in_specs=in_specs + [full(a) for a in wargs],
        out_specs=pl.BlockSpec((nseq, NSA_HEADS, NSA_HD), lambda b, pt: (b, 0, 0)),
        scratch_shapes=[pltpu.VMEM((nseq, 2, CMP_STRIDE, ncb + 8, LANES), F32), pltpu.VMEM((2, 8, 2 * LANES), F32)],
    )
    return pl.pallas_call(
        functools.partial(_nsa_sample_kernel, n_pages=n_pages, page=page, past=past, wlen=wlen, nseq=nseq),
        grid_spec=grid_spec,
        out_shape=jax.ShapeDtypeStruct((nb, NSA_HEADS, NSA_HD), F32),
        compiler_params=_cparams(("arbitrary",)),
        name="nsa_sample",
    )(page_table, *args, *wargs)


def _gla_sample_kernel(col_ref, v_ref, s_ref, o_ref, sn_ref, *, bb):
    for b in range(bb):
        cols = col_ref[b]
        for h in range(GLA_HEADS):
            a = jnp.exp(cols[:, h:h + 1])
            k = cols[:, GLA_HEADS + h:GLA_HEADS + h + 1]
            q = cols[:, 2 * GLA_HEADS + h:2 * GLA_HEADS + h + 1]
            v = v_ref[b, h:h + 1, :]
            s0 = s_ref[b, h]
            qk = jnp.sum(q * k, axis=0, keepdims=True)
            o_ref[b, h:h + 1, :] = jnp.sum((q * a) * s0, axis=0, keepdims=True) + qk * v
            sn_ref[b, h] = a * s0 + k * v


def _gla_sample(cols, v, state, bb):
    nb = v.shape[0]
    return pl.pallas_call(
        functools.partial(_gla_sample_kernel, bb=bb),
        grid=(nb // bb,),
        in_specs=[pl.BlockSpec((bb, GLA_DK, 16), lambda i: (i, 0, 0)),
                  pl.BlockSpec((bb, GLA_HEADS, GLA_DV), lambda i: (i, 0, 0)),
                  pl.BlockSpec((bb, GLA_HEADS, GLA_DK, GLA_DV), lambda i: (i, 0, 0, 0))],
        out_specs=[pl.BlockSpec((bb, GLA_HEADS, GLA_DV), lambda i: (i, 0, 0)),
                   pl.BlockSpec((bb, GLA_HEADS, GLA_DK, GLA_DV), lambda i: (i, 0, 0, 0))],
        out_shape=[jax.ShapeDtypeStruct((nb, GLA_HEADS, GLA_DV), F32),
                   jax.ShapeDtypeStruct((nb, GLA_HEADS, GLA_DK, GLA_DV), F32)],
        compiler_params=_cparams(("parallel",)),
        name="gla_sample",
    )(cols, v, state)


def _proj_weight(w):
    d = w.shape[0]
    offs = np.cumsum((0,) + IN_SIZES)
    seg = lambda i: w[:, offs[i]:offs[i + 1]]

    def slots(a, n):
        per = a.shape[1] // n
        return jnp.pad(a.reshape(d, n, per), ((0, 0), (0, 0), (0, LANES - per))).reshape(d, n * LANES)

    return jnp.concatenate([slots(seg(0), NSA_HEADS), seg(1), seg(2), seg(3), slots(seg(4), NSA_KV_HEADS),
                            seg(5), seg(6), seg(7), slots(seg(8), 1), seg(9)], axis=1)


def _cmp_weights(pe, w1, b1, w2):
    g = NSA_KV_HEADS
    eye = jnp.eye(2, dtype=F32)
    eg = jnp.eye(g, dtype=F32)
    w1r = w1.reshape(2, 2, CMP_STRIDE, NSA_HD, CMP_HIDDEN)
    big = jnp.einsum("krsdh,ka,gb->rskgdabh", w1r, eye, eg).reshape(2, CMP_STRIDE * KV_COLS, 2 * g * CMP_HIDDEN)
    w2big = jnp.einsum("khd,ka,gb->kghabd", w2, eye, eg).reshape(2 * g * CMP_HIDDEN, KV_COLS)
    per = pe.reshape(2, 2, CMP_STRIDE, NSA_HD)
    pet = jnp.broadcast_to(per.transpose(1, 2, 0, 3)[:, :, :, None, :], (2, CMP_STRIDE, 2, g, NSA_HD))
    pet = pet.reshape(2, 1, CMP_STRIDE * KV_COLS)
    b1big = jnp.broadcast_to(b1[:, None, :], (2, g, CMP_HIDDEN)).reshape(1, 2 * g * CMP_HIDDEN)
    w1s = jnp.einsum("ksdh,gb->ksgdbh", w1.reshape(2, CMP_BLOCK, NSA_HD, CMP_HIDDEN), eg)
    w1s = w1s.reshape(2, CMP_BLOCK // 2, 2 * g * NSA_HD, g * CMP_HIDDEN)
    w2s = jnp.einsum("khd,gb->kghbd", w2, eg).reshape(2, g * CMP_HIDDEN, g * NSA_HD)
    pes = jnp.broadcast_to(pe[:, :, None, :], (2, CMP_BLOCK, g, NSA_HD)).reshape(2, CMP_BLOCK // 2, 2 * g * NSA_HD)
    b1s = jnp.broadcast_to(b1[:, None, None, :], (2, 1, g, CMP_HIDDEN)).reshape(2, 1, g * CMP_HIDDEN)
    return {
        "w1s": w1s.astype(BF16), "w2s": w2s.astype(BF16), "pes": pes, "b1s": b1s,
        "w1lo_t": big[0].T.astype(BF16), "w1hi_t": big[1].T.astype(BF16),
        "pe_lo": pet[0], "pe_hi": pet[1], "b1_col": b1big.T, "w2_t": w2big.T.astype(BF16),
    }


def _mods(mod, rows):
    m = mod.reshape(mod.shape[0], N_MOD, D_MODEL)
    if rows == 1:
        return [m[:, j][:, None, :] for j in range(N_MOD)]
    return [m[:, j][None, :, :] for j in range(N_MOD)]


def _layer_prompt(x, mods, lw, tm, tq, tk, final_gain):
    b, t, _ = x.shape
    x = _ffn(x, mods[0], mods[1], mods[2], lw["norm_g"][0:1], lw["f1_in"], lw["f1_out"], tm)
    (qn, kvc, _, _, kvct, kvst, kvwt, kvtb, gn, qg, kg, vg, la, og) = _proj(
        x, mods[3], mods[4], lw["norm_g"][1:2], lw["w_all"], lw["wkvt"], lw["wa2p"], lw["ba"], lw["qb"], tm)
    g = NSA_KV_HEADS
    ncp = t // CMP_STRIDE
    ckvt = _cmp_prompt(kvc.reshape(b, ncp, CMP_STRIDE * KV_COLS), lw["cmp"]).astype(BF16)
    ckvt = ckvt.reshape(b, 2, g, NSA_HD, ncp)
    kvs5 = kvtb[:, :KV_COLS].reshape(b, 2, g, NSA_HD, t)
    kvw5 = kvtb[:, KV_COLS:].reshape(b, 2, g, NSA_HD, t)
    pos_rows = _alibi_k_rows(np.arange(t))
    ones_row = jnp.ones((1, t), BF16)
    onehot = jnp.asarray(np.arange(max(LANES, t // SEL_BLOCK))[:, None] == np.arange(t)[None, :] // SEL_BLOCK, BF16)
    lead = (b, g)
    ck = _aug_rows([ckvt[:, 0], _alibi_k_rows(np.arange(ncp) * CMP_STRIDE + CMP_BLOCK - 1)], LANES, lead)
    ks = _aug_rows([onehot, kvs5[:, 0], pos_rows], onehot.shape[0] + LANES, lead)
    vs = _aug_rows([kvs5[:, 1], ones_row], LANES, lead)
    kw = _aug_rows([kvw5[:, 0], pos_rows], LANES, lead)
    vw = _aug_rows([kvw5[:, 1], ones_row], LANES, lead)
    o_nsa = _nsa_prompt(qn, gn, ck, ckvt[:, 1], ks, vs, kw, vw, tq, tk)
    o_nsa = o_nsa.reshape(b, NSA_HEADS, t, NSA_HD)
    o_gla, s_fin = _gla_prompt(qg, kg, la, vg, min(8, t // GLA_CHUNK))
    x = _out_proj(x, mods[5], o_nsa, o_gla, og, lw["gla_norm"], lw["w_out"], tm)
    x = _ffn(x, mods[6], mods[7], mods[8], lw["norm_g"][2:3], lw["f2_in"], lw["f2_out"], tm, final_gain)
    win = min(WINDOW, t)
    back = lambda a: a.reshape(b, 2, NSA_KV_HEADS, NSA_HD, a.shape[-1]).transpose(0, 4, 1, 2, 3)
    return x, (back(kvct), back(kvst), back(kvwt[:, :, t - win:]), s_fin)


def _layer_sample(layer, x, mods, lw, cache_c, cache_s, cache_w, state, page_table, final_gain):
    nb = x.shape[1]
    x = _ffn(x, mods[0], mods[1], mods[2], lw["norm_g"][0:1], lw["f1_in"], lw["f1_out"], nb)
    (qn, kvc, kvs, kvw, kvct, kvst, kvwt, _, gn, qg, kg, vg, la, og) = _proj(
        x, mods[3], mods[4], lw["norm_g"][1:2], lw["w_all"], lw["wkvt"], lw["wa2p"], lw["ba"], lw["qb"], nb)
    q4 = qn[0].reshape(nb, NSA_KV_HEADS, NSA_GROUP, LANES)[..., :NSA_HD]
    q_all = (q4[:, :, :, None, :] * jnp.eye(NSA_KV_HEADS, dtype=BF16)[None, :, None, :, None]).reshape(nb, NSA_HEADS, LANES)
    gates = gn[0].reshape(nb, NSA_KV_HEADS, LANES)[..., :3 * NSA_GROUP].reshape(nb, NSA_HEADS, 3)
    gates = jnp.pad(gates, ((0, 0), (0, 0), (0, LANES - 3)))
    r3 = lambda a: a[0][:, None, :]
    o_nsa = _nsa_sample(layer, page_table, q_all, gates, r3(kvc), r3(kvs), r3(kvw),
                        cache_c, cache_s, cache_w, lw["cmp"])
    o_nsa = o_nsa.transpose(1, 0, 2)[None]
    col = lambda a: a[0].reshape(nb, GLA_HEADS, GLA_DK).transpose(0, 2, 1)
    cols = jnp.concatenate([col(la), col(kg), col(qg), jnp.zeros((nb, GLA_DK, 16 - 3 * GLA_HEADS), F32)], axis=-1)
    o_gla, s_new = _gla_sample(cols, vg[0].reshape(nb, GLA_HEADS, GLA_DV), state, 8)
    o_gla = o_gla.reshape(1, nb, GLA_WIDTH)
    x = _out_proj(x, mods[5], o_nsa, o_gla, og, lw["gla_norm"], lw["w_out"], nb)
    x = _ffn(x, mods[6], mods[7], mods[8], lw["norm_g"][2:3], lw["f2_in"], lw["f2_out"], nb, final_gain)
    kv5 = lambda a: a[0].reshape(2, NSA_KV_HEADS, NSA_HD, nb).transpose(3, 0, 1, 2)[:, None]
    return x, (kv5(kvct), kv5(kvst), kv5(kvwt), s_new)


def kernel(x_prompt, x_sample, c_prompt, c_sample, cache_kv_cmp, cache_kv_slc, cache_kv_win, state_gla, page_table, w_ada, b_ada, norm_g, ffn1_w_in, ffn1_w_out, w_in, cmp_pe, cmp_w1, cmp_b1, cmp_w2, gla_wa2, gla_ba, gla_norm, w_out, ffn2_w_in, ffn2_w_out, final_norm):
    depth = w_ada.shape[0]
    b, t, d = x_prompt.shape
    nb = x_sample.shape[0]
    bp = -(-b // 8) * 8
    c_all = jnp.concatenate([c_prompt, jnp.zeros((bp - b, d), F32), c_sample], axis=0)
    mod_all = _ada(c_all, w_ada, b_ada)

    tm = min(512, t)
    tq = min(256, t)
    tk = min(512, t)
    xp = x_prompt
    xs = x_sample.reshape(1, nb, d)
    outs_p, outs_s = [], []
    for l in range(depth):
        lw = {
            "norm_g": norm_g[l],
            "f1_in": ffn1_w_in[l].astype(BF16), "f1_out": ffn1_w_out[l].astype(BF16),
            "f2_in": ffn2_w_in[l].astype(BF16), "f2_out": ffn2_w_out[l].astype(BF16),
            "w_all": _proj_weight(w_in[l]).astype(BF16),
            "qb": _q_bias(),
            "wkvt": w_in[l][:, NSA_WIDTH:NSA_WIDTH + 3 * KV_COLS].T.astype(BF16),
            "wa2p": jnp.pad(gla_wa2[l], ((0, 128 - GLA_RANK), (0, 0))).astype(BF16),
            "ba": gla_ba[l][None, :],
            "gla_norm": gla_norm[l][None, :],
            "w_out": w_out[l].astype(BF16),
            "cmp": _cmp_weights(cmp_pe[l], cmp_w1[l], cmp_b1[l], cmp_w2[l]),
        }
        fg = final_norm[None, :] if l == depth - 1 else None
        xp, st_p = _layer_prompt(xp, _mods(mod_all[l, :b], 1), lw, tm, tq, tk, fg)
        xs, st_s = _layer_sample(l, xs, _mods(mod_all[l, bp:], nb), lw, cache_kv_cmp, cache_kv_slc,
                                 cache_kv_win, state_gla[l], page_table, fg)
        outs_p.append(st_p)
        outs_s.append(st_s)
    stack = lambda outs, j: jnp.stack([o[j] for o in outs])
    return (xp, xs.reshape(nb, 1, d),
            stack(outs_p, 0), stack(outs_p, 1), stack(outs_p, 2), stack(outs_p, 3),
            stack(outs_s, 0), stack(outs_s, 1), stack(outs_s, 2), stack(outs_s, 3))
```

```python
import functools

import numpy as np
import jax
import jax.numpy as jnp
from jax import lax
from jax.experimental import pallas as pl
from jax.experimental.pallas import tpu as pltpu

F32 = jnp.float32
BF16 = jnp.bfloat16

D_MODEL = 1024
NSA_HD = 64
NSA_HEADS = 8
NSA_KV_HEADS = 2
NSA_GROUP = 4
NSA_WIDTH = NSA_HEADS * NSA_HD
CMP_BLOCK = 32
CMP_STRIDE = 16
CMP_HIDDEN = 2 * NSA_HD
SEL_BLOCK = 64
SEL_TOPK = 16
WINDOW = 512
FORCE_BONUS = 1000.0
GLA_HEADS = 4
GLA_DV = 128
GLA_DK = 64
GLA_WIDTH = GLA_HEADS * GLA_DV
GLA_RANK = 16
GLA_TAU = 16.0
GLA_CHUNK = 64
GLA_SUB = 16
FFN_DIM = 2816
N_MOD = 9
EPS = 1e-6
KV_COLS = 2 * NSA_KV_HEADS * NSA_HD
IN_SIZES = (NSA_WIDTH, KV_COLS, KV_COLS, KV_COLS, 3 * NSA_HEADS,
            GLA_HEADS * GLA_DK, GLA_HEADS * GLA_DK, GLA_WIDTH, GLA_RANK, GLA_WIDTH)
IN_PAD = (NSA_HEADS * 128, KV_COLS, KV_COLS, KV_COLS, NSA_KV_HEADS * 128,
          GLA_HEADS * GLA_DK, GLA_HEADS * GLA_DK, GLA_WIDTH, 128, GLA_WIDTH)
IN_OFF = tuple(int(v) for v in np.cumsum((0,) + IN_PAD))

LANES = 128
NEG = -1e30
LOG2E = 1.4426950408889634
VMEM_LIMIT = 56 * 1024 * 1024
FFN_TF = FFN_DIM // 2
NSA_SAMPLE_SEQS = 2
GLA_UNROLL = 4


def _cparams(sem):
    return pltpu.CompilerParams(dimension_semantics=sem, vmem_limit_bytes=VMEM_LIMIT)


def _dot(a, b):
    return jnp.dot(a, b, preferred_element_type=F32)


def _dot_nt(a, b):
    return lax.dot_general(a, b, (((1,), (1,)), ((), ())), preferred_element_type=F32)


def _split3(x):
    x1 = x.astype(BF16)
    r = x - x1.astype(F32)
    x2 = r.astype(BF16)
    x3 = (r - x2.astype(F32)).astype(BF16)
    return x1, x2, x3


def _rms_mod(x, g, sc, sh):
    y = x * lax.rsqrt(jnp.mean(x * x, axis=-1, keepdims=True) + EPS) * g
    return y * (1.0 + sc) + sh


def _masked_softmax(lg, mask):
    lg = jnp.where(mask, lg, NEG)
    m = jnp.max(lg, axis=-1, keepdims=True)
    e = jnp.where(mask, jnp.exp(lg - m), 0.0)
    return e / jnp.maximum(jnp.sum(e, axis=-1, keepdims=True), 1e-30)


def _top_k_mask(score, blk, k, axis=-1):
    sel = jnp.zeros_like(score)
    big = float(score.shape[axis])
    for _ in range(k):
        m = jnp.max(score, axis=axis, keepdims=True)
        idx = jnp.min(jnp.where(score == m, blk, big), axis=axis, keepdims=True)
        hit = blk == idx
        sel = jnp.where(hit, 1.0, sel)
        score = jnp.where(hit, -jnp.inf, score)
    return sel


def _ada_kernel(c_ref, w_ref, b_ref, o_ref):
    c = c_ref[...]
    s = c * jax.nn.sigmoid(c)
    o_ref[0] = _dot(s.astype(BF16), w_ref[0].astype(BF16)) + b_ref[0]


def _ada(c_all, w_ada, b_ada):
    depth, d, n = w_ada.shape
    m = c_all.shape[0]
    tn = 1024
    return pl.pallas_call(
        _ada_kernel,
        grid=(depth, n // tn),
        in_specs=[pl.BlockSpec((m, d), lambda l, j: (0, 0)),
                  pl.BlockSpec((1, d, tn), lambda l, j: (l, 0, j)),
                  pl.BlockSpec((1, 1, tn), lambda l, j: (l, 0, j))],
        out_specs=pl.BlockSpec((1, m, tn), lambda l, j: (l, 0, j)),
        out_shape=jax.ShapeDtypeStruct((depth, m, n), F32),
        compiler_params=_cparams(("parallel", "parallel")),
        name="ada",
    )(c_all, w_ada, b_ada.reshape(depth, 1, n))


def _mod_spec(mod, tm, nargs):
    r = mod.shape[1]
    d = mod.shape[2]
    if nargs == 3:
        if r == 1:
            return pl.BlockSpec((1, 1, d), lambda b, i, f: (b, 0, 0))
        return pl.BlockSpec((1, tm, d), lambda b, i, f: (b, i, 0))
    if r == 1:
        return pl.BlockSpec((1, 1, d), lambda b, i: (b, 0, 0))
    return pl.BlockSpec((1, tm, d), lambda b, i: (b, i, 0))


def _ffn_kernel(x_ref, sh_ref, sc_ref, gt_ref, g_ref, wi_ref, wo_ref, fg_ref, o_ref, *, final):
    x = x_ref[0]
    h = _rms_mod(x, g_ref[...], sc_ref[0], sh_ref[0]).astype(BF16)
    y = jnp.zeros(x.shape, F32)
    for f in range(FFN_DIM // FFN_TF):
        g = _dot(h, wi_ref[:, f * FFN_TF:(f + 1) * FFN_TF])
        u = _dot(h, wi_ref[:, FFN_DIM + f * FFN_TF:FFN_DIM + (f + 1) * FFN_TF])
        a = (g * jax.nn.sigmoid(g) * u).astype(BF16)
        y = y + _dot(a, wo_ref[f * FFN_TF:(f + 1) * FFN_TF, :])
    x = x + 0.5 * gt_ref[0] * y
    if final:
        x = x * lax.rsqrt(jnp.mean(x * x, axis=-1, keepdims=True) + EPS) * fg_ref[...]
    o_ref[0] = x


def _ffn(x, sh, sc, gt, g, w_in, w_out, tm, final_gain=None):
    bx, tx, d = x.shape
    xs = pl.BlockSpec((1, tm, d), lambda b, i: (b, i, 0))
    ms = _mod_spec(sh, tm, 2)
    once = lambda a: pl.BlockSpec(a.shape, lambda b, i: (0,) * a.ndim, pipeline_mode=pl.Buffered(1))
    fg = g if final_gain is None else final_gain
    return pl.pallas_call(
        functools.partial(_ffn_kernel, final=final_gain is not None),
        grid=(bx, tx // tm),
        in_specs=[xs, ms, ms, ms, once(g), once(w_in), once(w_out), once(fg)],
        out_specs=xs,
        out_shape=jax.ShapeDtypeStruct(x.shape, F32),
        compiler_params=_cparams(("parallel", "parallel")),
        name="ffn",
    )(x, sh, sc, gt, g, w_in, w_out, fg)


def _proj_kernel(x_ref, sh_ref, sc_ref, g_ref, w_ref, wkvt_ref, wa2_ref, ba_ref, qb_ref,
                 qn_ref, kvc_ref, kvs_ref, kvw_ref, kvct_ref, kvst_ref, kvwt_ref, kvtb_ref,
                 gn_ref, qg_ref, kg_ref, vg_ref, la_ref, og_ref):
    h = _rms_mod(x_ref[0], g_ref[...], sc_ref[0], sh_ref[0]).astype(BF16)
    p = _dot(h, w_ref[...])
    o = IN_OFF
    qn_ref[0] = (p[:, o[0]:o[1]] * (NSA_HD ** -0.5 * LOG2E) + qb_ref[...]).astype(BF16)
    kvc_ref[0] = p[:, o[1]:o[2]]
    kvs_ref[0] = p[:, o[2]:o[3]]
    kvw_ref[0] = p[:, o[3]:o[4]]
    kvt = _dot_nt(wkvt_ref[...], h)
    kvct_ref[0] = kvt[0:KV_COLS]
    kvst_ref[0] = kvt[KV_COLS:2 * KV_COLS]
    kvwt_ref[0] = kvt[2 * KV_COLS:3 * KV_COLS]
    kvtb_ref[0] = kvt[KV_COLS:3 * KV_COLS].astype(BF16)
    gn_ref[0] = jax.nn.sigmoid(p[:, o[4]:o[5]])
    qg_ref[0] = p[:, o[5]:o[6]] * (GLA_DK ** -0.5)
    kg_ref[0] = p[:, o[6]:o[7]]
    vg_ref[0] = p[:, o[7]:o[8]]
    a_pre = _dot(p[:, o[8]:o[9]].astype(BF16), wa2_ref[...]) + ba_ref[...]
    log_sig = jnp.minimum(a_pre, 0.0) - jnp.log(1.0 + jnp.exp(-jnp.abs(a_pre)))
    la_ref[0] = log_sig * (1.0 / GLA_TAU)
    og_ref[0] = p[:, o[9]:o[10]]


def _proj(x, sh, sc, g, w_all, wkvt, wa2p, ba, qb, tm):
    bx, tx, d = x.shape
    xs = pl.BlockSpec((1, tm, d), lambda b, i: (b, i, 0))
    ms = _mod_spec(sh, tm, 2)
    full = lambda shape: pl.BlockSpec(shape, lambda b, i: (0,) * len(shape))
    rows = lambda w, dt: (pl.BlockSpec((1, tm, w), lambda b, i: (b, i, 0)), jax.ShapeDtypeStruct((bx, tx, w), dt))
    cols = lambda w, dt: (pl.BlockSpec((1, w, tm), lambda b, i: (b, 0, i)), jax.ShapeDtypeStruct((bx, w, tx), dt))
    outs = [rows(IN_PAD[0], BF16), rows(KV_COLS, F32), rows(KV_COLS, F32), rows(KV_COLS, F32),
            cols(KV_COLS, F32), cols(KV_COLS, F32), cols(KV_COLS, F32), cols(2 * KV_COLS, BF16),
            rows(IN_PAD[4], F32), rows(256, F32), rows(256, F32), rows(GLA_WIDTH, F32), rows(256, F32),
            rows(GLA_WIDTH, F32)]
    return pl.pallas_call(
        _proj_kernel,
        grid=(bx, tx // tm),
        in_specs=[xs, ms, ms, full((1, d)), full(w_all.shape), full(wkvt.shape), full(wa2p.shape),
                  full(ba.shape), full(qb.shape)],
        out_specs=[o[0] for o in outs],
        out_shape=[o[1] for o in outs],
        compiler_params=_cparams(("parallel", "parallel")),
        name="proj",
    )(x, sh, sc, g, w_all, wkvt, wa2p, ba, qb)


def _cmp_kernel(x_ref, pelo_ref, pehi_ref, w1lo_ref, w1hi_ref, b1_ref, w2_ref, o_ref):
    x = x_ref[0]
    n = x.shape[0]
    h_lo = _dot_nt(w1lo_ref[...], (x + pelo_ref[...]).astype(BF16))
    h_hi = _dot_nt(w1hi_ref[...], (x + pehi_ref[...]).astype(BF16))
    h = h_lo + pltpu.roll(h_hi, n - 1, 1) + b1_ref[...]
    o_ref[0] = _dot(w2_ref[...], jax.nn.gelu(h).astype(BF16))


def _cmp_prompt(x16, cw):
    b, n, w = x16.shape
    full = lambda a: pl.BlockSpec(a.shape, lambda i: (0,) * a.ndim)
    args = (cw["pe_lo"], cw["pe_hi"], cw["w1lo_t"], cw["w1hi_t"], cw["b1_col"], cw["w2_t"])
    return pl.pallas_call(
        _cmp_kernel,
        grid=(b,),
        in_specs=[pl.BlockSpec((1, n, w), lambda i: (i, 0, 0))] + [full(a) for a in args],
        out_specs=pl.BlockSpec((1, KV_COLS, n), lambda i: (i, 0, 0)),
        out_shape=jax.ShapeDtypeStruct((b, KV_COLS, n), F32),
        compiler_params=_cparams(("parallel",)),
        name="cmp_prompt",
    )(x16, *args)


def _nsa_prompt_kernel(q_ref, gate_ref, ck_ref, cv_ref, ks_ref, vs_ref, kw_ref, vw_ref, ovt_ref,
                       o_ref, ids_scr, *, tq, tk, ncp, nsp):
    i = pl.program_id(2)
    q0 = i * tq
    nr = NSA_GROUP * tq
    row = q0 + jnp.bitwise_and(lax.broadcasted_iota(jnp.int32, (nr, 1), 0), tq - 1)
    row_t = q0 + lax.broadcasted_iota(jnp.int32, (tq, 1), 0)
    qblk = q_ref[0]
    q = jnp.concatenate([qblk[:, r * LANES:(r + 1) * LANES] for r in range(NSA_GROUP)], axis=0)

    def per_head(x, mask):
        return (x.reshape(NSA_GROUP, tq, x.shape[-1]) + mask[None]).reshape(x.shape)

    c_end = lax.broadcasted_iota(jnp.int32, (1, ncp), 1) * CMP_STRIDE + (CMP_BLOCK - 1)
    lg = per_head(_dot(q, ck_ref[0, 0]), jnp.where(c_end <= row_t, 0.0, NEG))
    e = jnp.exp2(lg - jnp.max(lg, axis=-1, keepdims=True))
    pc = e * jnp.where(row >= CMP_BLOCK - 1, 1.0 / jnp.sum(e, axis=-1, keepdims=True), 0.0)
    o_cmp = _dot_nt(pc.astype(BF16), cv_ref[0, 0])
    psum = pc[0:tq]
    for r in range(1, NSA_GROUP):
        psum = psum + pc[r * tq:(r + 1) * tq]
    p_hi = psum.astype(BF16)
    p_lo = (psum - p_hi.astype(F32)).astype(BF16)
    imp_t = _dot_nt(ovt_ref[...], p_hi) + _dot_nt(ovt_ref[...], p_lo)

    blk = lax.broadcasted_iota(jnp.int32, (nsp, 1), 0)
    t_lane = q0 + lax.broadcasted_iota(jnp.int32, (1, tq), 1)
    cur = jnp.right_shift(t_lane, 6)
    forced = jnp.where(blk == 0, 1.0, 0.0) + jnp.where(blk == cur, 1.0, 0.0) + jnp.where(blk == cur - 1, 1.0, 0.0)
    valid = blk * SEL_BLOCK <= t_lane
    score = jnp.where(valid, imp_t + jnp.where(forced > 0.5, FORCE_BONUS, 0.0), NEG)
    sel_t = _top_k_mask(score, blk.astype(F32), SEL_TOPK, axis=0)
    sel_t = jnp.where(valid, sel_t, 0.0)
    sel_bias = ((sel_t - 1.0) * 1e30).T.astype(BF16)
    qa = jnp.concatenate([jnp.concatenate([sel_bias] * NSA_GROUP, axis=0), q], axis=1)

    n_full = q0 // tk
    per_tile = tk // SEL_BLOCK
    n_tiles = nsp // per_tile
    blk_any = jnp.max(sel_t, axis=1, keepdims=True)
    tile_any = jnp.max(blk_any.reshape(n_tiles, per_tile, 1), axis=1)
    bit = jnp.left_shift(1, lax.broadcasted_iota(jnp.int32, (n_tiles, 1), 0))
    tile_bits = jnp.sum(jnp.where(tile_any > 0.0, bit, 0))
    n_used = jnp.int32(0)
    for kt in range(min(n_tiles, ks_ref.shape[3] // tk)):
        use = jnp.logical_and(jnp.bitwise_and(jnp.right_shift(tile_bits, kt), 1) == 1, kt < n_full)
        ids_scr[n_used] = kt
        n_used = n_used + use.astype(jnp.int32)

    def sel_tile(kt, carry, diag):
        m, acc = carry
        k0 = pl.multiple_of(kt * tk, tk)
        lg = _dot(qa, ks_ref[0, 0, :, pl.ds(k0, tk)])
        if diag:
            lg = jnp.where(k0 + lax.broadcasted_iota(jnp.int32, (1, tk), 1) <= row, lg, NEG)
        m_new = jnp.maximum(m, jnp.max(lg, axis=-1, keepdims=True))
        p = jnp.exp2(lg - m_new).astype(BF16)
        return m_new, jnp.exp2(m - m_new) * acc + _dot_nt(p, vs_ref[0, 0, :, pl.ds(k0, tk)])

    carry = sel_tile(n_full, (jnp.full((nr, 1), NEG, F32), jnp.zeros((nr, LANES), F32)), True)

    def two_tiles(j, c):
        return sel_tile(ids_scr[2 * j + 1], sel_tile(ids_scr[2 * j], c, False), False)

    carry = lax.fori_loop(0, n_used // 2, two_tiles, carry)
    _, acc = lax.cond(n_used % 2 == 1, lambda c: sel_tile(ids_scr[n_used - 1], c, False), lambda c: c, carry)
    o_slc = acc[:, :NSA_HD] / acc[:, NSA_HD:NSA_HD + 1]

    wl = WINDOW + tq
    w0 = pl.multiple_of(jnp.maximum(q0 - WINDOW, 0), tq)
    dist_w = row_t - (w0 + lax.broadcasted_iota(jnp.int32, (1, wl), 1))
    neg_w = jnp.where(dist_w >= 0, jnp.where(dist_w < WINDOW, 0.0, NEG), NEG)
    lg = per_head(_dot(q, kw_ref[0, 0, :, pl.ds(w0, wl)]), neg_w)
    p = jnp.exp2(lg - jnp.max(lg, axis=-1, keepdims=True)).astype(BF16)
    ow = _dot_nt(p, vw_ref[0, 0, :, pl.ds(w0, wl)])
    o_win = ow[:, :NSA_HD] / ow[:, NSA_HD:NSA_HD + 1]

    gates = gate_ref[0]
    gcol = lambda j: jnp.concatenate([gates[:, 3 * r + j:3 * r + j + 1] for r in range(NSA_GROUP)], axis=0)
    o = gcol(0) * o_cmp + gcol(1) * o_slc + gcol(2) * o_win
    o_ref[0] = jnp.concatenate([o[r * tq:(r + 1) * tq] for r in range(NSA_GROUP)], axis=1)


def _overlap_matrix(ncp, nsp):
    c = np.arange(ncp)[:, None] * CMP_STRIDE
    s = np.arange(nsp)[None, :] * SEL_BLOCK
    return jnp.asarray((c <= s + SEL_BLOCK - 1) & (c + CMP_BLOCK - 1 >= s), dtype=BF16)


def _bf16_round(x):
    return np.asarray(x, dtype=BF16).astype(np.float32)


def _q_bias():
    slopes = np.exp2(-8.0 * np.arange(1, NSA_HEADS + 1) / NSA_HEADS)
    c_hi = float(_bf16_round(LOG2E))
    c_lo = float(_bf16_round(LOG2E - c_hi))
    qb = np.zeros((NSA_HEADS, LANES), np.float32)
    qb[:, NSA_HD:NSA_HD + 4] = np.stack([c_hi * slopes * SEL_BLOCK, c_lo * slopes * SEL_BLOCK,
                                         c_hi * slopes, c_lo * slopes], axis=1)
    return jnp.asarray(qb.reshape(1, NSA_HEADS * LANES))


def _alibi_k_rows(pos):
    pos = np.asarray(pos)
    return jnp.asarray(np.stack([pos // SEL_BLOCK, pos // SEL_BLOCK, pos % SEL_BLOCK, pos % SEL_BLOCK]), BF16)


def _aug_rows(parts, total, lead):
    parts = [jnp.broadcast_to(p, lead + p.shape[-2:]) for p in parts]
    used = sum(p.shape[-2] for p in parts)
    if total > used:
        parts.append(jnp.zeros(lead + (total - used, parts[0].shape[-1]), BF16))
    return jnp.concatenate(parts, axis=-2)


def _nsa_prompt(q_pad, gates, ck, cv, ks, vs, kw, vw, tq, tk):
    b, t, _ = q_pad.shape
    g, r = NSA_KV_HEADS, NSA_GROUP
    ncp = ck.shape[3]
    nsp = max(LANES, t // SEL_BLOCK)
    kv = lambda a: pl.BlockSpec((1, 1) + a.shape[2:], lambda bi, gi, i: (bi, gi, 0, 0))
    ovt = _overlap_matrix(ncp, nsp).T
    return pl.pallas_call(
        functools.partial(_nsa_prompt_kernel, tq=tq, tk=tk, ncp=ncp, nsp=nsp),
        grid=(b, g, t // tq),
        in_specs=[pl.BlockSpec((1, tq, r * LANES), lambda bi, gi, i: (bi, i, gi)),
                  pl.BlockSpec((1, tq, LANES), lambda bi, gi, i: (bi, i, gi)),
                  kv(ck), kv(cv), kv(ks), kv(vs), kv(kw), kv(vw),
                  pl.BlockSpec((nsp, ncp), lambda bi, gi, i: (0, 0))],
        out_specs=pl.BlockSpec((1, tq, r * NSA_HD), lambda bi, gi, i: (bi, i, gi)),
        out_shape=jax.ShapeDtypeStruct((b, t, NSA_WIDTH), F32),
        scratch_shapes=[pltpu.SMEM((nsp // (tk // SEL_BLOCK),), jnp.int32)],
        compiler_params=_cparams(("parallel", "parallel", "arbitrary")),
        name="nsa_prompt",
    )(q_pad, gates, ck, cv, ks, vs, kw, vw, ovt)


def _gla_prompt_kernel(q_ref, k_ref, la_ref, v_ref, o_ref, s_ref, s_scr, *, nch):
    c = GLA_CHUNK
    ti = pl.program_id(2)

    @pl.when(ti == 0)
    def _():
        s_scr[...] = jnp.zeros_like(s_scr)

    two_dk, two_dv = 2 * GLA_DK, 2 * GLA_DV
    r_i = lax.broadcasted_iota(jnp.int32, (c, c), 0)
    c_i = lax.broadcasted_iota(jnp.int32, (c, c), 1)
    tril = jnp.where(c_i <= r_i, 1.0, 0.0).astype(BF16)
    e_r = lax.broadcasted_iota(jnp.int32, (two_dk, two_dk), 0)
    e_c = lax.broadcasted_iota(jnp.int32, (two_dk, two_dk), 1)
    eye = jnp.where(e_r == e_c, 1.0, 0.0).astype(BF16)
    b_r = lax.broadcasted_iota(jnp.int32, (two_dk, two_dv), 0) // GLA_DK
    b_c = lax.broadcasted_iota(jnp.int32, (two_dk, two_dv), 1) // GLA_DV
    same_head = jnp.where(b_r == b_c, 1.0, 0.0)
    ones_bd = same_head.astype(BF16)
    head0 = lax.broadcasted_iota(jnp.int32, (1, two_dk), 1) < GLA_DK
    pos = lax.broadcasted_iota(jnp.int32, (c, 1), 0)
    nsub = c // GLA_SUB
    w_i = lax.broadcasted_iota(jnp.int32, (1, GLA_SUB, 1), 1)

    def chunk(ci):
        rows = pl.ds(pl.multiple_of(ci * c, c), c)
        q = q_ref[0, rows, :]
        k = k_ref[0, rows, :]
        la = la_ref[0, rows, :]
        v = v_ref[0, rows, :]
        vb = v.astype(BF16)
        s0 = s_scr[...]
        a1, a2, a3 = _split3(la)
        cum = _dot(tril, a1) + _dot(tril, a2) + _dot(tril, a3)
        last = cum[c - 1:c, :]
        o = _dot((q * jnp.exp(cum)).astype(BF16), s0.astype(BF16))
        attn = jnp.zeros((2 * c, c), F32)
        for j in range(nsub - 1):
            ce = cum[GLA_SUB * (j + 1) - 1:GLA_SUB * (j + 1), :]
            qh = jnp.where(pos >= GLA_SUB * (j + 1), q * jnp.exp(jnp.minimum(cum - ce, 0.0)), 0.0)
            in_j = jnp.where(pos >= GLA_SUB * j, jnp.where(pos < GLA_SUB * (j + 1), 1.0, 0.0), 0.0)
            kh = in_j * (k * jnp.exp(jnp.minimum(ce - cum, 0.0)))
            qh2 = jnp.concatenate([jnp.where(head0, qh, 0.0), jnp.where(head0, 0.0, qh)], axis=0)
            attn = attn + _dot_nt(qh2.astype(BF16), kh.astype(BF16))
        o_off = _dot(attn.astype(BF16), vb)
        o = o + jnp.concatenate([o_off[:c, :GLA_DV], o_off[c:, GLA_DV:]], axis=1)
        q3 = q.reshape(nsub, GLA_SUB, two_dk)
        k3 = k.reshape(nsub, GLA_SUB, two_dk)
        c3 = cum.reshape(nsub, GLA_SUB, two_dk)
        v3 = v.reshape(nsub, GLA_SUB, two_dv)
        ws = []
        for u in range(GLA_SUB):
            w = q3 * k3[:, u:u + 1, :] * jnp.exp(jnp.minimum(c3 - c3[:, u:u + 1, :], 0.0))
            ws.append(jnp.where(w_i >= u, w, 0.0).reshape(c, two_dk).astype(BF16))
        a_rep = _dot(jnp.concatenate(ws, axis=0), ones_bd)
        od = jnp.zeros((nsub, GLA_SUB, two_dv), F32)
        for u in range(GLA_SUB):
            od = od + a_rep[u * c:(u + 1) * c].reshape(nsub, GLA_SUB, two_dv) * v3[:, u:u + 1, :]
        o_ref[0, rows, :] = o + od.reshape(c, two_dv)
        kd = (k * jnp.exp(last - cum)).astype(BF16)
        kd_t = _dot_nt(eye, kd).astype(BF16)
        l1, l2, l3 = _split3(jnp.broadcast_to(last, (8, two_dk)))
        last_col = (_dot_nt(eye, l1) + _dot_nt(eye, l2) + _dot_nt(eye, l3))[:, 0:1]
        s_scr[...] = jnp.exp(last_col) * s0 + same_head * _dot(kd_t, vb)

    def several(i, _):
        for j in range(GLA_UNROLL):
            chunk(GLA_UNROLL * i + j)
        return 0

    lax.fori_loop(0, nch // GLA_UNROLL, several, 0)

    @pl.when(ti == pl.num_programs(2) - 1)
    def _():
        s = s_scr[...]
        s_ref[0, 0] = s[:GLA_DK, :GLA_DV]
        s_ref[0, 1] = s[GLA_DK:, GLA_DV:]


def _gla_prompt(q, k, la, v, nch):
    b, t, _ = q.shape
    tc = nch * GLA_CHUNK
    hp = GLA_HEADS // 2
    qs = pl.BlockSpec((1, tc, 2 * GLA_DK), lambda bi, pi, ti: (bi, ti, pi))
    vs = pl.BlockSpec((1, tc, 2 * GLA_DV), lambda bi, pi, ti: (bi, ti, pi))
    return pl.pallas_call(
        functools.partial(_gla_prompt_kernel, nch=nch),
        grid=(b, hp, t // tc),
        in_specs=[qs, qs, qs, vs],
        out_specs=[vs, pl.BlockSpec((1, 2, GLA_DK, GLA_DV), lambda bi, pi, ti: (bi, pi, 0, 0))],
        out_shape=[jax.ShapeDtypeStruct((b, t, GLA_WIDTH), F32),
                   jax.ShapeDtypeStruct((b, GLA_HEADS, GLA_DK, GLA_DV), F32)],
        scratch_shapes=[pltpu.VMEM((2 * GLA_DK, 2 * GLA_DV), F32)],
        compiler_params=_cparams(("parallel", "parallel", "arbitrary")),
        name="gla_prompt",
    )(q, k, la, v)


def _out_kernel(x_ref, gt_ref, on_ref, ogla_ref, og_ref, gn_ref, w_ref, o_ref):
    y = _dot(on_ref[0].astype(BF16), w_ref[:NSA_WIDTH, :])
    og = og_ref[0]
    ogla = ogla_ref[0]
    heads = []
    for h in range(GLA_HEADS):
        o = ogla[:, h * GLA_DV:(h + 1) * GLA_DV]
        o = o * lax.rsqrt(jnp.mean(o * o, axis=-1, keepdims=True) + EPS) * gn_ref[...]
        gate = og[:, h * GLA_DV:(h + 1) * GLA_DV]
        heads.append((o * (gate * jax.nn.sigmoid(gate))).astype(BF16))
    y = y + _dot(jnp.concatenate(heads, axis=1), w_ref[NSA_WIDTH:, :])
    o_ref[0] = x_ref[0] + gt_ref[0] * y


def _out_proj(x, gt, o_nsa, o_gla, o_g, gla_norm, w_out, tm):
    bx, tx, d = x.shape
    xs = pl.BlockSpec((1, tm, d), lambda b, i: (b, i, 0))
    return pl.pallas_call(
        _out_kernel,
        grid=(bx, tx // tm),
        in_specs=[xs, _mod_spec(gt, tm, 2),
                  pl.BlockSpec((1, tm, NSA_WIDTH), lambda b, i: (b, i, 0)),
                  pl.BlockSpec((1, tm, GLA_WIDTH), lambda b, i: (b, i, 0)),
                  pl.BlockSpec((1, tm, GLA_WIDTH), lambda b, i: (b, i, 0)),
                  pl.BlockSpec((1, GLA_DV), lambda b, i: (0, 0)),
                  pl.BlockSpec(w_out.shape, lambda b, i: (0, 0))],
        out_specs=xs,
        out_shape=jax.ShapeDtypeStruct(x.shape, F32),
        compiler_params=_cparams(("parallel", "parallel")),
        name="out_proj",
    )(x, gt, o_nsa, o_gla, o_g, gla_norm, w_out)


def _nsa_sample_kernel(pt_ref, *rest, n_pages, page, past, wlen, nseq):
    del pt_ref
    per = 6 + 2 * n_pages
    (w1_ref, pe_ref, b1_ref, w2_ref, ovt_ref, o_ref, x_scr, bias_scr) = rest[nseq * per:]

    @pl.when(pl.program_id(0) == 0)
    def _():
        for kv in range(2):
            acc = jnp.zeros((8, 2 * LANES), F32)
            for s2 in range(CMP_BLOCK // 2):
                pe2 = jnp.broadcast_to(pe_ref[kv, s2:s2 + 1, :], (8, 2 * LANES))
                acc = acc + _dot(pe2.astype(BF16), w1_ref[kv, s2])
            bias_scr[kv] = acc + b1_ref[kv]

    for s in range(nseq):
        refs = rest[s * per:(s + 1) * per]
        _nsa_sample_seq(refs[:5], refs[5:5 + n_pages], refs[5 + n_pages:5 + 2 * n_pages], refs[5 + 2 * n_pages],
                        (w1_ref, w2_ref, ovt_ref, bias_scr), x_scr.at[s], o_ref.at[s],
                        n_pages=n_pages, page=page, past=past, wlen=wlen)


def _nsa_sample_seq(row_refs, cmp_pages, slc_pages, cw_ref, shared, x_scr, o_ref, *, n_pages, page, past, wlen):
    qall_ref, gate_ref, kvc_ref, kvs_ref, kvw_ref = row_refs
    w1_ref, w2_ref, ovt_ref, bias_scr = shared
    nh = NSA_HEADS
    ncb = past // CMP_STRIDE
    qa = qall_ref[0]
    qa_f = qa.astype(F32)
    hrow = lax.broadcasted_iota(jnp.int32, (nh, 1), 0)
    slope = jnp.exp2(-(hrow + 1).astype(F32)) * LOG2E
    lane = lax.broadcasted_iota(jnp.int32, (1, LANES), 1)

    cpp = page // CMP_STRIDE
    r_i = lax.broadcasted_iota(jnp.int32, (page, page), 0)
    p_i = lax.broadcasted_iota(jnp.int32, (page, page), 1)
    in_chunk = jnp.bitwise_and(p_i, CMP_STRIDE - 1)
    perm_t = jnp.where(r_i * CMP_STRIDE == in_chunk * (cpp * CMP_STRIDE) + (p_i - in_chunk), 1.0, 0.0).astype(BF16)
    nrow = ncb + 8
    ckv = []
    for kv in range(2):
        for pg in range(n_pages):
            rows = _dot_nt(perm_t, cmp_pages[pg][0, 0, kv].reshape(LANES, page).astype(BF16))
            for s in range(CMP_STRIDE):
                x_scr[kv, s, pg * cpp:(pg + 1) * cpp, :] = rows[s * cpp:(s + 1) * cpp, :]
        new = kvc_ref[0][:, kv * LANES:(kv + 1) * LANES]
        first = lax.broadcasted_iota(jnp.int32, (8, 1), 0) == 0
        for s in range(CMP_STRIDE):
            x_scr[kv, s, ncb:nrow, :] = jnp.where(first, new, 0.0) if s == 0 else jnp.zeros((8, LANES), F32)
        acc_lo = jnp.zeros((nrow, 2 * LANES), F32)
        acc_hi = jnp.zeros((nrow, 2 * LANES), F32)
        half = CMP_STRIDE // 2
        for j in range(half):
            lhs = jnp.concatenate([x_scr[kv, 2 * j], x_scr[kv, 2 * j + 1]], axis=1).astype(BF16)
            acc_lo = acc_lo + _dot(lhs, w1_ref[kv, j])
            acc_hi = acc_hi + _dot(lhs, w1_ref[kv, half + j])
        h = jax.nn.gelu(acc_lo[:ncb] + pltpu.roll(acc_hi, nrow - 1, 0)[:ncb] + bias_scr[kv][0:1])
        ckv.append(_dot(h.astype(BF16), w2_ref[kv]).astype(BF16))
    c_end = lane * CMP_STRIDE + (CMP_BLOCK - 1)
    dist_c = past - c_end
    mask_c = dist_c >= 0
    lg_c = jnp.where(mask_c, _dot_nt(qa, ckv[0]) - slope * dist_c.astype(F32), NEG)
    e_c = jnp.where(mask_c, jnp.exp2(lg_c - jnp.max(lg_c, axis=-1, keepdims=True)), 0.0)
    pc = e_c / jnp.maximum(jnp.sum(e_c, axis=-1, keepdims=True), 1e-30)
    o_cmp = _dot(pc.astype(BF16), ckv[1])

    ps0 = jnp.sum(jnp.where(hrow < NSA_GROUP, pc, 0.0), axis=0, keepdims=True)
    ps1 = jnp.sum(jnp.where(hrow >= NSA_GROUP, pc, 0.0), axis=0, keepdims=True)
    psum = jnp.where(hrow == 0, ps0, jnp.where(hrow == 1, ps1, 0.0))
    psum = jnp.concatenate([psum, jnp.zeros((LANES - nh, LANES), F32)], axis=0)
    p_hi = psum.astype(BF16)
    p_lo = (psum - p_hi.astype(F32)).astype(BF16)
    imp_t = _dot_nt(ovt_ref[...], p_hi) + _dot_nt(ovt_ref[...], p_lo)
    blk = lax.broadcasted_iota(jnp.int32, (LANES, 1), 0)
    cur = past // SEL_BLOCK
    forced = jnp.where(blk == 0, 1.0, 0.0) + jnp.where(blk == cur, 1.0, 0.0) + jnp.where(blk == cur - 1, 1.0, 0.0)
    score = jnp.where(blk * SEL_BLOCK <= past, imp_t + jnp.where(forced > 0.5, FORCE_BONUS, 0.0), NEG)
    sel = _top_k_mask(score, blk.astype(F32), SEL_TOPK, axis=0).T
    sel_h = jnp.where(hrow < NSA_GROUP, sel[0:1], sel[1:2])

    def attend(tiles, new_row, new_neg):
        lgs = [_dot(qa, k_t) + neg for k_t, _, neg in tiles]
        lg_new = jnp.sum(qa_f * new_row[:, :LANES], axis=-1, keepdims=True) + new_neg
        m = lg_new
        for lg in lgs:
            m = jnp.maximum(m, jnp.max(lg, axis=-1, keepdims=True))
        p_new = jnp.exp2(lg_new - m)
        l = p_new
        acc = p_new * new_row[:, LANES:]
        for lg, (_, v_t, _) in zip(lgs, tiles):
            p = jnp.exp2(lg - m)
            l = l + jnp.sum(p, axis=-1, keepdims=True)
            acc = acc + _dot_nt(p.astype(BF16), v_t)
        return acc / l

    per = page // SEL_BLOCK
    tiles = []
    for pg in range(n_pages):
        blkp = slc_pages[pg]
        dist = past - (pg * page + lane)
        selx = jnp.zeros((nh, LANES), F32)
        for u in range(per):
            in_u = jnp.where(lane >= u * SEL_BLOCK, jnp.where(lane < (u + 1) * SEL_BLOCK, 1.0, 0.0), 0.0)
            selx = selx + in_u * sel_h[:, pg * per + u:pg * per + u + 1]
        neg = jnp.where(dist >= 0, (selx - 1.0) * 1e30, NEG) - slope * dist.astype(F32)
        tiles.append((blkp[0, 0, 0].reshape(LANES, page).astype(BF16),
                      blkp[0, 0, 1].reshape(LANES, page).astype(BF16), neg))
    o_slc = attend(tiles, kvs_ref[0], (sel_h[:, cur:cur + 1] - 1.0) * 1e30)

    wlane = lax.broadcasted_iota(jnp.int32, (1, wlen), 1)
    w_pos = past - wlen + wlane
    dist = past - w_pos
    ok = jnp.where(dist >= 0, jnp.where(dist < WINDOW, jnp.where(w_pos >= 0, 1.0, 0.0), 0.0), 0.0)
    neg = (ok - 1.0) * 1e30 - slope * dist.astype(F32)
    tiles = [(cw_ref[0, 0, 0].reshape(LANES, wlen).astype(BF16), cw_ref[0, 0, 1].reshape(LANES, wlen).astype(BF16), neg)]
    o_win = attend(tiles, kvw_ref[0], jnp.zeros((nh, 1), F32))

    gates = gate_ref[0]
    pick = lambda a: jnp.where(hrow < NSA_GROUP, a[:, :NSA_HD], a[:, NSA_HD:])
    o_ref[...] = gates[:, 0:1] * pick(o_cmp) + gates[:, 1:2] * pick(o_slc) + gates[:, 2:3] * pick(o_win)


def _nsa_sample(layer, page_table, q_all, gates, kvc, kvs, kvw, cache_c, cache_s, cache_w, cw):
    nb, n_pages = page_table.shape
    depth, n_phys, page = cache_c.shape[:3]
    past = n_pages * page
    wlen = cache_w.shape[2]
    ncb = past // CMP_STRIDE
    cc = cache_c.transpose(0, 1, 3, 4, 5, 2)
    cs = cache_s.transpose(0, 1, 3, 4, 5, 2)
    cwin = cache_w.transpose(0, 1, 3, 4, 5, 2)
    nseq = NSA_SAMPLE_SEQS if nb % NSA_SAMPLE_SEQS == 0 else 1
    full = lambda a: pl.BlockSpec(a.shape, lambda b, pt: (0,) * a.ndim)
    ovt = _overlap_matrix(ncb, LANES).T
    wargs = (cw["w1s"], cw["pes"], cw["b1s"], cw["w2s"], ovt)
    in_specs, args = [], []
    for s in range(nseq):
        seq = lambda b, s=s: nseq * b + s
        heads = pl.BlockSpec((1, NSA_HEADS, LANES), lambda b, pt, seq=seq: (seq(b), 0, 0))
        row = pl.BlockSpec((1, 1, KV_COLS), lambda b, pt, seq=seq: (seq(b), 0, 0))
        pages = [pl.BlockSpec((1, 1, 2, NSA_KV_HEADS, NSA_HD, page),
                              lambda b, pt, seq=seq, j=j: (layer, pt[seq(b), j], 0, 0, 0, 0)) for j in range(n_pages)]
        win = pl.BlockSpec((1, 1, 2, NSA_KV_HEADS, NSA_HD, wlen), lambda b, pt, seq=seq: (layer, seq(b), 0, 0, 0, 0))
        in_specs += [heads, heads, row, row, row] + pages + pages + [win]
        args += [q_all, gates, kvc, kvs, kvw] + [cc] * n_pages + [cs] * n_pages + [cwin]
    grid_spec = pltpu.PrefetchScalarGridSpec(
        num_scalar_prefetch=1,
        grid=(nb // nseq,),
        in_specs=in_specs + [full(a) for a in wargs],
        out_specs=pl.BlockSpec((nseq, NSA_HEADS, NSA_HD), lambda b, pt: (b, 0, 0)),
        scratch_shapes=[pltpu.VMEM((nseq, 2, CMP_STRIDE, ncb + 8, LANES), F32), pltpu.VMEM((2, 8, 2 * LANES), F32)],
    )
    return pl.pallas_call(
        functools.partial(_nsa_sample_kernel, n_pages=n_pages, page=page, past=past, wlen=wlen, nseq=nseq),
        grid_spec=grid_spec,
        out_shape=jax.ShapeDtypeStruct((nb, NSA_HEADS, NSA_HD), F32),
        compiler_params=_cparams(("arbitrary",)),
        name="nsa_sample",
    )(page_table, *args, *wargs)


def _gla_sample_kernel(col_ref, v_ref, s_ref, o_ref, sn_ref, *, bb):
    for b in range(bb):
        cols = col_ref[b]
        for h in range(GLA_HEADS):
            a = jnp.exp(cols[:, h:h + 1])
            k = cols[:, GLA_HEADS + h:GLA_HEADS + h + 1]
            q = cols[:, 2 * GLA_HEADS + h:2 * GLA_HEADS + h + 1]
            v = v_ref[b, h:h + 1, :]
            s0 = s_ref[b, h]
            qk = jnp.sum(q * k, axis=0, keepdims=True)
            o_ref[b, h:h + 1, :] = jnp.sum((q * a) * s0, axis=0, keepdims=True) + qk * v
            sn_ref[b, h] = a * s0 + k * v


def _gla_sample(cols, v, state, bb):
    nb = v.shape[0]
    return pl.pallas_call(
        functools.partial(_gla_sample_kernel, bb=bb),
        grid=(nb // bb,),
        in_specs=[pl.BlockSpec((bb, GLA_DK, 16), lambda i: (i, 0, 0)),
                  pl.BlockSpec((bb, GLA_HEADS, GLA_DV), lambda i: (i, 0, 0)),
                  pl.BlockSpec((bb, GLA_HEADS, GLA_DK, GLA_DV), lambda i: (i, 0, 0, 0))],
        out_specs=[pl.BlockSpec((bb, GLA_HEADS, GLA_DV), lambda i: (i, 0, 0)),
                   pl.BlockSpec((bb, GLA_HEADS, GLA_DK, GLA_DV), lambda i: (i, 0, 0, 0))],
        out_shape=[jax.ShapeDtypeStruct((nb, GLA_HEADS, GLA_DV), F32),
                   jax.ShapeDtypeStruct((nb, GLA_HEADS, GLA_DK, GLA_DV), F32)],
        compiler_params=_cparams(("parallel",)),
        name="gla_sample",
    )(cols, v, state)


def _proj_weight(w):
    d = w.shape[0]
    offs = np.cumsum((0,) + IN_SIZES)
    seg = lambda i: w[:, offs[i]:offs[i + 1]]

    def slots(a, n):
        per = a.shape[1] // n
        return jnp.pad(a.reshape(d, n, per), ((0, 0), (0, 0), (0, LANES - per))).reshape(d, n * LANES)

    return jnp.concatenate([slots(seg(0), NSA_HEADS), seg(1), seg(2), seg(3), slots(seg(4), NSA_KV_HEADS),
                            seg(5), seg(6), seg(7), slots(seg(8), 1), seg(9)], axis=1)


def _cmp_weights(pe, w1, b1, w2):
    g = NSA_KV_HEADS
    eye = jnp.eye(2, dtype=F32)
    eg = jnp.eye(g, dtype=F32)
    w1r = w1.reshape(2, 2, CMP_STRIDE, NSA_HD, CMP_HIDDEN)
    big = jnp.einsum("krsdh,ka,gb->rskgdabh", w1r, eye, eg).reshape(2, CMP_STRIDE * KV_COLS, 2 * g * CMP_HIDDEN)
    w2big = jnp.einsum("khd,ka,gb->kghabd", w2, eye, eg).reshape(2 * g * CMP_HIDDEN, KV_COLS)
    per = pe.reshape(2, 2, CMP_STRIDE, NSA_HD)
    pet = jnp.broadcast_to(per.transpose(1, 2, 0, 3)[:, :, :, None, :], (2, CMP_STRIDE, 2, g, NSA_HD))
    pet = pet.reshape(2, 1, CMP_STRIDE * KV_COLS)
    b1big = jnp.broadcast_to(b1[:, None, :], (2, g, CMP_HIDDEN)).reshape(1, 2 * g * CMP_HIDDEN)
    w1s = jnp.einsum("ksdh,gb->ksgdbh", w1.reshape(2, CMP_BLOCK, NSA_HD, CMP_HIDDEN), eg)
    w1s = w1s.reshape(2, CMP_BLOCK // 2, 2 * g * NSA_HD, g * CMP_HIDDEN)
    w2s = jnp.einsum("khd,gb->kghbd", w2, eg).reshape(2, g * CMP_HIDDEN, g * NSA_HD)
    pes = jnp.broadcast_to(pe[:, :, None, :], (2, CMP_BLOCK, g, NSA_HD)).reshape(2, CMP_BLOCK // 2, 2 * g * NSA_HD)
    b1s = jnp.broadcast_to(b1[:, None, None, :], (2, 1, g, CMP_HIDDEN)).reshape(2, 1, g * CMP_HIDDEN)
    return {
        "w1s": w1s.astype(BF16), "w2s": w2s.astype(BF16), "pes": pes, "b1s": b1s,
        "w1lo_t": big[0].T.astype(BF16), "w1hi_t": big[1].T.astype(BF16),
        "pe_lo": pet[0], "pe_hi": pet[1], "b1_col": b1big.T, "w2_t": w2big.T.astype(BF16),
    }


def _mods(mod, rows):
    m = mod.reshape(mod.shape[0], N_MOD, D_MODEL)
    if rows == 1:
        return [m[:, j][:, None, :] for j in range(N_MOD)]
    return [m[:, j][None, :, :] for j in range(N_MOD)]


def _layer_prompt(x, mods, lw, tm, tq, tk, final_gain):
    b, t, _ = x.shape
    x = _ffn(x, mods[0], mods[1], mods[2], lw["norm_g"][0:1], lw["f1_in"], lw["f1_out"], tm)
    (qn, kvc, _, _, kvct, kvst, kvwt, kvtb, gn, qg, kg, vg, la, og) = _proj(
        x, mods[3], mods[4], lw["norm_g"][1:2], lw["w_all"], lw["wkvt"], lw["wa2p"], lw["ba"], lw["qb"], tm)
    g = NSA_KV_HEADS
    ncp = t // CMP_STRIDE
    ckvt = _cmp_prompt(kvc.reshape(b, ncp, CMP_STRIDE * KV_COLS), lw["cmp"]).astype(BF16)
    ckvt = ckvt.reshape(b, 2, g, NSA_HD, ncp)
    kvs5 = kvtb[:, :KV_COLS].reshape(b, 2, g, NSA_HD, t)
    kvw5 = kvtb[:, KV_COLS:].reshape(b, 2, g, NSA_HD, t)
    pos_rows = _alibi_k_rows(np.arange(t))
    ones_row = jnp.ones((1, t), BF16)
    onehot = jnp.asarray(np.arange(max(LANES, t // SEL_BLOCK))[:, None] == np.arange(t)[None, :] // SEL_BLOCK, BF16)
    lead = (b, g)
    ck = _aug_rows([ckvt[:, 0], _alibi_k_rows(np.arange(ncp) * CMP_STRIDE + CMP_BLOCK - 1)], LANES, lead)
    ks = _aug_rows([onehot, kvs5[:, 0], pos_rows], onehot.shape[0] + LANES, lead)
    vs = _aug_rows([kvs5[:, 1], ones_row], LANES, lead)
    kw = _aug_rows([kvw5[:, 0], pos_rows], LANES, lead)
    vw = _aug_rows([kvw5[:, 1], ones_row], LANES, lead)
    o_nsa = _nsa_prompt(qn, gn, ck, ckvt[:, 1], ks, vs, kw, vw, tq, tk)
    o_gla, s_fin = _gla_prompt(qg, kg, la, vg, min(8, t // GLA_CHUNK))
    x = _out_proj(x, mods[5], o_nsa, o_gla, og, lw["gla_norm"], lw["w_out"], tm)
    x = _ffn(x, mods[6], mods[7], mods[8], lw["norm_g"][2:3], lw["f2_in"], lw["f2_out"], tm, final_gain)
    win = min(WINDOW, t)
    back = lambda a: a.reshape(b, 2, NSA_KV_HEADS, NSA_HD, a.shape[-1]).transpose(0, 4, 1, 2, 3)
    return x, (back(kvct), back(kvst), back(kvwt[:, :, t - win:]), s_fin)


def _layer_sample(layer, x, mods, lw, cache_c, cache_s, cache_w, state, page_table, final_gain):
    nb = x.shape[1]
    x = _ffn(x, mods[0], mods[1], mods[2], lw["norm_g"][0:1], lw["f1_in"], lw["f1_out"], nb)
    (qn, kvc, kvs, kvw, kvct, kvst, kvwt, _, gn, qg, kg, vg, la, og) = _proj(
        x, mods[3], mods[4], lw["norm_g"][1:2], lw["w_all"], lw["wkvt"], lw["wa2p"], lw["ba"], lw["qb"], nb)
    q4 = qn[0].reshape(nb, NSA_KV_HEADS, NSA_GROUP, LANES)[..., :NSA_HD]
    q_all = (q4[:, :, :, None, :] * jnp.eye(NSA_KV_HEADS, dtype=BF16)[None, :, None, :, None]).reshape(nb, NSA_HEADS, LANES)
    gates = gn[0].reshape(nb, NSA_KV_HEADS, LANES)[..., :3 * NSA_GROUP].reshape(nb, NSA_HEADS, 3)
    gates = jnp.pad(gates, ((0, 0), (0, 0), (0, LANES - 3)))
    r3 = lambda a: a[0][:, None, :]
    o_nsa = _nsa_sample(layer, page_table, q_all, gates, r3(kvc), r3(kvs), r3(kvw),
                        cache_c, cache_s, cache_w, lw["cmp"])
    o_nsa = o_nsa.reshape(1, nb, NSA_WIDTH)
    col = lambda a: a[0].reshape(nb, GLA_HEADS, GLA_DK).transpose(0, 2, 1)
    cols = jnp.concatenate([col(la), col(kg), col(qg), jnp.zeros((nb, GLA_DK, 16 - 3 * GLA_HEADS), F32)], axis=-1)
    o_gla, s_new = _gla_sample(cols, vg[0].reshape(nb, GLA_HEADS, GLA_DV), state, 8)
    o_gla = o_gla.reshape(1, nb, GLA_WIDTH)
    x = _out_proj(x, mods[5], o_nsa, o_gla, og, lw["gla_norm"], lw["w_out"], nb)
    x = _ffn(x, mods[6], mods[7], mods[8], lw["norm_g"][2:3], lw["f2_in"], lw["f2_out"], nb, final_gain)
    kv5 = lambda a: a[0].reshape(2, NSA_KV_HEADS, NSA_HD, nb).transpose(3, 0, 1, 2)[:, None]
    return x, (kv5(kvct), kv5(kvst), kv5(kvwt), s_new)


def kernel(x_prompt, x_sample, c_prompt, c_sample, cache_kv_cmp, cache_kv_slc, cache_kv_win, state_gla, page_table, w_ada, b_ada, norm_g, ffn1_w_in, ffn1_w_out, w_in, cmp_pe, cmp_w1, cmp_b1, cmp_w2, gla_wa2, gla_ba, gla_norm, w_out, ffn2_w_in, ffn2_w_out, final_norm):
    depth = w_ada.shape[0]
    b, t, d = x_prompt.shape
    nb = x_sample.shape[0]
    bp = -(-b // 8) * 8
    c_all = jnp.concatenate([c_prompt, jnp.zeros((bp - b, d), F32), c_sample], axis=0)
    mod_all = _ada(c_all, w_ada, b_ada)

    tm = min(512, t)
    tq = min(256, t)
    tk = min(512, t)
    xp = x_prompt
    xs = x_sample.reshape(1, nb, d)
    outs_p, outs_s = [], []
    for l in range(depth):
        lw = {
            "norm_g": norm_g[l],
            "f1_in": ffn1_w_in[l].astype(BF16), "f1_out": ffn1_w_out[l].astype(BF16),
            "f2_in": ffn2_w_in[l].astype(BF16), "f2_out": ffn2_w_out[l].astype(BF16),
            "w_all": _proj_weight(w_in[l]).astype(BF16),
            "qb": _q_bias(),
            "wkvt": w_in[l][:, NSA_WIDTH:NSA_WIDTH + 3 * KV_COLS].T.astype(BF16),
            "wa2p": jnp.pad(gla_wa2[l], ((0, 128 - GLA_RANK), (0, 0))).astype(BF16),
            "ba": gla_ba[l][None, :],
            "gla_norm": gla_norm[l][None, :],
            "w_out": w_out[l].astype(BF16),
            "cmp": _cmp_weights(cmp_pe[l], cmp_w1[l], cmp_b1[l], cmp_w2[l]),
        }
        fg = final_norm[None, :] if l == depth - 1 else None
        xp, st_p = _layer_prompt(xp, _mods(mod_all[l, :b], 1), lw, tm, tq, tk, fg)
        xs, st_s = _layer_sample(l, xs, _mods(mod_all[l, bp:], nb), lw, cache_kv_cmp, cache_kv_slc,
                                 cache_kv_win, state_gla[l], page_table, fg)
        outs_p.append(st_p)
        outs_s.append(st_s)
    stack = lambda outs, j: jnp.stack([o[j] for o in outs])
    return (xp, xs.reshape(nb, 1, d),
            stack(outs_p, 0), stack(outs_p, 1), stack(outs_p, 2), stack(outs_p, 3),
            stack(outs_s, 0), stack(outs_s, 1), stack(outs_s, 2), stack(outs_s, 3))
```

```python
import functools

import numpy as np
import jax
import jax.numpy as jnp
from jax import lax
from jax.experimental import pallas as pl
from jax.experimental.pallas import tpu as pltpu

F32 = jnp.float32
BF16 = jnp.bfloat16

D_MODEL = 1024
NSA_HD = 64
NSA_HEADS = 8
NSA_KV_HEADS = 2
NSA_GROUP = 4
NSA_WIDTH = NSA_HEADS * NSA_HD
CMP_BLOCK = 32
CMP_STRIDE = 16
CMP_HIDDEN = 2 * NSA_HD
SEL_BLOCK = 64
SEL_TOPK = 16
WINDOW = 512
FORCE_BONUS = 1000.0
GLA_HEADS = 4
GLA_DV = 128
GLA_DK = 64
GLA_WIDTH = GLA_HEADS * GLA_DV
GLA_RANK = 16
GLA_TAU = 16.0
GLA_CHUNK = 64
GLA_SUB = 16
FFN_DIM = 2816
N_MOD = 9
EPS = 1e-6
KV_COLS = 2 * NSA_KV_HEADS * NSA_HD
IN_SIZES = (NSA_WIDTH, KV_COLS, KV_COLS, KV_COLS, 3 * NSA_HEADS,
            GLA_HEADS * GLA_DK, GLA_HEADS * GLA_DK, GLA_WIDTH, GLA_RANK, GLA_WIDTH)
IN_PAD = (NSA_HEADS * 128, KV_COLS, KV_COLS, KV_COLS, NSA_KV_HEADS * 128,
          GLA_HEADS * GLA_DK, GLA_HEADS * GLA_DK, GLA_WIDTH, 128, GLA_WIDTH)
IN_OFF = tuple(int(v) for v in np.cumsum((0,) + IN_PAD))

LANES = 128
NEG = -1e30
LOG2E = 1.4426950408889634
VMEM_LIMIT = 56 * 1024 * 1024
FFN_TF = FFN_DIM // 2
NSA_SAMPLE_SEQS = 2
GLA_UNROLL = 4


def _cparams(sem):
    return pltpu.CompilerParams(dimension_semantics=sem, vmem_limit_bytes=VMEM_LIMIT)


def _dot(a, b):
    return jnp.dot(a, b, preferred_element_type=F32)


def _dot_nt(a, b):
    return lax.dot_general(a, b, (((1,), (1,)), ((), ())), preferred_element_type=F32)


def _split3(x):
    x1 = x.astype(BF16)
    r = x - x1.astype(F32)
    x2 = r.astype(BF16)
    x3 = (r - x2.astype(F32)).astype(BF16)
    return x1, x2, x3


def _rms_mod(x, g, sc, sh):
    y = x * lax.rsqrt(jnp.mean(x * x, axis=-1, keepdims=True) + EPS) * g
    return y * (1.0 + sc) + sh


def _masked_softmax(lg, mask):
    lg = jnp.where(mask, lg, NEG)
    m = jnp.max(lg, axis=-1, keepdims=True)
    e = jnp.where(mask, jnp.exp(lg - m), 0.0)
    return e / jnp.maximum(jnp.sum(e, axis=-1, keepdims=True), 1e-30)


def _top_k_mask(score, blk, k, axis=-1):
    sel = jnp.zeros_like(score)
    big = float(score.shape[axis])
    for _ in range(k):
        m = jnp.max(score, axis=axis, keepdims=True)
        idx = jnp.min(jnp.where(score == m, blk, big), axis=axis, keepdims=True)
        hit = blk == idx
        sel = jnp.where(hit, 1.0, sel)
        score = jnp.where(hit, -jnp.inf, score)
    return sel


def _ada_kernel(c_ref, w_ref, b_ref, o_ref):
    c = c_ref[...]
    s = c * jax.nn.sigmoid(c)
    o_ref[0] = _dot(s.astype(BF16), w_ref[0].astype(BF16)) + b_ref[0]


def _ada(c_all, w_ada, b_ada):
    depth, d, n = w_ada.shape
    m = c_all.shape[0]
    tn = 1024
    return pl.pallas_call(
        _ada_kernel,
        grid=(depth, n // tn),
        in_specs=[pl.BlockSpec((m, d), lambda l, j: (0, 0)),
                  pl.BlockSpec((1, d, tn), lambda l, j: (l, 0, j)),
                  pl.BlockSpec((1, 1, tn), lambda l, j: (l, 0, j))],
        out_specs=pl.BlockSpec((1, m, tn), lambda l, j: (l, 0, j)),
        out_shape=jax.ShapeDtypeStruct((depth, m, n), F32),
        compiler_params=_cparams(("parallel", "parallel")),
        name="ada",
    )(c_all, w_ada, b_ada.reshape(depth, 1, n))


def _mod_spec(mod, tm, nargs):
    r = mod.shape[1]
    d = mod.shape[2]
    if nargs == 3:
        if r == 1:
            return pl.BlockSpec((1, 1, d), lambda b, i, f: (b, 0, 0))
        return pl.BlockSpec((1, tm, d), lambda b, i, f: (b, i, 0))
    if r == 1:
        return pl.BlockSpec((1, 1, d), lambda b, i: (b, 0, 0))
    return pl.BlockSpec((1, tm, d), lambda b, i: (b, i, 0))


def _ffn_kernel(x_ref, sh_ref, sc_ref, gt_ref, g_ref, wi_ref, wo_ref, fg_ref, o_ref, *, final):
    x = x_ref[0]
    h = _rms_mod(x, g_ref[...], sc_ref[0], sh_ref[0]).astype(BF16)
    y = jnp.zeros(x.shape, F32)
    for f in range(FFN_DIM // FFN_TF):
        g = _dot(h, wi_ref[:, f * FFN_TF:(f + 1) * FFN_TF])
        u = _dot(h, wi_ref[:, FFN_DIM + f * FFN_TF:FFN_DIM + (f + 1) * FFN_TF])
        a = (g * jax.nn.sigmoid(g) * u).astype(BF16)
        y = y + _dot(a, wo_ref[f * FFN_TF:(f + 1) * FFN_TF, :])
    x = x + 0.5 * gt_ref[0] * y
    if final:
        x = x * lax.rsqrt(jnp.mean(x * x, axis=-1, keepdims=True) + EPS) * fg_ref[...]
    o_ref[0] = x


def _ffn(x, sh, sc, gt, g, w_in, w_out, tm, final_gain=None):
    bx, tx, d = x.shape
    xs = pl.BlockSpec((1, tm, d), lambda b, i: (b, i, 0))
    ms = _mod_spec(sh, tm, 2)
    once = lambda a: pl.BlockSpec(a.shape, lambda b, i: (0,) * a.ndim, pipeline_mode=pl.Buffered(1))
    fg = g if final_gain is None else final_gain
    return pl.pallas_call(
        functools.partial(_ffn_kernel, final=final_gain is not None),
        grid=(bx, tx // tm),
        in_specs=[xs, ms, ms, ms, once(g), once(w_in), once(w_out), once(fg)],
        out_specs=xs,
        out_shape=jax.ShapeDtypeStruct(x.shape, F32),
        compiler_params=_cparams(("parallel", "parallel")),
        name="ffn",
    )(x, sh, sc, gt, g, w_in, w_out, fg)


def _proj_kernel(x_ref, sh_ref, sc_ref, g_ref, w_ref, wkvt_ref, wa2_ref, ba_ref, qb_ref,
                 qn_ref, kvc_ref, kvs_ref, kvw_ref, kvct_ref, kvst_ref, kvwt_ref, kvtb_ref,
                 gn_ref, qg_ref, kg_ref, vg_ref, la_ref, og_ref):
    h = _rms_mod(x_ref[0], g_ref[...], sc_ref[0], sh_ref[0]).astype(BF16)
    p = _dot(h, w_ref[...])
    o = IN_OFF
    qn_ref[0] = (p[:, o[0]:o[1]] * (NSA_HD ** -0.5 * LOG2E) + qb_ref[...]).astype(BF16)
    kvc_ref[0] = p[:, o[1]:o[2]]
    kvs_ref[0] = p[:, o[2]:o[3]]
    kvw_ref[0] = p[:, o[3]:o[4]]
    kvt = _dot_nt(wkvt_ref[...], h)
    kvct_ref[0] = kvt[0:KV_COLS]
    kvst_ref[0] = kvt[KV_COLS:2 * KV_COLS]
    kvwt_ref[0] = kvt[2 * KV_COLS:3 * KV_COLS]
    kvtb_ref[0] = kvt[KV_COLS:3 * KV_COLS].astype(BF16)
    gn_ref[0] = jax.nn.sigmoid(p[:, o[4]:o[5]])
    qg_ref[0] = p[:, o[5]:o[6]] * (GLA_DK ** -0.5)
    kg_ref[0] = p[:, o[6]:o[7]]
    vg_ref[0] = p[:, o[7]:o[8]]
    a_pre = _dot(p[:, o[8]:o[9]].astype(BF16), wa2_ref[...]) + ba_ref[...]
    log_sig = jnp.minimum(a_pre, 0.0) - jnp.log(1.0 + jnp.exp(-jnp.abs(a_pre)))
    la_ref[0] = log_sig * (1.0 / GLA_TAU)
    og_ref[0] = p[:, o[9]:o[10]]


def _proj(x, sh, sc, g, w_all, wkvt, wa2p, ba, qb, tm):
    bx, tx, d = x.shape
    xs = pl.BlockSpec((1, tm, d), lambda b, i: (b, i, 0))
    ms = _mod_spec(sh, tm, 2)
    full = lambda shape: pl.BlockSpec(shape, lambda b, i: (0,) * len(shape))
    rows = lambda w, dt: (pl.BlockSpec((1, tm, w), lambda b, i: (b, i, 0)), jax.ShapeDtypeStruct((bx, tx, w), dt))
    cols = lambda w, dt: (pl.BlockSpec((1, w, tm), lambda b, i: (b, 0, i)), jax.ShapeDtypeStruct((bx, w, tx), dt))
    outs = [rows(IN_PAD[0], BF16), rows(KV_COLS, F32), rows(KV_COLS, F32), rows(KV_COLS, F32),
            cols(KV_COLS, F32), cols(KV_COLS, F32), cols(KV_COLS, F32), cols(2 * KV_COLS, BF16),
            rows(IN_PAD[4], F32), rows(256, F32), rows(256, F32), rows(GLA_WIDTH, F32), rows(256, F32),
            rows(GLA_WIDTH, F32)]
    return pl.pallas_call(
        _proj_kernel,
        grid=(bx, tx // tm),
        in_specs=[xs, ms, ms, full((1, d)), full(w_all.shape), full(wkvt.shape), full(wa2p.shape),
                  full(ba.shape), full(qb.shape)],
        out_specs=[o[0] for o in outs],
        out_shape=[o[1] for o in outs],
        compiler_params=_cparams(("parallel", "parallel")),
        name="proj",
    )(x, sh, sc, g, w_all, wkvt, wa2p, ba, qb)


def _cmp_kernel(x_ref, pelo_ref, pehi_ref, w1lo_ref, w1hi_ref, b1_ref, w2_ref, o_ref):
    x = x_ref[0]
    n = x.shape[0]
    h_lo = _dot_nt(w1lo_ref[...], (x + pelo_ref[...]).astype(BF16))
    h_hi = _dot_nt(w1hi_ref[...], (x + pehi_ref[...]).astype(BF16))
    h = h_lo + pltpu.roll(h_hi, n - 1, 1) + b1_ref[...]
    o_ref[0] = _dot(w2_ref[...], jax.nn.gelu(h).astype(BF16))


def _cmp_prompt(x16, cw):
    b, n, w = x16.shape
    full = lambda a: pl.BlockSpec(a.shape, lambda i: (0,) * a.ndim)
    args = (cw["pe_lo"], cw["pe_hi"], cw["w1lo_t"], cw["w1hi_t"], cw["b1_col"], cw["w2_t"])
    return pl.pallas_call(
        _cmp_kernel,
        grid=(b,),
        in_specs=[pl.BlockSpec((1, n, w), lambda i: (i, 0, 0))] + [full(a) for a in args],
        out_specs=pl.BlockSpec((1, KV_COLS, n), lambda i: (i, 0, 0)),
        out_shape=jax.ShapeDtypeStruct((b, KV_COLS, n), F32),
        compiler_params=_cparams(("parallel",)),
        name="cmp_prompt",
    )(x16, *args)


def _nsa_prompt_kernel(q_ref, gate_ref, ck_ref, cv_ref, ks_ref, vs_ref, kw_ref, vw_ref, ovt_ref,
                       o_ref, ids_scr, *, tq, tk, ncp, nsp):
    i = pl.program_id(2)
    q0 = i * tq
    nr = NSA_GROUP * tq
    row = q0 + jnp.bitwise_and(lax.broadcasted_iota(jnp.int32, (nr, 1), 0), tq - 1)
    row_t = q0 + lax.broadcasted_iota(jnp.int32, (tq, 1), 0)
    qblk = q_ref[0]
    q = jnp.concatenate([qblk[:, r * LANES:(r + 1) * LANES] for r in range(NSA_GROUP)], axis=0)

    def per_head(x, mask):
        return (x.reshape(NSA_GROUP, tq, x.shape[-1]) + mask[None]).reshape(x.shape)

    c_end = lax.broadcasted_iota(jnp.int32, (1, ncp), 1) * CMP_STRIDE + (CMP_BLOCK - 1)
    lg = per_head(_dot(q, ck_ref[0, 0]), jnp.where(c_end <= row_t, 0.0, NEG))
    e = jnp.exp2(lg - jnp.max(lg, axis=-1, keepdims=True))
    pc = e * jnp.where(row >= CMP_BLOCK - 1, 1.0 / jnp.sum(e, axis=-1, keepdims=True), 0.0)
    o_cmp = _dot_nt(pc.astype(BF16), cv_ref[0, 0])
    psum = pc[0:tq]
    for r in range(1, NSA_GROUP):
        psum = psum + pc[r * tq:(r + 1) * tq]
    p_hi = psum.astype(BF16)
    p_lo = (psum - p_hi.astype(F32)).astype(BF16)
    imp_t = _dot_nt(ovt_ref[...], p_hi) + _dot_nt(ovt_ref[...], p_lo)

    wl = WINDOW + tq
    w0 = pl.multiple_of(jnp.maximum(q0 - WINDOW, 0), tq)
    dist_w = row_t - (w0 + lax.broadcasted_iota(jnp.int32, (1, wl), 1))
    neg_w = jnp.where(dist_w >= 0, jnp.where(dist_w < WINDOW, 0.0, NEG), NEG)
    lg = per_head(_dot(q, kw_ref[0, 0, :, pl.ds(w0, wl)]), neg_w)
    p = jnp.exp2(lg - jnp.max(lg, axis=-1, keepdims=True)).astype(BF16)
    ow = _dot_nt(p, vw_ref[0, 0, :, pl.ds(w0, wl)])
    o_win = ow[:, :NSA_HD] / ow[:, NSA_HD:NSA_HD + 1]
    gates = gate_ref[0]
    gcol = lambda j: jnp.concatenate([gates[:, 3 * r + j:3 * r + j + 1] for r in range(NSA_GROUP)], axis=0)
    o_two = gcol(0) * o_cmp + gcol(2) * o_win

    blk = lax.broadcasted_iota(jnp.int32, (nsp, 1), 0)
    t_lane = q0 + lax.broadcasted_iota(jnp.int32, (1, tq), 1)
    cur = jnp.right_shift(t_lane, 6)
    forced = jnp.where(blk == 0, 1.0, 0.0) + jnp.where(blk == cur, 1.0, 0.0) + jnp.where(blk == cur - 1, 1.0, 0.0)
    valid = blk * SEL_BLOCK <= t_lane
    score = jnp.where(valid, imp_t + jnp.where(forced > 0.5, FORCE_BONUS, 0.0), NEG)
    sel_t = _top_k_mask(score, blk.astype(F32), SEL_TOPK, axis=0)
    sel_t = jnp.where(valid, sel_t, 0.0)
    sel_bias = ((sel_t - 1.0) * 1e30).T.astype(BF16)
    qa = jnp.concatenate([jnp.concatenate([sel_bias] * NSA_GROUP, axis=0), q], axis=1)

    n_full = q0 // tk
    per_tile = tk // SEL_BLOCK
    n_tiles = nsp // per_tile
    blk_any = jnp.max(sel_t, axis=1, keepdims=True)
    tile_any = jnp.max(blk_any.reshape(n_tiles, per_tile, 1), axis=1)
    bit = jnp.left_shift(1, lax.broadcasted_iota(jnp.int32, (n_tiles, 1), 0))
    tile_bits = jnp.sum(jnp.where(tile_any > 0.0, bit, 0))
    n_used = jnp.int32(0)
    for kt in range(min(n_tiles, ks_ref.shape[3] // tk)):
        use = jnp.logical_and(jnp.bitwise_and(jnp.right_shift(tile_bits, kt), 1) == 1, kt < n_full)
        ids_scr[n_used] = kt
        n_used = n_used + use.astype(jnp.int32)

    def sel_tile(kt, carry, diag):
        m, acc = carry
        k0 = pl.multiple_of(kt * tk, tk)
        lg = _dot(qa, ks_ref[0, 0, :, pl.ds(k0, tk)])
        if diag:
            lg = jnp.where(k0 + lax.broadcasted_iota(jnp.int32, (1, tk), 1) <= row, lg, NEG)
        m_new = jnp.maximum(m, jnp.max(lg, axis=-1, keepdims=True))
        p = jnp.exp2(lg - m_new).astype(BF16)
        return m_new, jnp.exp2(m - m_new) * acc + _dot_nt(p, vs_ref[0, 0, :, pl.ds(k0, tk)])

    carry = sel_tile(n_full, (jnp.full((nr, 1), NEG, F32), jnp.zeros((nr, LANES), F32)), True)

    def two_tiles(j, c):
        return sel_tile(ids_scr[2 * j + 1], sel_tile(ids_scr[2 * j], c, False), False)

    carry = lax.fori_loop(0, n_used // 2, two_tiles, carry)
    _, acc = lax.cond(n_used % 2 == 1, lambda c: sel_tile(ids_scr[n_used - 1], c, False), lambda c: c, carry)
    o_slc = acc[:, :NSA_HD] / acc[:, NSA_HD:NSA_HD + 1]
    o = o_two + gcol(1) * o_slc
    o_ref[0] = jnp.concatenate([o[r * tq:(r + 1) * tq] for r in range(NSA_GROUP)], axis=1)


def _overlap_matrix(ncp, nsp):
    c = np.arange(ncp)[:, None] * CMP_STRIDE
    s = np.arange(nsp)[None, :] * SEL_BLOCK
    return jnp.asarray((c <= s + SEL_BLOCK - 1) & (c + CMP_BLOCK - 1 >= s), dtype=BF16)


def _bf16_round(x):
    return np.asarray(x, dtype=BF16).astype(np.float32)


def _q_bias():
    slopes = np.exp2(-8.0 * np.arange(1, NSA_HEADS + 1) / NSA_HEADS)
    c_hi = float(_bf16_round(LOG2E))
    c_lo = float(_bf16_round(LOG2E - c_hi))
    qb = np.zeros((NSA_HEADS, LANES), np.float32)
    qb[:, NSA_HD:NSA_HD + 4] = np.stack([c_hi * slopes * SEL_BLOCK, c_lo * slopes * SEL_BLOCK,
                                         c_hi * slopes, c_lo * slopes], axis=1)
    return jnp.asarray(qb.reshape(1, NSA_HEADS * LANES))


def _alibi_k_rows(pos):
    pos = np.asarray(pos)
    return jnp.asarray(np.stack([pos // SEL_BLOCK, pos // SEL_BLOCK, pos % SEL_BLOCK, pos % SEL_BLOCK]), BF16)


def _aug_rows(parts, total, lead):
    parts = [jnp.broadcast_to(p, lead + p.shape[-2:]) for p in parts]
    used = sum(p.shape[-2] for p in parts)
    if total > used:
        parts.append(jnp.zeros(lead + (total - used, parts[0].shape[-1]), BF16))
    return jnp.concatenate(parts, axis=-2)


def _nsa_prompt(q_pad, gates, ck, cv, ks, vs, kw, vw, tq, tk):
    b, t, _ = q_pad.shape
    g, r = NSA_KV_HEADS, NSA_GROUP
    ncp = ck.shape[3]
    nsp = max(LANES, t // SEL_BLOCK)
    kv = lambda a: pl.BlockSpec((1, 1) + a.shape[2:], lambda bi, gi, i: (bi, gi, 0, 0))
    ovt = _overlap_matrix(ncp, nsp).T
    return pl.pallas_call(
        functools.partial(_nsa_prompt_kernel, tq=tq, tk=tk, ncp=ncp, nsp=nsp),
        grid=(b, g, t // tq),
        in_specs=[pl.BlockSpec((1, tq, r * LANES), lambda bi, gi, i: (bi, i, gi)),
                  pl.BlockSpec((1, tq, LANES), lambda bi, gi, i: (bi, i, gi)),
                  kv(ck), kv(cv), kv(ks), kv(vs), kv(kw), kv(vw),
                  pl.BlockSpec((nsp, ncp), lambda bi, gi, i: (0, 0))],
        out_specs=pl.BlockSpec((1, tq, r * NSA_HD), lambda bi, gi, i: (bi, i, gi)),
        out_shape=jax.ShapeDtypeStruct((b, t, NSA_WIDTH), F32),
        scratch_shapes=[pltpu.SMEM((nsp // (tk // SEL_BLOCK),), jnp.int32)],
        compiler_params=_cparams(("parallel", "parallel", "arbitrary")),
        name="nsa_prompt",
    )(q_pad, gates, ck, cv, ks, vs, kw, vw, ovt)


def _gla_prompt_kernel(q_ref, k_ref, la_ref, v_ref, o_ref, s_ref, s_scr, *, nch):
    c = GLA_CHUNK
    ti = pl.program_id(2)

    @pl.when(ti == 0)
    def _():
        s_scr[...] = jnp.zeros_like(s_scr)

    two_dk, two_dv = 2 * GLA_DK, 2 * GLA_DV
    r_i = lax.broadcasted_iota(jnp.int32, (c, c), 0)
    c_i = lax.broadcasted_iota(jnp.int32, (c, c), 1)
    tril = jnp.where(c_i <= r_i, 1.0, 0.0).astype(BF16)
    e_r = lax.broadcasted_iota(jnp.int32, (two_dk, two_dk), 0)
    e_c = lax.broadcasted_iota(jnp.int32, (two_dk, two_dk), 1)
    eye = jnp.where(e_r == e_c, 1.0, 0.0).astype(BF16)
    b_r = lax.broadcasted_iota(jnp.int32, (two_dk, two_dv), 0) // GLA_DK
    b_c = lax.broadcasted_iota(jnp.int32, (two_dk, two_dv), 1) // GLA_DV
    same_head = jnp.where(b_r == b_c, 1.0, 0.0)
    ones_bd = same_head.astype(BF16)
    head0 = lax.broadcasted_iota(jnp.int32, (1, two_dk), 1) < GLA_DK
    pos = lax.broadcasted_iota(jnp.int32, (c, 1), 0)
    nsub = c // GLA_SUB
    w_i = lax.broadcasted_iota(jnp.int32, (1, GLA_SUB, 1), 1)

    def chunk(ci):
        rows = pl.ds(pl.multiple_of(ci * c, c), c)
        q = q_ref[0, rows, :]
        k = k_ref[0, rows, :]
        la = la_ref[0, rows, :]
        v = v_ref[0, rows, :]
        vb = v.astype(BF16)
        s0 = s_scr[...]
        a1, a2, a3 = _split3(la)
        cum = _dot(tril, a1) + _dot(tril, a2) + _dot(tril, a3)
        last = cum[c - 1:c, :]
        o = _dot((q * jnp.exp(cum)).astype(BF16), s0.astype(BF16))
        attn = jnp.zeros((2 * c, c), F32)
        for j in range(nsub - 1):
            ce = cum[GLA_SUB * (j + 1) - 1:GLA_SUB * (j + 1), :]
            qh = jnp.where(pos >= GLA_SUB * (j + 1), q * jnp.exp(jnp.minimum(cum - ce, 0.0)), 0.0)
            in_j = jnp.where(pos >= GLA_SUB * j, jnp.where(pos < GLA_SUB * (j + 1), 1.0, 0.0), 0.0)
            kh = in_j * (k * jnp.exp(jnp.minimum(ce - cum, 0.0)))
            qh2 = jnp.concatenate([jnp.where(head0, qh, 0.0), jnp.where(head0, 0.0, qh)], axis=0)
            attn = attn + _dot_nt(qh2.astype(BF16), kh.astype(BF16))
        o_off = _dot(attn.astype(BF16), vb)
        o = o + jnp.concatenate([o_off[:c, :GLA_DV], o_off[c:, GLA_DV:]], axis=1)
        q3 = q.reshape(nsub, GLA_SUB, two_dk)
        k3 = k.reshape(nsub, GLA_SUB, two_dk)
        c3 = cum.reshape(nsub, GLA_SUB, two_dk)
        v3 = v.reshape(nsub, GLA_SUB, two_dv)
        ws = []
        for u in range(GLA_SUB):
            w = q3 * k3[:, u:u + 1, :] * jnp.exp(jnp.minimum(c3 - c3[:, u:u + 1, :], 0.0))
            ws.append(jnp.where(w_i >= u, w, 0.0).reshape(c, two_dk).astype(BF16))
        a_rep = _dot(jnp.concatenate(ws, axis=0), ones_bd)
        od = jnp.zeros((nsub, GLA_SUB, two_dv), F32)
        for u in range(GLA_SUB):
            od = od + a_rep[u * c:(u + 1) * c].reshape(nsub, GLA_SUB, two_dv) * v3[:, u:u + 1, :]
        o_ref[0, rows, :] = o + od.reshape(c, two_dv)
        kd = (k * jnp.exp(last - cum)).astype(BF16)
        kd_t = _dot_nt(eye, kd).astype(BF16)
        l1, l2, l3 = _split3(jnp.broadcast_to(last, (8, two_dk)))
        last_col = (_dot_nt(eye, l1) + _dot_nt(eye, l2) + _dot_nt(eye, l3))[:, 0:1]
        s_scr[...] = jnp.exp(last_col) * s0 + same_head * _dot(kd_t, vb)

    def several(i, _):
        for j in range(GLA_UNROLL):
            chunk(GLA_UNROLL * i + j)
        return 0

    lax.fori_loop(0, nch // GLA_UNROLL, several, 0)

    @pl.when(ti == pl.num_programs(2) - 1)
    def _():
        s = s_scr[...]
        s_ref[0, 0] = s[:GLA_DK, :GLA_DV]
        s_ref[0, 1] = s[GLA_DK:, GLA_DV:]


def _gla_prompt(q, k, la, v, nch):
    b, t, _ = q.shape
    tc = nch * GLA_CHUNK
    hp = GLA_HEADS // 2
    qs = pl.BlockSpec((1, tc, 2 * GLA_DK), lambda bi, pi, ti: (bi, ti, pi))
    vs = pl.BlockSpec((1, tc, 2 * GLA_DV), lambda bi, pi, ti: (bi, ti, pi))
    return pl.pallas_call(
        functools.partial(_gla_prompt_kernel, nch=nch),
        grid=(b, hp, t // tc),
        in_specs=[qs, qs, qs, vs],
        out_specs=[vs, pl.BlockSpec((1, 2, GLA_DK, GLA_DV), lambda bi, pi, ti: (bi, pi, 0, 0))],
        out_shape=[jax.ShapeDtypeStruct((b, t, GLA_WIDTH), F32),
                   jax.ShapeDtypeStruct((b, GLA_HEADS, GLA_DK, GLA_DV), F32)],
        scratch_shapes=[pltpu.VMEM((2 * GLA_DK, 2 * GLA_DV), F32)],
        compiler_params=_cparams(("parallel", "parallel", "arbitrary")),
        name="gla_prompt",
    )(q, k, la, v)


def _out_kernel(x_ref, gt_ref, on_ref, ogla_ref, og_ref, gn_ref, w_ref, o_ref):
    y = _dot(on_ref[0].astype(BF16), w_ref[:NSA_WIDTH, :])
    og = og_ref[0]
    ogla = ogla_ref[0]
    heads = []
    for h in range(GLA_HEADS):
        o = ogla[:, h * GLA_DV:(h + 1) * GLA_DV]
        o = o * lax.rsqrt(jnp.mean(o * o, axis=-1, keepdims=True) + EPS) * gn_ref[...]
        gate = og[:, h * GLA_DV:(h + 1) * GLA_DV]
        heads.append((o * (gate * jax.nn.sigmoid(gate))).astype(BF16))
    y = y + _dot(jnp.concatenate(heads, axis=1), w_ref[NSA_WIDTH:, :])
    o_ref[0] = x_ref[0] + gt_ref[0] * y


def _out_proj(x, gt, o_nsa, o_gla, o_g, gla_norm, w_out, tm):
    bx, tx, d = x.shape
    xs = pl.BlockSpec((1, tm, d), lambda b, i: (b, i, 0))
    return pl.pallas_call(
        _out_kernel,
        grid=(bx, tx // tm),
        in_specs=[xs, _mod_spec(gt, tm, 2),
                  pl.BlockSpec((1, tm, NSA_WIDTH), lambda b, i: (b, i, 0)),
                  pl.BlockSpec((1, tm, GLA_WIDTH), lambda b, i: (b, i, 0)),
                  pl.BlockSpec((1, tm, GLA_WIDTH), lambda b, i: (b, i, 0)),
                  pl.BlockSpec((1, GLA_DV), lambda b, i: (0, 0)),
                  pl.BlockSpec(w_out.shape, lambda b, i: (0, 0))],
        out_specs=xs,
        out_shape=jax.ShapeDtypeStruct(x.shape, F32),
        compiler_params=_cparams(("parallel", "parallel")),
        name="out_proj",
    )(x, gt, o_nsa, o_gla, o_g, gla_norm, w_out)


def _nsa_sample_kernel(pt_ref, *rest, n_pages, page, past, wlen, nseq):
    del pt_ref
    per = 6 + 2 * n_pages
    (w1_ref, pe_ref, b1_ref, w2_ref, ovt_ref, o_ref, x_scr, bias_scr) = rest[nseq * per:]

    @pl.when(pl.program_id(0) == 0)
    def _():
        for kv in range(2):
            acc = jnp.zeros((8, 2 * LANES), F32)
            for s2 in range(CMP_BLOCK // 2):
                pe2 = jnp.broadcast_to(pe_ref[kv, s2:s2 + 1, :], (8, 2 * LANES))
                acc = acc + _dot(pe2.astype(BF16), w1_ref[kv, s2])
            bias_scr[kv] = acc + b1_ref[kv]

    ncb = past // CMP_STRIDE
    nrow = ncb + 8
    cpp = page // CMP_STRIDE
    r_i = lax.broadcasted_iota(jnp.int32, (page, page), 0)
    p_i = lax.broadcasted_iota(jnp.int32, (page, page), 1)
    in_chunk = jnp.bitwise_and(p_i, CMP_STRIDE - 1)
    perm_t = jnp.where(r_i * CMP_STRIDE == in_chunk * (cpp * CMP_STRIDE) + (p_i - in_chunk), 1.0, 0.0).astype(BF16)
    first = lax.broadcasted_iota(jnp.int32, (8, 1), 0) == 0
    seqs = [rest[s * per:(s + 1) * per] for s in range(nseq)]
    for s, refs in enumerate(seqs):
        base = s * nrow
        for kv in range(2):
            for pg in range(n_pages):
                rows = _dot_nt(perm_t, refs[5 + pg][0, 0, kv].reshape(LANES, page).astype(BF16))
                for off in range(CMP_STRIDE):
                    x_scr[kv, off, base + pg * cpp:base + (pg + 1) * cpp, :] = rows[off * cpp:(off + 1) * cpp, :]
            new = refs[2][0][:, kv * LANES:(kv + 1) * LANES]
            for off in range(CMP_STRIDE):
                x_scr[kv, off, base + ncb:base + nrow, :] = (jnp.where(first, new, 0.0) if off == 0
                                                             else jnp.zeros((8, LANES), F32))
    ckvs = [[None, None] for _ in range(nseq)]
    half = CMP_STRIDE // 2
    for kv in range(2):
        acc_lo = jnp.zeros((nseq * nrow, 2 * LANES), F32)
        acc_hi = jnp.zeros((nseq * nrow, 2 * LANES), F32)
        for j in range(half):
            lhs = jnp.concatenate([x_scr[kv, 2 * j], x_scr[kv, 2 * j + 1]], axis=1).astype(BF16)
            acc_lo = acc_lo + _dot(lhs, w1_ref[kv, j])
            acc_hi = acc_hi + _dot(lhs, w1_ref[kv, half + j])
        hs = [jax.nn.gelu(acc_lo[s * nrow:s * nrow + ncb]
                          + pltpu.roll(acc_hi[s * nrow:(s + 1) * nrow], nrow - 1, 0)[:ncb] + bias_scr[kv][0:1])
              for s in range(nseq)]
        ckv = _dot(jnp.concatenate(hs, axis=0).astype(BF16), w2_ref[kv]).astype(BF16)
        for s in range(nseq):
            ckvs[s][kv] = ckv[s * ncb:(s + 1) * ncb]

    for s, refs in enumerate(seqs):
        _nsa_sample_seq(refs[:5], ckvs[s], refs[5 + n_pages:5 + 2 * n_pages], refs[5 + 2 * n_pages],
                        ovt_ref, o_ref.at[s], n_pages=n_pages, page=page, past=past, wlen=wlen)


def _nsa_sample_seq(row_refs, ckv, slc_pages, cw_ref, ovt_ref, o_ref, *, n_pages, page, past, wlen):
    qall_ref, gate_ref, _, kvs_ref, kvw_ref = row_refs
    nh = NSA_HEADS
    qa = qall_ref[0]
    qa_f = qa.astype(F32)
    hrow = lax.broadcasted_iota(jnp.int32, (nh, 1), 0)
    slope = jnp.exp2(-(hrow + 1).astype(F32)) * LOG2E
    lane = lax.broadcasted_iota(jnp.int32, (1, LANES), 1)
    c_end = lane * CMP_STRIDE + (CMP_BLOCK - 1)
    dist_c = past - c_end
    mask_c = dist_c >= 0
    lg_c = jnp.where(mask_c, _dot_nt(qa, ckv[0]) - slope * dist_c.astype(F32), NEG)
    e_c = jnp.where(mask_c, jnp.exp2(lg_c - jnp.max(lg_c, axis=-1, keepdims=True)), 0.0)
    pc = e_c / jnp.maximum(jnp.sum(e_c, axis=-1, keepdims=True), 1e-30)
    o_cmp = _dot(pc.astype(BF16), ckv[1])

    ps0 = jnp.sum(jnp.where(hrow < NSA_GROUP, pc, 0.0), axis=0, keepdims=True)
    ps1 = jnp.sum(jnp.where(hrow >= NSA_GROUP, pc, 0.0), axis=0, keepdims=True)
    psum = jnp.where(hrow == 0, ps0, jnp.where(hrow == 1, ps1, 0.0))
    psum = jnp.concatenate([psum, jnp.zeros((LANES - nh, LANES), F32)], axis=0)
    p_hi = psum.astype(BF16)
    p_lo = (psum - p_hi.astype(F32)).astype(BF16)
    imp_t = _dot_nt(ovt_ref[...], p_hi) + _dot_nt(ovt_ref[...], p_lo)
    blk = lax.broadcasted_iota(jnp.int32, (LANES, 1), 0)
    cur = past // SEL_BLOCK
    forced = jnp.where(blk == 0, 1.0, 0.0) + jnp.where(blk == cur, 1.0, 0.0) + jnp.where(blk == cur - 1, 1.0, 0.0)
    score = jnp.where(blk * SEL_BLOCK <= past, imp_t + jnp.where(forced > 0.5, FORCE_BONUS, 0.0), NEG)
    sel = _top_k_mask(score, blk.astype(F32), SEL_TOPK, axis=0).T
    sel_h = jnp.where(hrow < NSA_GROUP, sel[0:1], sel[1:2])

    def attend(tiles, new_row, new_neg):
        lgs = [_dot(qa, k_t) + neg for k_t, _, neg in tiles]
        lg_new = jnp.sum(qa_f * new_row[:, :LANES], axis=-1, keepdims=True) + new_neg
        m = lg_new
        for lg in lgs:
            m = jnp.maximum(m, jnp.max(lg, axis=-1, keepdims=True))
        p_new = jnp.exp2(lg_new - m)
        l = p_new
        acc = p_new * new_row[:, LANES:]
        for lg, (_, v_t, _) in zip(lgs, tiles):
            p = jnp.exp2(lg - m)
            l = l + jnp.sum(p, axis=-1, keepdims=True)
            acc = acc + _dot_nt(p.astype(BF16), v_t)
        return acc / l

    per = page // SEL_BLOCK
    tiles = []
    for pg in range(n_pages):
        blkp = slc_pages[pg]
        dist = past - (pg * page + lane)
        selx = jnp.zeros((nh, LANES), F32)
        for u in range(per):
            in_u = jnp.where(lane >= u * SEL_BLOCK, jnp.where(lane < (u + 1) * SEL_BLOCK, 1.0, 0.0), 0.0)
            selx = selx + in_u * sel_h[:, pg * per + u:pg * per + u + 1]
        neg = jnp.where(dist >= 0, (selx - 1.0) * 1e30, NEG) - slope * dist.astype(F32)
        tiles.append((blkp[0, 0, 0].reshape(LANES, page).astype(BF16),
                      blkp[0, 0, 1].reshape(LANES, page).astype(BF16), neg))
    o_slc = attend(tiles, kvs_ref[0], (sel_h[:, cur:cur + 1] - 1.0) * 1e30)

    wlane = lax.broadcasted_iota(jnp.int32, (1, wlen), 1)
    w_pos = past - wlen + wlane
    dist = past - w_pos
    ok = jnp.where(dist >= 0, jnp.where(dist < WINDOW, jnp.where(w_pos >= 0, 1.0, 0.0), 0.0), 0.0)
    neg = (ok - 1.0) * 1e30 - slope * dist.astype(F32)
    tiles = [(cw_ref[0, 0, 0].reshape(LANES, wlen).astype(BF16), cw_ref[0, 0, 1].reshape(LANES, wlen).astype(BF16), neg)]
    o_win = attend(tiles, kvw_ref[0], jnp.zeros((nh, 1), F32))

    gates = gate_ref[0]
    pick = lambda a: jnp.where(hrow < NSA_GROUP, a[:, :NSA_HD], a[:, NSA_HD:])
    o_ref[...] = gates[:, 0:1] * pick(o_cmp) + gates[:, 1:2] * pick(o_slc) + gates[:, 2:3] * pick(o_win)


def _nsa_sample(layer, page_table, q_all, gates, kvc, kvs, kvw, cache_c, cache_s, cache_w, cw):
    nb, n_pages = page_table.shape
    depth, n_phys, page = cache_c.shape[:3]
    past = n_pages * page
    wlen = cache_w.shape[2]
    ncb = past // CMP_STRIDE
    cc = cache_c.transpose(0, 1, 3, 4, 5, 2)
    cs = cache_s.transpose(0, 1, 3, 4, 5, 2)
    cwin = cache_w.transpose(0, 1, 3, 4, 5, 2)
    nseq = NSA_SAMPLE_SEQS if nb % NSA_SAMPLE_SEQS == 0 else 1
    full = lambda a: pl.BlockSpec(a.shape, lambda b, pt: (0,) * a.ndim)
    ovt = _overlap_matrix(ncb, LANES).T
    wargs = (cw["w1s"], cw["pes"], cw["b1s"], cw["w2s"], ovt)
    in_specs, args = [], []
    for s in range(nseq):
        seq = lambda b, s=s: nseq * b + s
        heads = pl.BlockSpec((1, NSA_HEADS, LANES), lambda b, pt, seq=seq: (seq(b), 0, 0))
        row = pl.BlockSpec((1, 1, KV_COLS), lambda b, pt, seq=seq: (seq(b), 0, 0))
        pages = [pl.BlockSpec((1, 1, 2, NSA_KV_HEADS, NSA_HD, page),
                              lambda b, pt, seq=seq, j=j: (layer, pt[seq(b), j], 0, 0, 0, 0)) for j in range(n_pages)]
        win = pl.BlockSpec((1, 1, 2, NSA_KV_HEADS, NSA_HD, wlen), lambda b, pt, seq=seq: (layer, seq(b), 0, 0, 0, 0))
        in_specs += [heads, heads, row, row, row] + pages + pages + [win]
        args += [q_all, gates, kvc, kvs, kvw] + [cc] * n_pages + [cs] * n_pages + [cwin]
    grid_spec = pltpu.PrefetchScalarGridSpec(
        num_scalar_prefetch=1,
        grid=(nb // nseq,),
        in_specs=in_specs + [full(a) for a in wargs],
        out_specs=pl.BlockSpec((nseq, NSA_HEADS, NSA_HD), lambda b, pt: (b, 0, 0)),
        scratch_shapes=[pltpu.VMEM((2, CMP_STRIDE, nseq * (ncb + 8), LANES), F32), pltpu.VMEM((2, 8, 2 * LANES), F32)],
    )
    return pl.pallas_call(
        functools.partial(_nsa_sample_kernel, n_pages=n_pages, page=page, past=past, wlen=wlen, nseq=nseq),
        grid_spec=grid_spec,
        out_shape=jax.ShapeDtypeStruct((nb, NSA_HEADS, NSA_HD), F32),
        compiler_params=_cparams(("arbitrary",)),
        name="nsa_sample",
    )(page_table, *args, *wargs)


def _gla_sample_kernel(col_ref, v_ref, s_ref, o_ref, sn_ref, *, bb):
    for b in range(bb):
        cols = col_ref[b]
        for h in range(GLA_HEADS):
            a = jnp.exp(cols[:, h:h + 1])
            k = cols[:, GLA_HEADS + h:GLA_HEADS + h + 1]
            q = cols[:, 2 * GLA_HEADS + h:2 * GLA_HEADS + h + 1]
            v = v_ref[b, h:h + 1, :]
            s0 = s_ref[b, h]
            qk = jnp.sum(q * k, axis=0, keepdims=True)
            o_ref[b, h:h + 1, :] = jnp.sum((q * a) * s0, axis=0, keepdims=True) + qk * v
            sn_ref[b, h] = a * s0 + k * v


def _gla_sample(cols, v, state, bb):
    nb = v.shape[0]
    return pl.pallas_call(
        functools.partial(_gla_sample_kernel, bb=bb),
        grid=(nb // bb,),
        in_specs=[pl.BlockSpec((bb, GLA_DK, 16), lambda i: (i, 0, 0)),
                  pl.BlockSpec((bb, GLA_HEADS, GLA_DV), lambda i: (i, 0, 0)),
                  pl.BlockSpec((bb, GLA_HEADS, GLA_DK, GLA_DV), lambda i: (i, 0, 0, 0))],
        out_specs=[pl.BlockSpec((bb, GLA_HEADS, GLA_DV), lambda i: (i, 0, 0)),
                   pl.BlockSpec((bb, GLA_HEADS, GLA_DK, GLA_DV), lambda i: (i, 0, 0, 0))],
        out_shape=[jax.ShapeDtypeStruct((nb, GLA_HEADS, GLA_DV), F32),
                   jax.ShapeDtypeStruct((nb, GLA_HEADS, GLA_DK, GLA_DV), F32)],
        compiler_params=_cparams(("parallel",)),
        name="gla_sample",
    )(cols, v, state)


def _proj_weight(w):
    d = w.shape[0]
    offs = np.cumsum((0,) + IN_SIZES)
    seg = lambda i: w[:, offs[i]:offs[i + 1]]

    def slots(a, n):
        per = a.shape[1] // n
        return jnp.pad(a.reshape(d, n, per), ((0, 0), (0, 0), (0, LANES - per))).reshape(d, n * LANES)

    return jnp.concatenate([slots(seg(0), NSA_HEADS), seg(1), seg(2), seg(3), slots(seg(4), NSA_KV_HEADS),
                            seg(5), seg(6), seg(7), slots(seg(8), 1), seg(9)], axis=1)


def _cmp_weights(pe, w1, b1, w2):
    g = NSA_KV_HEADS
    eye = jnp.eye(2, dtype=F32)
    eg = jnp.eye(g, dtype=F32)
    w1r = w1.reshape(2, 2, CMP_STRIDE, NSA_HD, CMP_HIDDEN)
    big = jnp.einsum("krsdh,ka,gb->rskgdabh", w1r, eye, eg).reshape(2, CMP_STRIDE * KV_COLS, 2 * g * CMP_HIDDEN)
    w2big = jnp.einsum("khd,ka,gb->kghabd", w2, eye, eg).reshape(2 * g * CMP_HIDDEN, KV_COLS)
    per = pe.reshape(2, 2, CMP_STRIDE, NSA_HD)
    pet = jnp.broadcast_to(per.transpose(1, 2, 0, 3)[:, :, :, None, :], (2, CMP_STRIDE, 2, g, NSA_HD))
    pet = pet.reshape(2, 1, CMP_STRIDE * KV_COLS)
    b1big = jnp.broadcast_to(b1[:, None, :], (2, g, CMP_HIDDEN)).reshape(1, 2 * g * CMP_HIDDEN)
    w1s = jnp.einsum("ksdh,gb->ksgdbh", w1.reshape(2, CMP_BLOCK, NSA_HD, CMP_HIDDEN), eg)
    w1s = w1s.reshape(2, CMP_BLOCK // 2, 2 * g * NSA_HD, g * CMP_HIDDEN)
    w2s = jnp.einsum("khd,gb->kghbd", w2, eg).reshape(2, g * CMP_HIDDEN, g * NSA_HD)
    pes = jnp.broadcast_to(pe[:, :, None, :], (2, CMP_BLOCK, g, NSA_HD)).reshape(2, CMP_BLOCK // 2, 2 * g * NSA_HD)
    b1s = jnp.broadcast_to(b1[:, None, None, :], (2, 1, g, CMP_HIDDEN)).reshape(2, 1, g * CMP_HIDDEN)
    return {
        "w1s": w1s.astype(BF16), "w2s": w2s.astype(BF16), "pes": pes, "b1s": b1s,
        "w1lo_t": big[0].T.astype(BF16), "w1hi_t": big[1].T.astype(BF16),
        "pe_lo": pet[0], "pe_hi": pet[1], "b1_col": b1big.T, "w2_t": w2big.T.astype(BF16),
    }


def _mods(mod, rows):
    m = mod.reshape(mod.shape[0], N_MOD, D_MODEL)
    if rows == 1:
        return [m[:, j][:, None, :] for j in range(N_MOD)]
    return [m[:, j][None, :, :] for j in range(N_MOD)]


def _layer_prompt(x, mods, lw, tm, tq, tk, final_gain):
    b, t, _ = x.shape
    x = _ffn(x, mods[0], mods[1], mods[2], lw["norm_g"][0:1], lw["f1_in"], lw["f1_out"], tm)
    (qn, kvc, _, _, kvct, kvst, kvwt, kvtb, gn, qg, kg, vg, la, og) = _proj(
        x, mods[3], mods[4], lw["norm_g"][1:2], lw["w_all"], lw["wkvt"], lw["wa2p"], lw["ba"], lw["qb"], tm)
    g = NSA_KV_HEADS
    ncp = t // CMP_STRIDE
    ckvt = _cmp_prompt(kvc.reshape(b, ncp, CMP_STRIDE * KV_COLS), lw["cmp"]).astype(BF16)
    ckvt = ckvt.reshape(b, 2, g, NSA_HD, ncp)
    kvs5 = kvtb[:, :KV_COLS].reshape(b, 2, g, NSA_HD, t)
    kvw5 = kvtb[:, KV_COLS:].reshape(b, 2, g, NSA_HD, t)
    pos_rows = _alibi_k_rows(np.arange(t))
    ones_row = jnp.ones((1, t), BF16)
    onehot = jnp.asarray(np.arange(max(LANES, t // SEL_BLOCK))[:, None] == np.arange(t)[None, :] // SEL_BLOCK, BF16)
    lead = (b, g)
    ck = _aug_rows([ckvt[:, 0], _alibi_k_rows(np.arange(ncp) * CMP_STRIDE + CMP_BLOCK - 1)], LANES, lead)
    ks = _aug_rows([onehot, kvs5[:, 0], pos_rows], onehot.shape[0] + LANES, lead)
    vs = _aug_rows([kvs5[:, 1], ones_row], LANES, lead)
    kw = _aug_rows([kvw5[:, 0], pos_rows], LANES, lead)
    vw = _aug_rows([kvw5[:, 1], ones_row], LANES, lead)
    o_nsa = _nsa_prompt(qn, gn, ck, ckvt[:, 1], ks, vs, kw, vw, tq, tk)
    o_gla, s_fin = _gla_prompt(qg, kg, la, vg, min(8, t // GLA_CHUNK))
    x = _out_proj(x, mods[5], o_nsa, o_gla, og, lw["gla_norm"], lw["w_out"], tm)
    x = _ffn(x, mods[6], mods[7], mods[8], lw["norm_g"][2:3], lw["f2_in"], lw["f2_out"], tm, final_gain)
    win = min(WINDOW, t)
    back = lambda a: a.reshape(b, 2, NSA_KV_HEADS, NSA_HD, a.shape[-1]).transpose(0, 4, 1, 2, 3)
    return x, (back(kvct), back(kvst), back(kvwt[:, :, t - win:]), s_fin)


def _layer_sample(layer, x, mods, lw, cache_c, cache_s, cache_w, state, page_table, final_gain):
    nb = x.shape[1]
    x = _ffn(x, mods[0], mods[1], mods[2], lw["norm_g"][0:1], lw["f1_in"], lw["f1_out"], nb)
    (qn, kvc, kvs, kvw, kvct, kvst, kvwt, _, gn, qg, kg, vg, la, og) = _proj(
        x, mods[3], mods[4], lw["norm_g"][1:2], lw["w_all"], lw["wkvt"], lw["wa2p"], lw["ba"], lw["qb"], nb)
    q4 = qn[0].reshape(nb, NSA_KV_HEADS, NSA_GROUP, LANES)[..., :NSA_HD]
    q_all = (q4[:, :, :, None, :] * jnp.eye(NSA_KV_HEADS, dtype=BF16)[None, :, None, :, None]).reshape(nb, NSA_HEADS, LANES)
    gates = gn[0].reshape(nb, NSA_KV_HEADS, LANES)[..., :3 * NSA_GROUP].reshape(nb, NSA_HEADS, 3)
    gates = jnp.pad(gates, ((0, 0), (0, 0), (0, LANES - 3)))
    r3 = lambda a: a[0][:, None, :]
    o_nsa = _nsa_sample(layer, page_table, q_all, gates, r3(kvc), r3(kvs), r3(kvw),
                        cache_c, cache_s, cache_w, lw["cmp"])
    o_nsa = o_nsa.reshape(1, nb, NSA_WIDTH)
    col = lambda a: a[0].reshape(nb, GLA_HEADS, GLA_DK).transpose(0, 2, 1)
    cols = jnp.concatenate([col(la), col(kg), col(qg), jnp.zeros((nb, GLA_DK, 16 - 3 * GLA_HEADS), F32)], axis=-1)
    o_gla, s_new = _gla_sample(cols, vg[0].reshape(nb, GLA_HEADS, GLA_DV), state, 8)
    o_gla = o_gla.reshape(1, nb, GLA_WIDTH)
    x = _out_proj(x, mods[5], o_nsa, o_gla, og, lw["gla_norm"], lw["w_out"], nb)
    x = _ffn(x, mods[6], mods[7], mods[8], lw["norm_g"][2:3], lw["f2_in"], lw["f2_out"], nb, final_gain)
    kv5 = lambda a: a[0].reshape(2, NSA_KV_HEADS, NSA_HD, nb).transpose(3, 0, 1, 2)[:, None]
    return x, (kv5(kvct), kv5(kvst), kv5(kvwt), s_new)


def kernel(x_prompt, x_sample, c_prompt, c_sample, cache_kv_cmp, cache_kv_slc, cache_kv_win, state_gla, page_table, w_ada, b_ada, norm_g, ffn1_w_in, ffn1_w_out, w_in, cmp_pe, cmp_w1, cmp_b1, cmp_w2, gla_wa2, gla_ba, gla_norm, w_out, ffn2_w_in, ffn2_w_out, final_norm):
    depth = w_ada.shape[0]
    b, t, d = x_prompt.shape
    nb = x_sample.shape[0]
    bp = -(-b // 8) * 8
    c_all = jnp.concatenate([c_prompt, jnp.zeros((bp - b, d), F32), c_sample], axis=0)
    mod_all = _ada(c_all, w_ada, b_ada)

    tm = min(512, t)
    tq = min(256, t)
    tk = min(512, t)
    xp = x_prompt
    xs = x_sample.reshape(1, nb, d)
    outs_p, outs_s = [], []
    for l in range(depth):
        lw = {
            "norm_g": norm_g[l],
            "f1_in": ffn1_w_in[l].astype(BF16), "f1_out": ffn1_w_out[l].astype(BF16),
            "f2_in": ffn2_w_in[l].astype(BF16), "f2_out": ffn2_w_out[l].astype(BF16),
            "w_all": _proj_weight(w_in[l]).astype(BF16),
            "qb": _q_bias(),
            "wkvt": w_in[l][:, NSA_WIDTH:NSA_WIDTH + 3 * KV_COLS].T.astype(BF16),
            "wa2p": jnp.pad(gla_wa2[l], ((0, 128 - GLA_RANK), (0, 0))).astype(BF16),
            "ba": gla_ba[l][None, :],
            "gla_norm": gla_norm[l][None, :],
            "w_out": w_out[l].astype(BF16),
            "cmp": _cmp_weights(cmp_pe[l], cmp_w1[l], cmp_b1[l], cmp_w2[l]),
        }
        fg = final_norm[None, :] if l == depth - 1 else None
        xp, st_p = _layer_prompt(xp, _mods(mod_all[l, :b], 1), lw, tm, tq, tk, fg)
        xs, st_s = _layer_sample(l, xs, _mods(mod_all[l, bp:], nb), lw, cache_kv_cmp, cache_kv_slc,
                                 cache_kv_win, state_gla[l], page_table, fg)
        outs_p.append(st_p)
        outs_s.append(st_s)
    stack = lambda outs, j: jnp.stack([o[j] for o in outs])
    return (xp, xs.reshape(nb, 1, d),
            stack(outs_p, 0), stack(outs_p, 1), stack(outs_p, 2), stack(outs_p, 3),
            stack(outs_s, 0), stack(outs_s, 1), stack(outs_s, 2), stack(outs_s, 3))
```

```python
import functools

import numpy as np
import jax
import jax.numpy as jnp
from jax import lax
from jax.experimental import pallas as pl
from jax.experimental.pallas import tpu as pltpu

F32 = jnp.float32
BF16 = jnp.bfloat16

D_MODEL = 1024
NSA_HD = 64
NSA_HEADS = 8
NSA_KV_HEADS = 2
NSA_GROUP = 4
NSA_WIDTH = NSA_HEADS * NSA_HD
CMP_BLOCK = 32
CMP_STRIDE = 16
CMP_HIDDEN = 2 * NSA_HD
SEL_BLOCK = 64
SEL_TOPK = 16
WINDOW = 512
FORCE_BONUS = 1000.0
GLA_HEADS = 4
GLA_DV = 128
GLA_DK = 64
GLA_WIDTH = GLA_HEADS * GLA_DV
GLA_RANK = 16
GLA_TAU = 16.0
GLA_CHUNK = 64
GLA_SUB = 16
FFN_DIM = 2816
N_MOD = 9
EPS = 1e-6
KV_COLS = 2 * NSA_KV_HEADS * NSA_HD
IN_SIZES = (NSA_WIDTH, KV_COLS, KV_COLS, KV_COLS, 3 * NSA_HEADS,
            GLA_HEADS * GLA_DK, GLA_HEADS * GLA_DK, GLA_WIDTH, GLA_RANK, GLA_WIDTH)
IN_PAD = (NSA_HEADS * 128, KV_COLS, KV_COLS, KV_COLS, NSA_KV_HEADS * 128,
          GLA_HEADS * GLA_DK, GLA_HEADS * GLA_DK, GLA_WIDTH, 128, GLA_WIDTH)
IN_OFF = tuple(int(v) for v in np.cumsum((0,) + IN_PAD))

LANES = 128
NEG = -1e30
LOG2E = 1.4426950408889634
VMEM_LIMIT = 56 * 1024 * 1024
FFN_TF = FFN_DIM // 2
NSA_SAMPLE_SEQS = 2
GLA_UNROLL = 8


def _cparams(sem):
    return pltpu.CompilerParams(dimension_semantics=sem, vmem_limit_bytes=VMEM_LIMIT)


def _dot(a, b):
    return jnp.dot(a, b, preferred_element_type=F32)


def _dot_nt(a, b):
    return lax.dot_general(a, b, (((1,), (1,)), ((), ())), preferred_element_type=F32)


def _split3(x):
    x1 = x.astype(BF16)
    r = x - x1.astype(F32)
    x2 = r.astype(BF16)
    x3 = (r - x2.astype(F32)).astype(BF16)
    return x1, x2, x3


def _rms_mod(x, g, sc, sh):
    y = x * lax.rsqrt(jnp.mean(x * x, axis=-1, keepdims=True) + EPS) * g
    return y * (1.0 + sc) + sh


def _masked_softmax(lg, mask):
    lg = jnp.where(mask, lg, NEG)
    m = jnp.max(lg, axis=-1, keepdims=True)
    e = jnp.where(mask, jnp.exp(lg - m), 0.0)
    return e / jnp.maximum(jnp.sum(e, axis=-1, keepdims=True), 1e-30)


def _top_k_mask(score, blk, k, axis=-1):
    sel = jnp.zeros_like(score)
    big = float(score.shape[axis])
    for _ in range(k):
        m = jnp.max(score, axis=axis, keepdims=True)
        idx = jnp.min(jnp.where(score == m, blk, big), axis=axis, keepdims=True)
        hit = blk == idx
        sel = jnp.where(hit, 1.0, sel)
        score = jnp.where(hit, -jnp.inf, score)
    return sel


def _ada_kernel(c_ref, w_ref, b_ref, o_ref):
    c = c_ref[...]
    s = c * jax.nn.sigmoid(c)
    o_ref[0] = _dot(s.astype(BF16), w_ref[0].astype(BF16)) + b_ref[0]


def _ada(c_all, w_ada, b_ada):
    depth, d, n = w_ada.shape
    m = c_all.shape[0]
    tn = 1024
    return pl.pallas_call(
        _ada_kernel,
        grid=(depth, n // tn),
        in_specs=[pl.BlockSpec((m, d), lambda l, j: (0, 0)),
                  pl.BlockSpec((1, d, tn), lambda l, j: (l, 0, j)),
                  pl.BlockSpec((1, 1, tn), lambda l, j: (l, 0, j))],
        out_specs=pl.BlockSpec((1, m, tn), lambda l, j: (l, 0, j)),
        out_shape=jax.ShapeDtypeStruct((depth, m, n), F32),
        compiler_params=_cparams(("parallel", "parallel")),
        name="ada",
    )(c_all, w_ada, b_ada.reshape(depth, 1, n))


def _mod_spec(mod, tm, nargs):
    r = mod.shape[1]
    d = mod.shape[2]
    if nargs == 3:
        if r == 1:
            return pl.BlockSpec((1, 1, d), lambda b, i, f: (b, 0, 0))
        return pl.BlockSpec((1, tm, d), lambda b, i, f: (b, i, 0))
    if r == 1:
        return pl.BlockSpec((1, 1, d), lambda b, i: (b, 0, 0))
    return pl.BlockSpec((1, tm, d), lambda b, i: (b, i, 0))


def _ffn_kernel(x_ref, sh_ref, sc_ref, gt_ref, g_ref, wi_ref, wo_ref, fg_ref, o_ref, *, final):
    x = x_ref[0]
    h = _rms_mod(x, g_ref[...], sc_ref[0], sh_ref[0]).astype(BF16)
    y = jnp.zeros(x.shape, F32)
    for f in range(FFN_DIM // FFN_TF):
        g = _dot(h, wi_ref[:, f * FFN_TF:(f + 1) * FFN_TF])
        u = _dot(h, wi_ref[:, FFN_DIM + f * FFN_TF:FFN_DIM + (f + 1) * FFN_TF])
        a = (g * jax.nn.sigmoid(g) * u).astype(BF16)
        y = y + _dot(a, wo_ref[f * FFN_TF:(f + 1) * FFN_TF, :])
    x = x + 0.5 * gt_ref[0] * y
    if final:
        x = x * lax.rsqrt(jnp.mean(x * x, axis=-1, keepdims=True) + EPS) * fg_ref[...]
    o_ref[0] = x


def _ffn(x, sh, sc, gt, g, w_in, w_out, tm, final_gain=None):
    bx, tx, d = x.shape
    xs = pl.BlockSpec((1, tm, d), lambda b, i: (b, i, 0))
    ms = _mod_spec(sh, tm, 2)
    once = lambda a: pl.BlockSpec(a.shape, lambda b, i: (0,) * a.ndim, pipeline_mode=pl.Buffered(1))
    fg = g if final_gain is None else final_gain
    return pl.pallas_call(
        functools.partial(_ffn_kernel, final=final_gain is not None),
        grid=(bx, tx // tm),
        in_specs=[xs, ms, ms, ms, once(g), once(w_in), once(w_out), once(fg)],
        out_specs=xs,
        out_shape=jax.ShapeDtypeStruct(x.shape, F32),
        compiler_params=_cparams(("parallel", "parallel")),
        name="ffn",
    )(x, sh, sc, gt, g, w_in, w_out, fg)


def _proj_kernel(x_ref, sh_ref, sc_ref, g_ref, w_ref, wkvt_ref, wa2_ref, ba_ref, qb_ref, oh_ref, pos_ref, one_ref,
                 qn_ref, kvc_ref, kvs_ref, kvw_ref, kvct_ref, kvst_ref, kvwt_ref, ksa_ref, vsa_ref, kwa_ref, vwa_ref,
                 gn_ref, qg_ref, kg_ref, vg_ref, la_ref, og_ref):
    h = _rms_mod(x_ref[0], g_ref[...], sc_ref[0], sh_ref[0]).astype(BF16)
    p = _dot(h, w_ref[...])
    o = IN_OFF
    qn_ref[0] = (p[:, o[0]:o[1]] * (NSA_HD ** -0.5 * LOG2E) + qb_ref[...]).astype(BF16)
    kvc_ref[0] = p[:, o[1]:o[2]]
    kvs_ref[0] = p[:, o[2]:o[3]]
    kvw_ref[0] = p[:, o[3]:o[4]]
    kvt = _dot_nt(wkvt_ref[...], h)
    kvct_ref[0] = kvt[0:KV_COLS]
    kvst_ref[0] = kvt[KV_COLS:2 * KV_COLS]
    kvwt_ref[0] = kvt[2 * KV_COLS:3 * KV_COLS]
    kb = kvt.astype(BF16)
    oh, pos, one = oh_ref[...], pos_ref[...], one_ref[...]
    hd = NSA_HD
    for src, k_ref, v_ref in ((KV_COLS, ksa_ref, vsa_ref), (2 * KV_COLS, kwa_ref, vwa_ref)):
        k_parts, v_parts = [], []
        for gi in range(NSA_KV_HEADS):
            k_parts += ([oh] if k_ref is ksa_ref else []) + [kb[src + gi * hd:src + (gi + 1) * hd], pos]
            v_parts += [kb[src + (NSA_KV_HEADS + gi) * hd:src + (NSA_KV_HEADS + gi + 1) * hd], one]
        k_ref[0] = jnp.concatenate(k_parts, axis=0)
        v_ref[0] = jnp.concatenate(v_parts, axis=0)
    gn_ref[0] = jax.nn.sigmoid(p[:, o[4]:o[5]])
    qg_ref[0] = p[:, o[5]:o[6]] * (GLA_DK ** -0.5)
    kg_ref[0] = p[:, o[6]:o[7]]
    vg_ref[0] = p[:, o[7]:o[8]]
    a_pre = _dot(p[:, o[8]:o[9]].astype(BF16), wa2_ref[...]) + ba_ref[...]
    log_sig = jnp.minimum(a_pre, 0.0) - jnp.log(1.0 + jnp.exp(-jnp.abs(a_pre)))
    la_ref[0] = log_sig * (1.0 / GLA_TAU)
    og_ref[0] = p[:, o[9]:o[10]]


def _proj(x, sh, sc, g, w_all, wkvt, wa2p, ba, qb, tm):
    bx, tx, d = x.shape
    xs = pl.BlockSpec((1, tm, d), lambda b, i: (b, i, 0))
    ms = _mod_spec(sh, tm, 2)
    full = lambda shape: pl.BlockSpec(shape, lambda b, i: (0,) * len(shape))
    rows = lambda w, dt: (pl.BlockSpec((1, tm, w), lambda b, i: (b, i, 0)), jax.ShapeDtypeStruct((bx, tx, w), dt))
    cols = lambda w, dt: (pl.BlockSpec((1, w, tm), lambda b, i: (b, 0, i)), jax.ShapeDtypeStruct((bx, w, tx), dt))
    pos_np = np.arange(tx)
    nsp = max(LANES, tx // SEL_BLOCK)
    oh = jnp.asarray(np.arange(nsp)[:, None] == pos_np[None, :] // SEL_BLOCK, BF16)
    pos = jnp.pad(_alibi_k_rows(pos_np), ((0, NSA_HD - 4), (0, 0)))
    one = jnp.pad(jnp.ones((1, tx), BF16), ((0, NSA_HD - 1), (0, 0)))
    lanes = lambda a: pl.BlockSpec((a.shape[0], tm), lambda b, i: (0, i))
    g2 = NSA_KV_HEADS
    outs = [rows(IN_PAD[0], BF16), rows(KV_COLS, F32), rows(KV_COLS, F32), rows(KV_COLS, F32),
            cols(KV_COLS, F32), cols(KV_COLS, F32), cols(KV_COLS, F32),
            cols(g2 * (nsp + LANES), BF16), cols(g2 * LANES, BF16), cols(g2 * LANES, BF16), cols(g2 * LANES, BF16),
            rows(IN_PAD[4], F32), rows(256, F32), rows(256, F32), rows(GLA_WIDTH, F32), rows(256, F32),
            rows(GLA_WIDTH, F32)]
    return pl.pallas_call(
        _proj_kernel,
        grid=(bx, tx // tm),
        in_specs=[xs, ms, ms, full((1, d)), full(w_all.shape), full(wkvt.shape), full(wa2p.shape),
                  full(ba.shape), full(qb.shape), lanes(oh), lanes(pos), lanes(one)],
        out_specs=[o[0] for o in outs],
        out_shape=[o[1] for o in outs],
        compiler_params=_cparams(("parallel", "parallel")),
        name="proj",
    )(x, sh, sc, g, w_all, wkvt, wa2p, ba, qb, oh, pos, one)


def _cmp_kernel(x_ref, pelo_ref, pehi_ref, w1lo_ref, w1hi_ref, b1_ref, w2_ref, o_ref):
    x = x_ref[0]
    n = x.shape[0]
    h_lo = _dot_nt(w1lo_ref[...], (x + pelo_ref[...]).astype(BF16))
    h_hi = _dot_nt(w1hi_ref[...], (x + pehi_ref[...]).astype(BF16))
    h = h_lo + pltpu.roll(h_hi, n - 1, 1) + b1_ref[...]
    o_ref[0] = _dot(w2_ref[...], jax.nn.gelu(h).astype(BF16))


def _cmp_prompt(x16, cw):
    b, n, w = x16.shape
    full = lambda a: pl.BlockSpec(a.shape, lambda i: (0,) * a.ndim)
    args = (cw["pe_lo"], cw["pe_hi"], cw["w1lo_t"], cw["w1hi_t"], cw["b1_col"], cw["w2_t"])
    return pl.pallas_call(
        _cmp_kernel,
        grid=(b,),
        in_specs=[pl.BlockSpec((1, n, w), lambda i: (i, 0, 0))] + [full(a) for a in args],
        out_specs=pl.BlockSpec((1, KV_COLS, n), lambda i: (i, 0, 0)),
        out_shape=jax.ShapeDtypeStruct((b, KV_COLS, n), F32),
        compiler_params=_cparams(("parallel",)),
        name="cmp_prompt",
    )(x16, *args)


def _nsa_prompt_kernel(q_ref, gate_ref, ck_ref, cv_ref, ks_ref, vs_ref, kw_ref, vw_ref, ovt_ref,
                       o_ref, ids_scr, *, tq, tk, ncp, nsp):
    i = pl.program_id(2)
    q0 = i * tq
    nr = NSA_GROUP * tq
    row = q0 + jnp.bitwise_and(lax.broadcasted_iota(jnp.int32, (nr, 1), 0), tq - 1)
    row_t = q0 + lax.broadcasted_iota(jnp.int32, (tq, 1), 0)
    qblk = q_ref[0]
    q = jnp.concatenate([qblk[:, r * LANES:(r + 1) * LANES] for r in range(NSA_GROUP)], axis=0)

    def per_head(x, mask):
        return (x.reshape(NSA_GROUP, tq, x.shape[-1]) + mask[None]).reshape(x.shape)

    c_end = lax.broadcasted_iota(jnp.int32, (1, ncp), 1) * CMP_STRIDE + (CMP_BLOCK - 1)
    lg = per_head(_dot(q, ck_ref[0, 0]), jnp.where(c_end <= row_t, 0.0, NEG))
    e = jnp.exp2(lg - jnp.max(lg, axis=-1, keepdims=True))
    pc = e * jnp.where(row >= CMP_BLOCK - 1, 1.0 / jnp.sum(e, axis=-1, keepdims=True), 0.0)
    o_cmp = _dot_nt(pc.astype(BF16), cv_ref[0, 0])
    psum = pc[0:tq]
    for r in range(1, NSA_GROUP):
        psum = psum + pc[r * tq:(r + 1) * tq]
    p_hi = psum.astype(BF16)
    p_lo = (psum - p_hi.astype(F32)).astype(BF16)
    imp_t = _dot_nt(ovt_ref[...], p_hi) + _dot_nt(ovt_ref[...], p_lo)

    wl = WINDOW + tq
    w0 = pl.multiple_of(jnp.maximum(q0 - WINDOW, 0), tq)
    dist_w = row_t - (w0 + lax.broadcasted_iota(jnp.int32, (1, wl), 1))
    neg_w = jnp.where(dist_w >= 0, jnp.where(dist_w < WINDOW, 0.0, NEG), NEG)
    lg = per_head(_dot(q, kw_ref[0, 0, :, pl.ds(w0, wl)]), neg_w)
    p = jnp.exp2(lg - jnp.max(lg, axis=-1, keepdims=True)).astype(BF16)
    ow = _dot_nt(p, vw_ref[0, 0, :, pl.ds(w0, wl)])
    o_win = ow[:, :NSA_HD] / ow[:, NSA_HD:NSA_HD + 1]
    gates = gate_ref[0]
    gcol = lambda j: jnp.concatenate([gates[:, 3 * r + j:3 * r + j + 1] for r in range(NSA_GROUP)], axis=0)
    o_two = gcol(0) * o_cmp + gcol(2) * o_win

    blk = lax.broadcasted_iota(jnp.int32, (nsp, 1), 0)
    t_lane = q0 + lax.broadcasted_iota(jnp.int32, (1, tq), 1)
    cur = jnp.right_shift(t_lane, 6)
    forced = jnp.where(blk == 0, 1.0, 0.0) + jnp.where(blk == cur, 1.0, 0.0) + jnp.where(blk == cur - 1, 1.0, 0.0)
    valid = blk * SEL_BLOCK <= t_lane
    score = jnp.where(valid, imp_t + jnp.where(forced > 0.5, FORCE_BONUS, 0.0), NEG)
    sel_t = _top_k_mask(score, blk.astype(F32), SEL_TOPK, axis=0)
    sel_t = jnp.where(valid, sel_t, 0.0)
    sel_bias = ((sel_t - 1.0) * 1e30).T.astype(BF16)
    qa = jnp.concatenate([jnp.concatenate([sel_bias] * NSA_GROUP, axis=0), q], axis=1)

    n_full = q0 // tk
    per_tile = tk // SEL_BLOCK
    n_tiles = nsp // per_tile
    blk_any = jnp.max(sel_t, axis=1, keepdims=True)
    tile_any = jnp.max(blk_any.reshape(n_tiles, per_tile, 1), axis=1)
    bit = jnp.left_shift(1, lax.broadcasted_iota(jnp.int32, (n_tiles, 1), 0))
    tile_bits = jnp.sum(jnp.where(tile_any > 0.0, bit, 0))
    n_used = jnp.int32(0)
    for kt in range(min(n_tiles, ks_ref.shape[3] // tk)):
        use = jnp.logical_and(jnp.bitwise_and(jnp.right_shift(tile_bits, kt), 1) == 1, kt < n_full)
        ids_scr[n_used] = kt
        n_used = n_used + use.astype(jnp.int32)

    def sel_tile(kt, carry, diag):
        m, acc = carry
        k0 = pl.multiple_of(kt * tk, tk)
        lg = _dot(qa, ks_ref[0, 0, :, pl.ds(k0, tk)])
        if diag:
            lg = jnp.where(k0 + lax.broadcasted_iota(jnp.int32, (1, tk), 1) <= row, lg, NEG)
        m_new = jnp.maximum(m, jnp.max(lg, axis=-1, keepdims=True))
        p = jnp.exp2(lg - m_new).astype(BF16)
        return m_new, jnp.exp2(m - m_new) * acc + _dot_nt(p, vs_ref[0, 0, :, pl.ds(k0, tk)])

    carry = sel_tile(n_full, (jnp.full((nr, 1), NEG, F32), jnp.zeros((nr, LANES), F32)), True)

    def two_tiles(j, c):
        return sel_tile(ids_scr[2 * j + 1], sel_tile(ids_scr[2 * j], c, False), False)

    carry = lax.fori_loop(0, n_used // 2, two_tiles, carry)
    _, acc = lax.cond(n_used % 2 == 1, lambda c: sel_tile(ids_scr[n_used - 1], c, False), lambda c: c, carry)
    o_slc = acc[:, :NSA_HD] / acc[:, NSA_HD:NSA_HD + 1]
    o = o_two + gcol(1) * o_slc
    o_ref[0] = jnp.concatenate([o[r * tq:(r + 1) * tq] for r in range(NSA_GROUP)], axis=1)


def _overlap_matrix(ncp, nsp):
    c = np.arange(ncp)[:, None] * CMP_STRIDE
    s = np.arange(nsp)[None, :] * SEL_BLOCK
    return jnp.asarray((c <= s + SEL_BLOCK - 1) & (c + CMP_BLOCK - 1 >= s), dtype=BF16)


def _bf16_round(x):
    return np.asarray(x, dtype=BF16).astype(np.float32)


def _q_bias():
    slopes = np.exp2(-8.0 * np.arange(1, NSA_HEADS + 1) / NSA_HEADS)
    c_hi = float(_bf16_round(LOG2E))
    c_lo = float(_bf16_round(LOG2E - c_hi))
    qb = np.zeros((NSA_HEADS, LANES), np.float32)
    qb[:, NSA_HD:NSA_HD + 4] = np.stack([c_hi * slopes * SEL_BLOCK, c_lo * slopes * SEL_BLOCK,
                                         c_hi * slopes, c_lo * slopes], axis=1)
    return jnp.asarray(qb.reshape(1, NSA_HEADS * LANES))


def _alibi_k_rows(pos):
    pos = np.asarray(pos)
    return jnp.asarray(np.stack([pos // SEL_BLOCK, pos // SEL_BLOCK, pos % SEL_BLOCK, pos % SEL_BLOCK]), BF16)


def _aug_rows(parts, total, lead):
    parts = [jnp.broadcast_to(p, lead + p.shape[-2:]) for p in parts]
    used = sum(p.shape[-2] for p in parts)
    if total > used:
        parts.append(jnp.zeros(lead + (total - used, parts[0].shape[-1]), BF16))
    return jnp.concatenate(parts, axis=-2)


def _nsa_prompt(q_pad, gates, ck, cv, ks, vs, kw, vw, tq, tk):
    b, t, _ = q_pad.shape
    g, r = NSA_KV_HEADS, NSA_GROUP
    ncp = ck.shape[3]
    nsp = max(LANES, t // SEL_BLOCK)
    kv = lambda a: pl.BlockSpec((1, 1) + a.shape[2:], lambda bi, gi, i: (bi, gi, 0, 0))
    ovt = _overlap_matrix(ncp, nsp).T
    return pl.pallas_call(
        functools.partial(_nsa_prompt_kernel, tq=tq, tk=tk, ncp=ncp, nsp=nsp),
        grid=(b, g, t // tq),
        in_specs=[pl.BlockSpec((1, tq, r * LANES), lambda bi, gi, i: (bi, i, gi)),
                  pl.BlockSpec((1, tq, LANES), lambda bi, gi, i: (bi, i, gi)),
                  kv(ck), kv(cv), kv(ks), kv(vs), kv(kw), kv(vw),
                  pl.BlockSpec((nsp, ncp), lambda bi, gi, i: (0, 0))],
        out_specs=pl.BlockSpec((1, tq, r * NSA_HD), lambda bi, gi, i: (bi, i, gi)),
        out_shape=jax.ShapeDtypeStruct((b, t, NSA_WIDTH), F32),
        scratch_shapes=[pltpu.SMEM((nsp // (tk // SEL_BLOCK),), jnp.int32)],
        compiler_params=_cparams(("parallel", "parallel", "arbitrary")),
        name="nsa_prompt",
    )(q_pad, gates, ck, cv, ks, vs, kw, vw, ovt)


def _gla_prompt_kernel(q_ref, k_ref, la_ref, v_ref, o_ref, s_ref, s_scr, *, nch):
    c = GLA_CHUNK
    ti = pl.program_id(2)

    @pl.when(ti == 0)
    def _():
        s_scr[...] = jnp.zeros_like(s_scr)

    two_dk, two_dv = 2 * GLA_DK, 2 * GLA_DV
    r_i = lax.broadcasted_iota(jnp.int32, (c, c), 0)
    c_i = lax.broadcasted_iota(jnp.int32, (c, c), 1)
    tril = jnp.where(c_i <= r_i, 1.0, 0.0).astype(BF16)
    e_r = lax.broadcasted_iota(jnp.int32, (two_dk, two_dk), 0)
    e_c = lax.broadcasted_iota(jnp.int32, (two_dk, two_dk), 1)
    eye = jnp.where(e_r == e_c, 1.0, 0.0).astype(BF16)
    b_r = lax.broadcasted_iota(jnp.int32, (two_dk, two_dv), 0) // GLA_DK
    b_c = lax.broadcasted_iota(jnp.int32, (two_dk, two_dv), 1) // GLA_DV
    same_head = jnp.where(b_r == b_c, 1.0, 0.0)
    ones_bd = same_head.astype(BF16)
    head0 = lax.broadcasted_iota(jnp.int32, (1, two_dk), 1) < GLA_DK
    pos = lax.broadcasted_iota(jnp.int32, (c, 1), 0)
    nsub = c // GLA_SUB
    w_i = lax.broadcasted_iota(jnp.int32, (1, GLA_SUB, 1), 1)

    def chunk(ci):
        rows = pl.ds(pl.multiple_of(ci * c, c), c)
        q = q_ref[0, rows, :]
        k = k_ref[0, rows, :]
        la = la_ref[0, rows, :]
        v = v_ref[0, rows, :]
        vb = v.astype(BF16)
        s0 = s_scr[...]
        a1, a2, a3 = _split3(la)
        cum = _dot(tril, a1) + _dot(tril, a2) + _dot(tril, a3)
        last = cum[c - 1:c, :]
        o = _dot((q * jnp.exp(cum)).astype(BF16), s0.astype(BF16))
        attn = jnp.zeros((2 * c, c), F32)
        for j in range(nsub - 1):
            ce = cum[GLA_SUB * (j + 1) - 1:GLA_SUB * (j + 1), :]
            qh = jnp.where(pos >= GLA_SUB * (j + 1), q * jnp.exp(jnp.minimum(cum - ce, 0.0)), 0.0)
            in_j = jnp.where(pos >= GLA_SUB * j, jnp.where(pos < GLA_SUB * (j + 1), 1.0, 0.0), 0.0)
            kh = in_j * (k * jnp.exp(jnp.minimum(ce - cum, 0.0)))
            qh2 = jnp.concatenate([jnp.where(head0, qh, 0.0), jnp.where(head0, 0.0, qh)], axis=0)
            attn = attn + _dot_nt(qh2.astype(BF16), kh.astype(BF16))
        o_off = _dot(attn.astype(BF16), vb)
        o = o + jnp.concatenate([o_off[:c, :GLA_DV], o_off[c:, GLA_DV:]], axis=1)
        q3 = q.reshape(nsub, GLA_SUB, two_dk)
        k3 = k.reshape(nsub, GLA_SUB, two_dk)
        c3 = cum.reshape(nsub, GLA_SUB, two_dk)
        v3 = v.reshape(nsub, GLA_SUB, two_dv)
        ws = []
        for u in range(GLA_SUB):
            w = q3 * k3[:, u:u + 1, :] * jnp.exp(jnp.minimum(c3 - c3[:, u:u + 1, :], 0.0))
            ws.append(jnp.where(w_i >= u, w, 0.0).reshape(c, two_dk).astype(BF16))
        a_rep = _dot(jnp.concatenate(ws, axis=0), ones_bd)
        od = jnp.zeros((nsub, GLA_SUB, two_dv), F32)
        for u in range(GLA_SUB):
            od = od + a_rep[u * c:(u + 1) * c].reshape(nsub, GLA_SUB, two_dv) * v3[:, u:u + 1, :]
        o_ref[0, rows, :] = o + od.reshape(c, two_dv)
        kd = (k * jnp.exp(last - cum)).astype(BF16)
        kd_t = _dot_nt(eye, kd).astype(BF16)
        l1, l2, l3 = _split3(jnp.broadcast_to(last, (8, two_dk)))
        last_col = (_dot_nt(eye, l1) + _dot_nt(eye, l2) + _dot_nt(eye, l3))[:, 0:1]
        s_scr[...] = jnp.exp(last_col) * s0 + same_head * _dot(kd_t, vb)

    def several(i, _):
        for j in range(GLA_UNROLL):
            chunk(GLA_UNROLL * i + j)
        return 0

    lax.fori_loop(0, nch // GLA_UNROLL, several, 0)

    @pl.when(ti == pl.num_programs(2) - 1)
    def _():
        s = s_scr[...]
        s_ref[0, 0] = s[:GLA_DK, :GLA_DV]
        s_ref[0, 1] = s[GLA_DK:, GLA_DV:]


def _gla_prompt(q, k, la, v, nch):
    b, t, _ = q.shape
    tc = nch * GLA_CHUNK
    hp = GLA_HEADS // 2
    qs = pl.BlockSpec((1, tc, 2 * GLA_DK), lambda bi, pi, ti: (bi, ti, pi))
    vs = pl.BlockSpec((1, tc, 2 * GLA_DV), lambda bi, pi, ti: (bi, ti, pi))
    return pl.pallas_call(
        functools.partial(_gla_prompt_kernel, nch=nch),
        grid=(b, hp, t // tc),
        in_specs=[qs, qs, qs, vs],
        out_specs=[vs, pl.BlockSpec((1, 2, GLA_DK, GLA_DV), lambda bi, pi, ti: (bi, pi, 0, 0))],
        out_shape=[jax.ShapeDtypeStruct((b, t, GLA_WIDTH), F32),
                   jax.ShapeDtypeStruct((b, GLA_HEADS, GLA_DK, GLA_DV), F32)],
        scratch_shapes=[pltpu.VMEM((2 * GLA_DK, 2 * GLA_DV), F32)],
        compiler_params=_cparams(("parallel", "parallel", "arbitrary")),
        name="gla_prompt",
    )(q, k, la, v)


def _out_kernel(x_ref, gt_ref, on_ref, ogla_ref, og_ref, gn_ref, w_ref, o_ref):
    y = _dot(on_ref[0].astype(BF16), w_ref[:NSA_WIDTH, :])
    og = og_ref[0]
    ogla = ogla_ref[0]
    heads = []
    for h in range(GLA_HEADS):
        o = ogla[:, h * GLA_DV:(h + 1) * GLA_DV]
        o = o * lax.rsqrt(jnp.mean(o * o, axis=-1, keepdims=True) + EPS) * gn_ref[...]
        gate = og[:, h * GLA_DV:(h + 1) * GLA_DV]
        heads.append((o * (gate * jax.nn.sigmoid(gate))).astype(BF16))
    y = y + _dot(jnp.concatenate(heads, axis=1), w_ref[NSA_WIDTH:, :])
    o_ref[0] = x_ref[0] + gt_ref[0] * y


def _out_proj(x, gt, o_nsa, o_gla, o_g, gla_norm, w_out, tm):
    bx, tx, d = x.shape
    xs = pl.BlockSpec((1, tm, d), lambda b, i: (b, i, 0))
    return pl.pallas_call(
        _out_kernel,
        grid=(bx, tx // tm),
        in_specs=[xs, _mod_spec(gt, tm, 2),
                  pl.BlockSpec((1, tm, NSA_WIDTH), lambda b, i: (b, i, 0)),
                  pl.BlockSpec((1, tm, GLA_WIDTH), lambda b, i: (b, i, 0)),
                  pl.BlockSpec((1, tm, GLA_WIDTH), lambda b, i: (b, i, 0)),
                  pl.BlockSpec((1, GLA_DV), lambda b, i: (0, 0)),
                  pl.BlockSpec(w_out.shape, lambda b, i: (0, 0))],
        out_specs=xs,
        out_shape=jax.ShapeDtypeStruct(x.shape, F32),
        compiler_params=_cparams(("parallel", "parallel")),
        name="out_proj",
    )(x, gt, o_nsa, o_gla, o_g, gla_norm, w_out)


def _nsa_sample_kernel(pt_ref, *rest, n_pages, page, past, wlen, nseq):
    del pt_ref
    per = 6 + 2 * n_pages
    (w1_ref, pe_ref, b1_ref, w2_ref, ovt_ref, o_ref, x_scr, bias_scr) = rest[nseq * per:]

    @pl.when(pl.program_id(0) == 0)
    def _():
        for kv in range(2):
            acc = jnp.zeros((8, 2 * LANES), F32)
            for s2 in range(CMP_BLOCK // 2):
                pe2 = jnp.broadcast_to(pe_ref[kv, s2:s2 + 1, :], (8, 2 * LANES))
                acc = acc + _dot(pe2.astype(BF16), w1_ref[kv, s2])
            bias_scr[kv] = acc + b1_ref[kv]

    ncb = past // CMP_STRIDE
    nrow = ncb + 8
    cpp = page // CMP_STRIDE
    r_i = lax.broadcasted_iota(jnp.int32, (page, page), 0)
    p_i = lax.broadcasted_iota(jnp.int32, (page, page), 1)
    in_chunk = jnp.bitwise_and(p_i, CMP_STRIDE - 1)
    perm_t = jnp.where(r_i * CMP_STRIDE == in_chunk * (cpp * CMP_STRIDE) + (p_i - in_chunk), 1.0, 0.0).astype(BF16)
    first = lax.broadcasted_iota(jnp.int32, (8, 1), 0) == 0
    seqs = [rest[s * per:(s + 1) * per] for s in range(nseq)]
    for s, refs in enumerate(seqs):
        base = s * nrow
        for kv in range(2):
            for pg in range(n_pages):
                rows = _dot_nt(perm_t, refs[5 + pg][0, 0, kv].reshape(LANES, page).astype(BF16))
                for off in range(CMP_STRIDE):
                    x_scr[kv, off, base + pg * cpp:base + (pg + 1) * cpp, :] = rows[off * cpp:(off + 1) * cpp, :]
            new = refs[2][0][:, kv * LANES:(kv + 1) * LANES]
            for off in range(CMP_STRIDE):
                x_scr[kv, off, base + ncb:base + nrow, :] = (jnp.where(first, new, 0.0) if off == 0
                                                             else jnp.zeros((8, LANES), F32))
    ckvs = [[None, None] for _ in range(nseq)]
    half = CMP_STRIDE // 2
    for kv in range(2):
        acc_lo = jnp.zeros((nseq * nrow, 2 * LANES), F32)
        acc_hi = jnp.zeros((nseq * nrow, 2 * LANES), F32)
        for j in range(half):
            lhs = jnp.concatenate([x_scr[kv, 2 * j], x_scr[kv, 2 * j + 1]], axis=1).astype(BF16)
            acc_lo = acc_lo + _dot(lhs, w1_ref[kv, j])
            acc_hi = acc_hi + _dot(lhs, w1_ref[kv, half + j])
        hs = [jax.nn.gelu(acc_lo[s * nrow:s * nrow + ncb]
                          + pltpu.roll(acc_hi[s * nrow:(s + 1) * nrow], nrow - 1, 0)[:ncb] + bias_scr[kv][0:1])
              for s in range(nseq)]
        ckv = _dot(jnp.concatenate(hs, axis=0).astype(BF16), w2_ref[kv]).astype(BF16)
        for s in range(nseq):
            ckvs[s][kv] = ckv[s * ncb:(s + 1) * ncb]

    for s, refs in enumerate(seqs):
        _nsa_sample_seq(refs[:5], ckvs[s], refs[5 + n_pages:5 + 2 * n_pages], refs[5 + 2 * n_pages],
                        ovt_ref, o_ref.at[s], n_pages=n_pages, page=page, past=past, wlen=wlen)


def _nsa_sample_seq(row_refs, ckv, slc_pages, cw_ref, ovt_ref, o_ref, *, n_pages, page, past, wlen):
    qall_ref, gate_ref, _, kvs_ref, kvw_ref = row_refs
    nh = NSA_HEADS
    qa = qall_ref[0]
    qa_f = qa.astype(F32)
    hrow = lax.broadcasted_iota(jnp.int32, (nh, 1), 0)
    slope = jnp.exp2(-(hrow + 1).astype(F32)) * LOG2E
    lane = lax.broadcasted_iota(jnp.int32, (1, LANES), 1)
    c_end = lane * CMP_STRIDE + (CMP_BLOCK - 1)
    dist_c = past - c_end
    mask_c = dist_c >= 0
    lg_c = jnp.where(mask_c, _dot_nt(qa, ckv[0]) - slope * dist_c.astype(F32), NEG)
    e_c = jnp.where(mask_c, jnp.exp2(lg_c - jnp.max(lg_c, axis=-1, keepdims=True)), 0.0)
    pc = e_c / jnp.maximum(jnp.sum(e_c, axis=-1, keepdims=True), 1e-30)
    o_cmp = _dot(pc.astype(BF16), ckv[1])

    ps0 = jnp.sum(jnp.where(hrow < NSA_GROUP, pc, 0.0), axis=0, keepdims=True)
    ps1 = jnp.sum(jnp.where(hrow >= NSA_GROUP, pc, 0.0), axis=0, keepdims=True)
    psum = jnp.where(hrow == 0, ps0, jnp.where(hrow == 1, ps1, 0.0))
    psum = jnp.concatenate([psum, jnp.zeros((LANES - nh, LANES), F32)], axis=0)
    p_hi = psum.astype(BF16)
    p_lo = (psum - p_hi.astype(F32)).astype(BF16)
    imp_t = _dot_nt(ovt_ref[...], p_hi) + _dot_nt(ovt_ref[...], p_lo)
    blk = lax.broadcasted_iota(jnp.int32, (LANES, 1), 0)
    cur = past // SEL_BLOCK
    forced = jnp.where(blk == 0, 1.0, 0.0) + jnp.where(blk == cur, 1.0, 0.0) + jnp.where(blk == cur - 1, 1.0, 0.0)
    score = jnp.where(blk * SEL_BLOCK <= past, imp_t + jnp.where(forced > 0.5, FORCE_BONUS, 0.0), NEG)
    sel = _top_k_mask(score, blk.astype(F32), SEL_TOPK, axis=0).T
    sel_h = jnp.where(hrow < NSA_GROUP, sel[0:1], sel[1:2])

    def attend(tiles, new_row, new_neg):
        lgs = [_dot(qa, k_t) + neg for k_t, _, neg in tiles]
        lg_new = jnp.sum(qa_f * new_row[:, :LANES], axis=-1, keepdims=True) + new_neg
        m = lg_new
        for lg in lgs:
            m = jnp.maximum(m, jnp.max(lg, axis=-1, keepdims=True))
        p_new = jnp.exp2(lg_new - m)
        l = p_new
        acc = p_new * new_row[:, LANES:]
        for lg, (_, v_t, _) in zip(lgs, tiles):
            p = jnp.exp2(lg - m)
            l = l + jnp.sum(p, axis=-1, keepdims=True)
            acc = acc + _dot_nt(p.astype(BF16), v_t)
        return acc / l

    per = page // SEL_BLOCK
    tiles = []
    for pg in range(n_pages):
        blkp = slc_pages[pg]
        dist = past - (pg * page + lane)
        selx = jnp.zeros((nh, LANES), F32)
        for u in range(per):
            in_u = jnp.where(lane >= u * SEL_BLOCK, jnp.where(lane < (u + 1) * SEL_BLOCK, 1.0, 0.0), 0.0)
            selx = selx + in_u * sel_h[:, pg * per + u:pg * per + u + 1]
        neg = jnp.where(dist >= 0, (selx - 1.0) * 1e30, NEG) - slope * dist.astype(F32)
        tiles.append((blkp[0, 0, 0].reshape(LANES, page).astype(BF16),
                      blkp[0, 0, 1].reshape(LANES, page).astype(BF16), neg))
    o_slc = attend(tiles, kvs_ref[0], (sel_h[:, cur:cur + 1] - 1.0) * 1e30)

    wlane = lax.broadcasted_iota(jnp.int32, (1, wlen), 1)
    w_pos = past - wlen + wlane
    dist = past - w_pos
    ok = jnp.where(dist >= 0, jnp.where(dist < WINDOW, jnp.where(w_pos >= 0, 1.0, 0.0), 0.0), 0.0)
    neg = (ok - 1.0) * 1e30 - slope * dist.astype(F32)
    tiles = [(cw_ref[0, 0, 0].reshape(LANES, wlen).astype(BF16), cw_ref[0, 0, 1].reshape(LANES, wlen).astype(BF16), neg)]
    o_win = attend(tiles, kvw_ref[0], jnp.zeros((nh, 1), F32))

    gates = gate_ref[0]
    pick = lambda a: jnp.where(hrow < NSA_GROUP, a[:, :NSA_HD], a[:, NSA_HD:])
    o_ref[...] = gates[:, 0:1] * pick(o_cmp) + gates[:, 1:2] * pick(o_slc) + gates[:, 2:3] * pick(o_win)


def _nsa_sample(layer, page_table, q_all, gates, kvc, kvs, kvw, cache_c, cache_s, cache_w, cw):
    nb, n_pages = page_table.shape
    depth, n_phys, page = cache_c.shape[:3]
    past = n_pages * page
    wlen = cache_w.shape[2]
    ncb = past // CMP_STRIDE
    cc = cache_c.transpose(0, 1, 3, 4, 5, 2)
    cs = cache_s.transpose(0, 1, 3, 4, 5, 2)
    cwin = cache_w.transpose(0, 1, 3, 4, 5, 2)
    nseq = NSA_SAMPLE_SEQS if nb % NSA_SAMPLE_SEQS == 0 else 1
    full = lambda a: pl.BlockSpec(a.shape, lambda b, pt: (0,) * a.ndim)
    ovt = _overlap_matrix(ncb, LANES).T
    wargs = (cw["w1s"], cw["pes"], cw["b1s"], cw["w2s"], ovt)
    in_specs, args = [], []
    for s in range(nseq):
        seq = lambda b, s=s: nseq * b + s
        heads = pl.BlockSpec((1, NSA_HEADS, LANES), lambda b, pt, seq=seq: (seq(b), 0, 0))
        row = pl.BlockSpec((1, 1, KV_COLS), lambda b, pt, seq=seq: (seq(b), 0, 0))
        pages = [pl.BlockSpec((1, 1, 2, NSA_KV_HEADS, NSA_HD, page),
                              lambda b, pt, seq=seq, j=j: (layer, pt[seq(b), j], 0, 0, 0, 0)) for j in range(n_pages)]
        win = pl.BlockSpec((1, 1, 2, NSA_KV_HEADS, NSA_HD, wlen), lambda b, pt, seq=seq: (layer, seq(b), 0, 0, 0, 0))
        in_specs += [heads, heads, row, row, row] + pages + pages + [win]
        args += [q_all, gates, kvc, kvs, kvw] + [cc] * n_pages + [cs] * n_pages + [cwin]
    grid_spec = pltpu.PrefetchScalarGridSpec(
        num_scalar_prefetch=1,
        grid=(nb // nseq,),
        in_specs=in_specs + [full(a) for a in wargs],
        out_specs=pl.BlockSpec((nseq, NSA_HEADS, NSA_HD), lambda b, pt: (b, 0, 0)),
        scratch_shapes=[pltpu.VMEM((2, CMP_STRIDE, nseq * (ncb + 8), LANES), F32), pltpu.VMEM((2, 8, 2 * LANES), F32)],
    )
    return pl.pallas_call(
        functools.partial(_nsa_sample_kernel, n_pages=n_pages, page=page, past=past, wlen=wlen, nseq=nseq),
        grid_spec=grid_spec,
        out_shape=jax.ShapeDtypeStruct((nb, NSA_HEADS, NSA_HD), F32),
        compiler_params=_cparams(("arbitrary",)),
        name="nsa_sample",
    )(page_table, *args, *wargs)


def _gla_sample_kernel(col_ref, v_ref, s_ref, o_ref, sn_ref, *, bb):
    for b in range(bb):
        cols = col_ref[b]
        for h in range(GLA_HEADS):
            a = jnp.exp(cols[:, h:h + 1])
            k = cols[:, GLA_HEADS + h:GLA_HEADS + h + 1]
            q = cols[:, 2 * GLA_HEADS + h:2 * GLA_HEADS + h + 1]
            v = v_ref[b, h:h + 1, :]
            s0 = s_ref[b, h]
            qk = jnp.sum(q * k, axis=0, keepdims=True)
            o_ref[b, h:h + 1, :] = jnp.sum((q * a) * s0, axis=0, keepdims=True) + qk * v
            sn_ref[b, h] = a * s0 + k * v


def _gla_sample(cols, v, state, bb):
    nb = v.shape[0]
    return pl.pallas_call(
        functools.partial(_gla_sample_kernel, bb=bb),
        grid=(nb // bb,),
        in_specs=[pl.BlockSpec((bb, GLA_DK, 16), lambda i: (i, 0, 0)),
                  pl.BlockSpec((bb, GLA_HEADS, GLA_DV), lambda i: (i, 0, 0)),
                  pl.BlockSpec((bb, GLA_HEADS, GLA_DK, GLA_DV), lambda i: (i, 0, 0, 0))],
        out_specs=[pl.BlockSpec((bb, GLA_HEADS, GLA_DV), lambda i: (i, 0, 0)),
                   pl.BlockSpec((bb, GLA_HEADS, GLA_DK, GLA_DV), lambda i: (i, 0, 0, 0))],
        out_shape=[jax.ShapeDtypeStruct((nb, GLA_HEADS, GLA_DV), F32),
                   jax.ShapeDtypeStruct((nb, GLA_HEADS, GLA_DK, GLA_DV), F32)],
        compiler_params=_cparams(("parallel",)),
        name="gla_sample",
    )(cols, v, state)


def _proj_weight(w):
    d = w.shape[0]
    offs = np.cumsum((0,) + IN_SIZES)
    seg = lambda i: w[:, offs[i]:offs[i + 1]]

    def slots(a, n):
        per = a.shape[1] // n
        return jnp.pad(a.reshape(d, n, per), ((0, 0), (0, 0), (0, LANES - per))).reshape(d, n * LANES)

    return jnp.concatenate([slots(seg(0), NSA_HEADS), seg(1), seg(2), seg(3), slots(seg(4), NSA_KV_HEADS),
                            seg(5), seg(6), seg(7), slots(seg(8), 1), seg(9)], axis=1)


def _cmp_weights(pe, w1, b1, w2):
    g = NSA_KV_HEADS
    eye = jnp.eye(2, dtype=F32)
    eg = jnp.eye(g, dtype=F32)
    w1r = w1.reshape(2, 2, CMP_STRIDE, NSA_HD, CMP_HIDDEN)
    big = jnp.einsum("krsdh,ka,gb->rskgdabh", w1r, eye, eg).reshape(2, CMP_STRIDE * KV_COLS, 2 * g * CMP_HIDDEN)
    w2big = jnp.einsum("khd,ka,gb->kghabd", w2, eye, eg).reshape(2 * g * CMP_HIDDEN, KV_COLS)
    per = pe.reshape(2, 2, CMP_STRIDE, NSA_HD)
    pet = jnp.broadcast_to(per.transpose(1, 2, 0, 3)[:, :, :, None, :], (2, CMP_STRIDE, 2, g, NSA_HD))
    pet = pet.reshape(2, 1, CMP_STRIDE * KV_COLS)
    b1big = jnp.broadcast_to(b1[:, None, :], (2, g, CMP_HIDDEN)).reshape(1, 2 * g * CMP_HIDDEN)
    w1s = jnp.einsum("ksdh,gb->ksgdbh", w1.reshape(2, CMP_BLOCK, NSA_HD, CMP_HIDDEN), eg)
    w1s = w1s.reshape(2, CMP_BLOCK // 2, 2 * g * NSA_HD, g * CMP_HIDDEN)
    w2s = jnp.einsum("khd,gb->kghbd", w2, eg).reshape(2, g * CMP_HIDDEN, g * NSA_HD)
    pes = jnp.broadcast_to(pe[:, :, None, :], (2, CMP_BLOCK, g, NSA_HD)).reshape(2, CMP_BLOCK // 2, 2 * g * NSA_HD)
    b1s = jnp.broadcast_to(b1[:, None, None, :], (2, 1, g, CMP_HIDDEN)).reshape(2, 1, g * CMP_HIDDEN)
    return {
        "w1s": w1s.astype(BF16), "w2s": w2s.astype(BF16), "pes": pes, "b1s": b1s,
        "w1lo_t": big[0].T.astype(BF16), "w1hi_t": big[1].T.astype(BF16),
        "pe_lo": pet[0], "pe_hi": pet[1], "b1_col": b1big.T, "w2_t": w2big.T.astype(BF16),
    }


def _mods(mod, rows):
    m = mod.reshape(mod.shape[0], N_MOD, D_MODEL)
    if rows == 1:
        return [m[:, j][:, None, :] for j in range(N_MOD)]
    return [m[:, j][None, :, :] for j in range(N_MOD)]


def _layer_prompt(x, mods, lw, tm, tq, tk, final_gain):
    b, t, _ = x.shape
    x = _ffn(x, mods[0], mods[1], mods[2], lw["norm_g"][0:1], lw["f1_in"], lw["f1_out"], tm)
    (qn, kvc, _, _, kvct, kvst, kvwt, ks, vs, kw, vw, gn, qg, kg, vg, la, og) = _proj(
        x, mods[3], mods[4], lw["norm_g"][1:2], lw["w_all"], lw["wkvt"], lw["wa2p"], lw["ba"], lw["qb"], tm)
    g = NSA_KV_HEADS
    ncp = t // CMP_STRIDE
    ckvt = _cmp_prompt(kvc.reshape(b, ncp, CMP_STRIDE * KV_COLS), lw["cmp"]).astype(BF16)
    ckvt = ckvt.reshape(b, 2, g, NSA_HD, ncp)
    ck = _aug_rows([ckvt[:, 0], _alibi_k_rows(np.arange(ncp) * CMP_STRIDE + CMP_BLOCK - 1)], LANES, (b, g))
    per_group = lambda a: a.reshape(b, g, a.shape[1] // g, t)
    o_nsa = _nsa_prompt(qn, gn, ck, ckvt[:, 1], per_group(ks), per_group(vs), per_group(kw), per_group(vw), tq, tk)
    o_gla, s_fin = _gla_prompt(qg, kg, la, vg, min(8, t // GLA_CHUNK))
    x = _out_proj(x, mods[5], o_nsa, o_gla, og, lw["gla_norm"], lw["w_out"], tm)
    x = _ffn(x, mods[6], mods[7], mods[8], lw["norm_g"][2:3], lw["f2_in"], lw["f2_out"], tm, final_gain)
    win = min(WINDOW, t)
    back = lambda a: a.reshape(b, 2, NSA_KV_HEADS, NSA_HD, a.shape[-1]).transpose(0, 4, 1, 2, 3)
    return x, (back(kvct), back(kvst), back(kvwt[:, :, t - win:]), s_fin)


def _layer_sample(layer, x, mods, lw, cache_c, cache_s, cache_w, state, page_table, final_gain):
    nb = x.shape[1]
    x = _ffn(x, mods[0], mods[1], mods[2], lw["norm_g"][0:1], lw["f1_in"], lw["f1_out"], nb)
    (qn, kvc, kvs, kvw, kvct, kvst, kvwt, _, _, _, _, gn, qg, kg, vg, la, og) = _proj(
        x, mods[3], mods[4], lw["norm_g"][1:2], lw["w_all"], lw["wkvt"], lw["wa2p"], lw["ba"], lw["qb"], nb)
    q4 = qn[0].reshape(nb, NSA_KV_HEADS, NSA_GROUP, LANES)[..., :NSA_HD]
    q_all = (q4[:, :, :, None, :] * jnp.eye(NSA_KV_HEADS, dtype=BF16)[None, :, None, :, None]).reshape(nb, NSA_HEADS, LANES)
    gates = gn[0].reshape(nb, NSA_KV_HEADS, LANES)[..., :3 * NSA_GROUP].reshape(nb, NSA_HEADS, 3)
    gates = jnp.pad(gates, ((0, 0), (0, 0), (0, LANES - 3)))
    r3 = lambda a: a[0][:, None, :]
    o_nsa = _nsa_sample(layer, page_table, q_all, gates, r3(kvc), r3(kvs), r3(kvw),
                        cache_c, cache_s, cache_w, lw["cmp"])
    o_nsa = o_nsa.reshape(1, nb, NSA_WIDTH)
    col = lambda a: a[0].reshape(nb, GLA_HEADS, GLA_DK).transpose(0, 2, 1)
    cols = jnp.concatenate([col(la), col(kg), col(qg), jnp.zeros((nb, GLA_DK, 16 - 3 * GLA_HEADS), F32)], axis=-1)
    o_gla, s_new = _gla_sample(cols, vg[0].reshape(nb, GLA_HEADS, GLA_DV), state, 8)
    o_gla = o_gla.reshape(1, nb, GLA_WIDTH)
    x = _out_proj(x, mods[5], o_nsa, o_gla, og, lw["gla_norm"], lw["w_out"], nb)
    x = _ffn(x, mods[6], mods[7], mods[8], lw["norm_g"][2:3], lw["f2_in"], lw["f2_out"], nb, final_gain)
    kv5 = lambda a: a[0].reshape(2, NSA_KV_HEADS, NSA_HD, nb).transpose(3, 0, 1, 2)[:, None]
    return x, (kv5(kvct), kv5(kvst), kv5(kvwt), s_new)


def kernel(x_prompt, x_sample, c_prompt, c_sample, cache_kv_cmp, cache_kv_slc, cache_kv_win, state_gla, page_table, w_ada, b_ada, norm_g, ffn1_w_in, ffn1_w_out, w_in, cmp_pe, cmp_w1, cmp_b1, cmp_w2, gla_wa2, gla_ba, gla_norm, w_out, ffn2_w_in, ffn2_w_out, final_norm):
    depth = w_ada.shape[0]
    b, t, d = x_prompt.shape
    nb = x_sample.shape[0]
    bp = -(-b // 8) * 8
    c_all = jnp.concatenate([c_prompt, jnp.zeros((bp - b, d), F32), c_sample], axis=0)
    mod_all = _ada(c_all, w_ada, b_ada)

    tm = min(512, t)
    tq = min(256, t)
    tk = min(512, t)
    xp = x_prompt
    xs = x_sample.reshape(1, nb, d)
    outs_p, outs_s = [], []
    for l in range(depth):
        lw = {
            "norm_g": norm_g[l],
            "f1_in": ffn1_w_in[l].astype(BF16), "f1_out": ffn1_w_out[l].astype(BF16),
            "f2_in": ffn2_w_in[l].astype(BF16), "f2_out": ffn2_w_out[l].astype(BF16),
            "w_all": _proj_weight(w_in[l]).astype(BF16),
            "qb": _q_bias(),
            "wkvt": w_in[l][:, NSA_WIDTH:NSA_WIDTH + 3 * KV_COLS].T.astype(BF16),
            "wa2p": jnp.pad(gla_wa2[l], ((0, 128 - GLA_RANK), (0, 0))).astype(BF16),
            "ba": gla_ba[l][None, :],
            "gla_norm": gla_norm[l][None, :],
            "w_out": w_out[l].astype(BF16),
            "cmp": _cmp_weights(cmp_pe[l], cmp_w1[l], cmp_b1[l], cmp_w2[l]),
        }
        fg = final_norm[None, :] if l == depth - 1 else None
        xp, st_p = _layer_prompt(xp, _mods(mod_all[l, :b], 1), lw, tm, tq, tk, fg)
        xs, st_s = _layer_sample(l, xs, _mods(mod_all[l, bp:], nb), lw, cache_kv_cmp, cache_kv_slc,
                                 cache_kv_win, state_gla[l], page_table, fg)
        outs_p.append(st_p)
        outs_s.append(st_s)
    stack = lambda outs, j: jnp.stack([o[j] for o in outs])
    return (xp, xs.reshape(nb, 1, d),
            stack(outs_p, 0), stack(outs_p, 1), stack(outs_p, 2), stack(outs_p, 3),
            stack(outs_s, 0), stack(outs_s, 1), stack(outs_s, 2), stack(outs_s, 3))
```

```python
import functools

import numpy as np
import jax
import jax.numpy as jnp
from jax import lax
from jax.experimental import pallas as pl
from jax.experimental.pallas import tpu as pltpu

F32 = jnp.float32
BF16 = jnp.bfloat16

D_MODEL = 1024
NSA_HD = 64
NSA_HEADS = 8
NSA_KV_HEADS = 2
NSA_GROUP = 4
NSA_WIDTH = NSA_HEADS * NSA_HD
CMP_BLOCK = 32
CMP_STRIDE = 16
CMP_HIDDEN = 2 * NSA_HD
SEL_BLOCK = 64
SEL_TOPK = 16
WINDOW = 512
FORCE_BONUS = 1000.0
GLA_HEADS = 4
GLA_DV = 128
GLA_DK = 64
GLA_WIDTH = GLA_HEADS * GLA_DV
GLA_RANK = 16
GLA_TAU = 16.0
GLA_CHUNK = 64
GLA_SUB = 16
FFN_DIM = 2816
N_MOD = 9
EPS = 1e-6
KV_COLS = 2 * NSA_KV_HEADS * NSA_HD
IN_SIZES = (NSA_WIDTH, KV_COLS, KV_COLS, KV_COLS, 3 * NSA_HEADS,
            GLA_HEADS * GLA_DK, GLA_HEADS * GLA_DK, GLA_WIDTH, GLA_RANK, GLA_WIDTH)
IN_PAD = (NSA_HEADS * 128, KV_COLS, KV_COLS, KV_COLS, NSA_KV_HEADS * 128,
          GLA_HEADS * GLA_DK, GLA_HEADS * GLA_DK, GLA_WIDTH, 128, GLA_WIDTH)
IN_OFF = tuple(int(v) for v in np.cumsum((0,) + IN_PAD))

LANES = 128
NEG = -1e30
LOG2E = 1.4426950408889634
VMEM_LIMIT = 56 * 1024 * 1024
FFN_TF = FFN_DIM // 2
NSA_SAMPLE_SEQS = 2
GLA_UNROLL = 8


def _cparams(sem):
    return pltpu.CompilerParams(dimension_semantics=sem, vmem_limit_bytes=VMEM_LIMIT)


def _dot(a, b):
    return jnp.dot(a, b, preferred_element_type=F32)


def _dot_nt(a, b):
    return lax.dot_general(a, b, (((1,), (1,)), ((), ())), preferred_element_type=F32)


def _split3(x):
    x1 = x.astype(BF16)
    r = x - x1.astype(F32)
    x2 = r.astype(BF16)
    x3 = (r - x2.astype(F32)).astype(BF16)
    return x1, x2, x3


def _rms_mod(x, g, sc, sh):
    y = x * lax.rsqrt(jnp.mean(x * x, axis=-1, keepdims=True) + EPS) * g
    return y * (1.0 + sc) + sh


def _masked_softmax(lg, mask):
    lg = jnp.where(mask, lg, NEG)
    m = jnp.max(lg, axis=-1, keepdims=True)
    e = jnp.where(mask, jnp.exp(lg - m), 0.0)
    return e / jnp.maximum(jnp.sum(e, axis=-1, keepdims=True), 1e-30)


def _top_k_mask(score, blk, k, axis=-1):
    sel = jnp.zeros_like(score)
    big = float(score.shape[axis])
    for _ in range(k):
        m = jnp.max(score, axis=axis, keepdims=True)
        idx = jnp.min(jnp.where(score == m, blk, big), axis=axis, keepdims=True)
        hit = blk == idx
        sel = jnp.where(hit, 1.0, sel)
        score = jnp.where(hit, -jnp.inf, score)
    return sel


def _ada_kernel(c_ref, w_ref, b_ref, o_ref):
    c = c_ref[...]
    s = c * jax.nn.sigmoid(c)
    o_ref[0] = _dot(s.astype(BF16), w_ref[0].astype(BF16)) + b_ref[0]


def _ada(c_all, w_ada, b_ada):
    depth, d, n = w_ada.shape
    m = c_all.shape[0]
    tn = 1024
    return pl.pallas_call(
        _ada_kernel,
        grid=(depth, n // tn),
        in_specs=[pl.BlockSpec((m, d), lambda l, j: (0, 0)),
                  pl.BlockSpec((1, d, tn), lambda l, j: (l, 0, j)),
                  pl.BlockSpec((1, 1, tn), lambda l, j: (l, 0, j))],
        out_specs=pl.BlockSpec((1, m, tn), lambda l, j: (l, 0, j)),
        out_shape=jax.ShapeDtypeStruct((depth, m, n), F32),
        compiler_params=_cparams(("parallel", "parallel")),
        name="ada",
    )(c_all, w_ada, b_ada.reshape(depth, 1, n))


def _mod_spec(mod, tm, nargs):
    r = mod.shape[1]
    d = mod.shape[2]
    if nargs == 3:
        if r == 1:
            return pl.BlockSpec((1, 1, d), lambda b, i, f: (b, 0, 0))
        return pl.BlockSpec((1, tm, d), lambda b, i, f: (b, i, 0))
    if r == 1:
        return pl.BlockSpec((1, 1, d), lambda b, i: (b, 0, 0))
    return pl.BlockSpec((1, tm, d), lambda b, i: (b, i, 0))


def _ffn_kernel(x_ref, sh_ref, sc_ref, gt_ref, g_ref, wi_ref, wo_ref, fg_ref, o_ref, *, final):
    x = x_ref[0]
    h = _rms_mod(x, g_ref[...], sc_ref[0], sh_ref[0]).astype(BF16)
    y = jnp.zeros(x.shape, F32)
    for f in range(FFN_DIM // FFN_TF):
        g = _dot(h, wi_ref[:, f * FFN_TF:(f + 1) * FFN_TF])
        u = _dot(h, wi_ref[:, FFN_DIM + f * FFN_TF:FFN_DIM + (f + 1) * FFN_TF])
        a = (g * jax.nn.sigmoid(g) * u).astype(BF16)
        y = y + _dot(a, wo_ref[f * FFN_TF:(f + 1) * FFN_TF, :])
    x = x + 0.5 * gt_ref[0] * y
    if final:
        x = x * lax.rsqrt(jnp.mean(x * x, axis=-1, keepdims=True) + EPS) * fg_ref[...]
    o_ref[0] = x


def _ffn(x, sh, sc, gt, g, w_in, w_out, layer, tm, final_gain=None):
    bx, tx, d = x.shape
    xs = pl.BlockSpec((1, tm, d), lambda b, i: (b, i, 0))
    ms = _mod_spec(sh, tm, 2)
    once = lambda a: pl.BlockSpec(a.shape, lambda b, i: (0,) * a.ndim, pipeline_mode=pl.Buffered(1))
    of_layer = lambda a: pl.BlockSpec((None,) + a.shape[1:], lambda b, i: (layer, 0, 0), pipeline_mode=pl.Buffered(1))
    fg = g if final_gain is None else final_gain
    return pl.pallas_call(
        functools.partial(_ffn_kernel, final=final_gain is not None),
        grid=(bx, tx // tm),
        in_specs=[xs, ms, ms, ms, once(g), of_layer(w_in), of_layer(w_out), once(fg)],
        out_specs=xs,
        out_shape=jax.ShapeDtypeStruct(x.shape, F32),
        compiler_params=_cparams(("parallel", "parallel")),
        name="ffn",
    )(x, sh, sc, gt, g, w_in, w_out, fg)


def _proj_kernel(x_ref, sh_ref, sc_ref, g_ref, w_ref, wkvt_ref, wa2_ref, ba_ref, qb_ref, oh_ref, pos_ref, one_ref,
                 qn_ref, kvc_ref, kvs_ref, kvw_ref, kvct_ref, kvst_ref, kvwt_ref, ksa_ref, vsa_ref, kwa_ref, vwa_ref,
                 gn_ref, qg_ref, kg_ref, vg_ref, la_ref, og_ref):
    h = _rms_mod(x_ref[0], g_ref[...], sc_ref[0], sh_ref[0]).astype(BF16)
    p = _dot(h, w_ref[...])
    o = IN_OFF
    qn_ref[0] = (p[:, o[0]:o[1]] * (NSA_HD ** -0.5 * LOG2E) + qb_ref[...]).astype(BF16)
    kvc_ref[0] = p[:, o[1]:o[2]]
    kvs_ref[0] = p[:, o[2]:o[3]]
    kvw_ref[0] = p[:, o[3]:o[4]]
    kvt = _dot_nt(wkvt_ref[...], h)
    kvct_ref[0] = kvt[0:KV_COLS]
    kvst_ref[0] = kvt[KV_COLS:2 * KV_COLS]
    kvwt_ref[0] = kvt[2 * KV_COLS:3 * KV_COLS]
    kb = kvt.astype(BF16)
    oh, pos, one = oh_ref[...], pos_ref[...], one_ref[...]
    hd = NSA_HD
    for src, k_ref, v_ref in ((KV_COLS, ksa_ref, vsa_ref), (2 * KV_COLS, kwa_ref, vwa_ref)):
        k_parts, v_parts = [], []
        for gi in range(NSA_KV_HEADS):
            k_parts += ([oh] if k_ref is ksa_ref else []) + [kb[src + gi * hd:src + (gi + 1) * hd], pos]
            v_parts += [kb[src + (NSA_KV_HEADS + gi) * hd:src + (NSA_KV_HEADS + gi + 1) * hd], one]
        k_ref[0] = jnp.concatenate(k_parts, axis=0)
        v_ref[0] = jnp.concatenate(v_parts, axis=0)
    gn_ref[0] = jax.nn.sigmoid(p[:, o[4]:o[5]])
    qg_ref[0] = p[:, o[5]:o[6]] * (GLA_DK ** -0.5)
    kg_ref[0] = p[:, o[6]:o[7]]
    vg_ref[0] = p[:, o[7]:o[8]]
    a_pre = _dot(p[:, o[8]:o[9]].astype(BF16), wa2_ref[...]) + ba_ref[...]
    log_sig = jnp.minimum(a_pre, 0.0) - jnp.log(1.0 + jnp.exp(-jnp.abs(a_pre)))
    la_ref[0] = log_sig * (1.0 / GLA_TAU)
    og_ref[0] = p[:, o[9]:o[10]]


def _proj(x, sh, sc, g, w_all, wkvt, wa2p, ba, qb, tm):
    bx, tx, d = x.shape
    xs = pl.BlockSpec((1, tm, d), lambda b, i: (b, i, 0))
    ms = _mod_spec(sh, tm, 2)
    full = lambda shape: pl.BlockSpec(shape, lambda b, i: (0,) * len(shape))
    rows = lambda w, dt: (pl.BlockSpec((1, tm, w), lambda b, i: (b, i, 0)), jax.ShapeDtypeStruct((bx, tx, w), dt))
    cols = lambda w, dt: (pl.BlockSpec((1, w, tm), lambda b, i: (b, 0, i)), jax.ShapeDtypeStruct((bx, w, tx), dt))
    pos_np = np.arange(tx)
    nsp = max(LANES, tx // SEL_BLOCK)
    oh = jnp.asarray(np.arange(nsp)[:, None] == pos_np[None, :] // SEL_BLOCK, BF16)
    pos = jnp.pad(_alibi_k_rows(pos_np), ((0, NSA_HD - 4), (0, 0)))
    one = jnp.pad(jnp.ones((1, tx), BF16), ((0, NSA_HD - 1), (0, 0)))
    lanes = lambda a: pl.BlockSpec((a.shape[0], tm), lambda b, i: (0, i))
    g2 = NSA_KV_HEADS
    outs = [rows(IN_PAD[0], BF16), rows(KV_COLS, F32), rows(KV_COLS, F32), rows(KV_COLS, F32),
            cols(KV_COLS, F32), cols(KV_COLS, F32), cols(KV_COLS, F32),
            cols(g2 * (nsp + LANES), BF16), cols(g2 * LANES, BF16), cols(g2 * LANES, BF16), cols(g2 * LANES, BF16),
            rows(IN_PAD[4], F32), rows(256, F32), rows(256, F32), rows(GLA_WIDTH, F32), rows(256, F32),
            rows(GLA_WIDTH, F32)]
    return pl.pallas_call(
        _proj_kernel,
        grid=(bx, tx // tm),
        in_specs=[xs, ms, ms, full((1, d)), full(w_all.shape), full(wkvt.shape), full(wa2p.shape),
                  full(ba.shape), full(qb.shape), lanes(oh), lanes(pos), lanes(one)],
        out_specs=[o[0] for o in outs],
        out_shape=[o[1] for o in outs],
        compiler_params=_cparams(("parallel", "parallel")),
        name="proj",
    )(x, sh, sc, g, w_all, wkvt, wa2p, ba, qb, oh, pos, one)


def _cmp_kernel(x_ref, pelo_ref, pehi_ref, w1lo_ref, w1hi_ref, b1_ref, w2_ref, o_ref):
    x = x_ref[0]
    n = x.shape[0]
    h_lo = _dot_nt(w1lo_ref[...], (x + pelo_ref[...]).astype(BF16))
    h_hi = _dot_nt(w1hi_ref[...], (x + pehi_ref[...]).astype(BF16))
    h = h_lo + pltpu.roll(h_hi, n - 1, 1) + b1_ref[...]
    o_ref[0] = _dot(w2_ref[...], jax.nn.gelu(h).astype(BF16))


def _cmp_prompt(x16, cw):
    b, n, w = x16.shape
    full = lambda a: pl.BlockSpec(a.shape, lambda i: (0,) * a.ndim)
    args = (cw["pe_lo"], cw["pe_hi"], cw["w1lo_t"], cw["w1hi_t"], cw["b1_col"], cw["w2_t"])
    return pl.pallas_call(
        _cmp_kernel,
        grid=(b,),
        in_specs=[pl.BlockSpec((1, n, w), lambda i: (i, 0, 0))] + [full(a) for a in args],
        out_specs=pl.BlockSpec((1, KV_COLS, n), lambda i: (i, 0, 0)),
        out_shape=jax.ShapeDtypeStruct((b, KV_COLS, n), F32),
        compiler_params=_cparams(("parallel",)),
        name="cmp_prompt",
    )(x16, *args)


def _nsa_prompt_kernel(q_ref, gate_ref, ck_ref, cv_ref, ks_ref, vs_ref, kw_ref, vw_ref, ovt_ref,
                       o_ref, ids_scr, *, tq, tk, ncp, nsp):
    i = pl.program_id(2)
    q0 = i * tq
    nr = NSA_GROUP * tq
    row = q0 + jnp.bitwise_and(lax.broadcasted_iota(jnp.int32, (nr, 1), 0), tq - 1)
    row_t = q0 + lax.broadcasted_iota(jnp.int32, (tq, 1), 0)
    qblk = q_ref[0]
    q = jnp.concatenate([qblk[:, r * LANES:(r + 1) * LANES] for r in range(NSA_GROUP)], axis=0)

    def per_head(x, mask):
        return (x.reshape(NSA_GROUP, tq, x.shape[-1]) + mask[None]).reshape(x.shape)

    c_end = lax.broadcasted_iota(jnp.int32, (1, ncp), 1) * CMP_STRIDE + (CMP_BLOCK - 1)
    lg = per_head(_dot(q, ck_ref[0, 0]), jnp.where(c_end <= row_t, 0.0, NEG))
    e = jnp.exp2(lg - jnp.max(lg, axis=-1, keepdims=True))
    pc = e * jnp.where(row >= CMP_BLOCK - 1, 1.0 / jnp.sum(e, axis=-1, keepdims=True), 0.0)
    o_cmp = _dot_nt(pc.astype(BF16), cv_ref[0, 0])
    psum = pc[0:tq]
    for r in range(1, NSA_GROUP):
        psum = psum + pc[r * tq:(r + 1) * tq]
    p_hi = psum.astype(BF16)
    p_lo = (psum - p_hi.astype(F32)).astype(BF16)
    imp_t = _dot_nt(ovt_ref[...], p_hi) + _dot_nt(ovt_ref[...], p_lo)

    wl = WINDOW + tq
    w0 = pl.multiple_of(jnp.maximum(q0 - WINDOW, 0), tq)
    dist_w = row_t - (w0 + lax.broadcasted_iota(jnp.int32, (1, wl), 1))
    neg_w = jnp.where(dist_w >= 0, jnp.where(dist_w < WINDOW, 0.0, NEG), NEG)
    lg = per_head(_dot(q, kw_ref[0, 0, :, pl.ds(w0, wl)]), neg_w)
    p = jnp.exp2(lg - jnp.max(lg, axis=-1, keepdims=True)).astype(BF16)
    ow = _dot_nt(p, vw_ref[0, 0, :, pl.ds(w0, wl)])
    o_win = ow[:, :NSA_HD] / ow[:, NSA_HD:NSA_HD + 1]
    gates = gate_ref[0]
    gcol = lambda j: jnp.concatenate([gates[:, 3 * r + j:3 * r + j + 1] for r in range(NSA_GROUP)], axis=0)
    o_two = gcol(0) * o_cmp + gcol(2) * o_win

    blk = lax.broadcasted_iota(jnp.int32, (nsp, 1), 0)
    t_lane = q0 + lax.broadcasted_iota(jnp.int32, (1, tq), 1)
    cur = jnp.right_shift(t_lane, 6)
    forced = jnp.where(blk == 0, 1.0, 0.0) + jnp.where(blk == cur, 1.0, 0.0) + jnp.where(blk == cur - 1, 1.0, 0.0)
    valid = blk * SEL_BLOCK <= t_lane
    score = jnp.where(valid, imp_t + jnp.where(forced > 0.5, FORCE_BONUS, 0.0), NEG)
    sel_t = _top_k_mask(score, blk.astype(F32), SEL_TOPK, axis=0)
    sel_t = jnp.where(valid, sel_t, 0.0)
    sel_bias = ((sel_t - 1.0) * 1e30).T.astype(BF16)
    qa = jnp.concatenate([jnp.concatenate([sel_bias] * NSA_GROUP, axis=0), q], axis=1)

    n_full = q0 // tk
    per_tile = tk // SEL_BLOCK
    n_tiles = nsp // per_tile
    blk_any = jnp.max(sel_t, axis=1, keepdims=True)
    tile_any = jnp.max(blk_any.reshape(n_tiles, per_tile, 1), axis=1)
    bit = jnp.left_shift(1, lax.broadcasted_iota(jnp.int32, (n_tiles, 1), 0))
    tile_bits = jnp.sum(jnp.where(tile_any > 0.0, bit, 0))
    n_used = jnp.int32(0)
    for kt in range(min(n_tiles, ks_ref.shape[3] // tk)):
        use = jnp.logical_and(jnp.bitwise_and(jnp.right_shift(tile_bits, kt), 1) == 1, kt < n_full)
        ids_scr[n_used] = kt
        n_used = n_used + use.astype(jnp.int32)

    def sel_tile(kt, carry, diag):
        m, acc = carry
        k0 = pl.multiple_of(kt * tk, tk)
        lg = _dot(qa, ks_ref[0, 0, :, pl.ds(k0, tk)])
        if diag:
            lg = jnp.where(k0 + lax.broadcasted_iota(jnp.int32, (1, tk), 1) <= row, lg, NEG)
        m_new = jnp.maximum(m, jnp.max(lg, axis=-1, keepdims=True))
        p = jnp.exp2(lg - m_new).astype(BF16)
        return m_new, jnp.exp2(m - m_new) * acc + _dot_nt(p, vs_ref[0, 0, :, pl.ds(k0, tk)])

    carry = sel_tile(n_full, (jnp.full((nr, 1), NEG, F32), jnp.zeros((nr, LANES), F32)), True)

    def two_tiles(j, c):
        return sel_tile(ids_scr[2 * j + 1], sel_tile(ids_scr[2 * j], c, False), False)

    carry = lax.fori_loop(0, n_used // 2, two_tiles, carry)
    _, acc = lax.cond(n_used % 2 == 1, lambda c: sel_tile(ids_scr[n_used - 1], c, False), lambda c: c, carry)
    o_slc = acc[:, :NSA_HD] / acc[:, NSA_HD:NSA_HD + 1]
    o = o_two + gcol(1) * o_slc
    o_ref[0] = jnp.concatenate([o[r * tq:(r + 1) * tq] for r in range(NSA_GROUP)], axis=1)


def _overlap_matrix(ncp, nsp):
    c = np.arange(ncp)[:, None] * CMP_STRIDE
    s = np.arange(nsp)[None, :] * SEL_BLOCK
    return jnp.asarray((c <= s + SEL_BLOCK - 1) & (c + CMP_BLOCK - 1 >= s), dtype=BF16)


def _bf16_round(x):
    return np.asarray(x, dtype=BF16).astype(np.float32)


def _q_bias():
    slopes = np.exp2(-8.0 * np.arange(1, NSA_HEADS + 1) / NSA_HEADS)
    c_hi = float(_bf16_round(LOG2E))
    c_lo = float(_bf16_round(LOG2E - c_hi))
    qb = np.zeros((NSA_HEADS, LANES), np.float32)
    qb[:, NSA_HD:NSA_HD + 4] = np.stack([c_hi * slopes * SEL_BLOCK, c_lo * slopes * SEL_BLOCK,
                                         c_hi * slopes, c_lo * slopes], axis=1)
    return jnp.asarray(qb.reshape(1, NSA_HEADS * LANES))


def _alibi_k_rows(pos):
    pos = np.asarray(pos)
    return jnp.asarray(np.stack([pos // SEL_BLOCK, pos // SEL_BLOCK, pos % SEL_BLOCK, pos % SEL_BLOCK]), BF16)


def _aug_rows(parts, total, lead):
    parts = [jnp.broadcast_to(p, lead + p.shape[-2:]) for p in parts]
    used = sum(p.shape[-2] for p in parts)
    if total > used:
        parts.append(jnp.zeros(lead + (total - used, parts[0].shape[-1]), BF16))
    return jnp.concatenate(parts, axis=-2)


def _nsa_prompt(q_pad, gates, ck, cv, ks, vs, kw, vw, tq, tk):
    b, t, _ = q_pad.shape
    g, r = NSA_KV_HEADS, NSA_GROUP
    ncp = ck.shape[3]
    nsp = max(LANES, t // SEL_BLOCK)
    kv = lambda a: pl.BlockSpec((1, 1) + a.shape[2:], lambda bi, gi, i: (bi, gi, 0, 0))
    ovt = _overlap_matrix(ncp, nsp).T
    return pl.pallas_call(
        functools.partial(_nsa_prompt_kernel, tq=tq, tk=tk, ncp=ncp, nsp=nsp),
        grid=(b, g, t // tq),
        in_specs=[pl.BlockSpec((1, tq, r * LANES), lambda bi, gi, i: (bi, i, gi)),
                  pl.BlockSpec((1, tq, LANES), lambda bi, gi, i: (bi, i, gi)),
                  kv(ck), kv(cv), kv(ks), kv(vs), kv(kw), kv(vw),
                  pl.BlockSpec((nsp, ncp), lambda bi, gi, i: (0, 0))],
        out_specs=pl.BlockSpec((1, tq, r * NSA_HD), lambda bi, gi, i: (bi, i, gi)),
        out_shape=jax.ShapeDtypeStruct((b, t, NSA_WIDTH), F32),
        scratch_shapes=[pltpu.SMEM((nsp // (tk // SEL_BLOCK),), jnp.int32)],
        compiler_params=_cparams(("parallel", "parallel", "arbitrary")),
        name="nsa_prompt",
    )(q_pad, gates, ck, cv, ks, vs, kw, vw, ovt)


def _gla_prompt_kernel(q_ref, k_ref, la_ref, v_ref, o_ref, s_ref, s_scr, *, nch):
    c = GLA_CHUNK
    ti = pl.program_id(2)

    @pl.when(ti == 0)
    def _():
        s_scr[...] = jnp.zeros_like(s_scr)

    two_dk, two_dv = 2 * GLA_DK, 2 * GLA_DV
    r_i = lax.broadcasted_iota(jnp.int32, (c, c), 0)
    c_i = lax.broadcasted_iota(jnp.int32, (c, c), 1)
    tril = jnp.where(c_i <= r_i, 1.0, 0.0).astype(BF16)
    e_r = lax.broadcasted_iota(jnp.int32, (two_dk, two_dk), 0)
    e_c = lax.broadcasted_iota(jnp.int32, (two_dk, two_dk), 1)
    eye = jnp.where(e_r == e_c, 1.0, 0.0).astype(BF16)
    b_r = lax.broadcasted_iota(jnp.int32, (two_dk, two_dv), 0) // GLA_DK
    b_c = lax.broadcasted_iota(jnp.int32, (two_dk, two_dv), 1) // GLA_DV
    same_head = jnp.where(b_r == b_c, 1.0, 0.0)
    ones_bd = same_head.astype(BF16)
    head0 = lax.broadcasted_iota(jnp.int32, (1, two_dk), 1) < GLA_DK
    pos = lax.broadcasted_iota(jnp.int32, (c, 1), 0)
    nsub = c // GLA_SUB
    w_i = lax.broadcasted_iota(jnp.int32, (1, GLA_SUB, 1), 1)

    def chunk(ci):
        rows = pl.ds(pl.multiple_of(ci * c, c), c)
        q = q_ref[0, rows, :]
        k = k_ref[0, rows, :]
        la = la_ref[0, rows, :]
        v = v_ref[0, rows, :]
        vb = v.astype(BF16)
        s0 = s_scr[...]
        a1, a2, a3 = _split3(la)
        cum = _dot(tril, a1) + _dot(tril, a2) + _dot(tril, a3)
        last = cum[c - 1:c, :]
        o = _dot((q * jnp.exp(cum)).astype(BF16), s0.astype(BF16))
        attn = jnp.zeros((2 * c, c), F32)
        for j in range(nsub - 1):
            ce = cum[GLA_SUB * (j + 1) - 1:GLA_SUB * (j + 1), :]
            qh = jnp.where(pos >= GLA_SUB * (j + 1), q * jnp.exp(jnp.minimum(cum - ce, 0.0)), 0.0)
            in_j = jnp.where(pos >= GLA_SUB * j, jnp.where(pos < GLA_SUB * (j + 1), 1.0, 0.0), 0.0)
            kh = in_j * (k * jnp.exp(jnp.minimum(ce - cum, 0.0)))
            qh2 = jnp.concatenate([jnp.where(head0, qh, 0.0), jnp.where(head0, 0.0, qh)], axis=0)
            attn = attn + _dot_nt(qh2.astype(BF16), kh.astype(BF16))
        o_off = _dot(attn.astype(BF16), vb)
        o = o + jnp.concatenate([o_off[:c, :GLA_DV], o_off[c:, GLA_DV:]], axis=1)
        q3 = q.reshape(nsub, GLA_SUB, two_dk)
        k3 = k.reshape(nsub, GLA_SUB, two_dk)
        c3 = cum.reshape(nsub, GLA_SUB, two_dk)
        v3 = v.reshape(nsub, GLA_SUB, two_dv)
        ws = []
        for u in range(GLA_SUB):
            w = q3 * k3[:, u:u + 1, :] * jnp.exp(jnp.minimum(c3 - c3[:, u:u + 1, :], 0.0))
            ws.append(jnp.where(w_i >= u, w, 0.0).reshape(c, two_dk).astype(BF16))
        a_rep = _dot(jnp.concatenate(ws, axis=0), ones_bd)
        od = jnp.zeros((nsub, GLA_SUB, two_dv), F32)
        for u in range(GLA_SUB):
            od = od + a_rep[u * c:(u + 1) * c].reshape(nsub, GLA_SUB, two_dv) * v3[:, u:u + 1, :]
        o_ref[0, rows, :] = o + od.reshape(c, two_dv)
        kd = (k * jnp.exp(last - cum)).astype(BF16)
        kd_t = _dot_nt(eye, kd).astype(BF16)
        l1, l2, l3 = _split3(jnp.broadcast_to(last, (8, two_dk)))
        last_col = (_dot_nt(eye, l1) + _dot_nt(eye, l2) + _dot_nt(eye, l3))[:, 0:1]
        s_scr[...] = jnp.exp(last_col) * s0 + same_head * _dot(kd_t, vb)

    def several(i, _):
        for j in range(GLA_UNROLL):
            chunk(GLA_UNROLL * i + j)
        return 0

    lax.fori_loop(0, nch // GLA_UNROLL, several, 0)

    @pl.when(ti == pl.num_programs(2) - 1)
    def _():
        s = s_scr[...]
        s_ref[0, 0] = s[:GLA_DK, :GLA_DV]
        s_ref[0, 1] = s[GLA_DK:, GLA_DV:]


def _gla_prompt(q, k, la, v, nch):
    b, t, _ = q.shape
    tc = nch * GLA_CHUNK
    hp = GLA_HEADS // 2
    qs = pl.BlockSpec((1, tc, 2 * GLA_DK), lambda bi, pi, ti: (bi, ti, pi))
    vs = pl.BlockSpec((1, tc, 2 * GLA_DV), lambda bi, pi, ti: (bi, ti, pi))
    return pl.pallas_call(
        functools.partial(_gla_prompt_kernel, nch=nch),
        grid=(b, hp, t // tc),
        in_specs=[qs, qs, qs, vs],
        out_specs=[vs, pl.BlockSpec((1, 2, GLA_DK, GLA_DV), lambda bi, pi, ti: (bi, pi, 0, 0))],
        out_shape=[jax.ShapeDtypeStruct((b, t, GLA_WIDTH), F32),
                   jax.ShapeDtypeStruct((b, GLA_HEADS, GLA_DK, GLA_DV), F32)],
        scratch_shapes=[pltpu.VMEM((2 * GLA_DK, 2 * GLA_DV), F32)],
        compiler_params=_cparams(("parallel", "parallel", "arbitrary")),
        name="gla_prompt",
    )(q, k, la, v)


def _out_kernel(x_ref, gt_ref, on_ref, ogla_ref, og_ref, gn_ref, w_ref, o_ref):
    y = _dot(on_ref[0].astype(BF16), w_ref[:NSA_WIDTH, :])
    og = og_ref[0]
    ogla = ogla_ref[0]
    heads = []
    for h in range(GLA_HEADS):
        o = ogla[:, h * GLA_DV:(h + 1) * GLA_DV]
        o = o * lax.rsqrt(jnp.mean(o * o, axis=-1, keepdims=True) + EPS) * gn_ref[...]
        gate = og[:, h * GLA_DV:(h + 1) * GLA_DV]
        heads.append((o * (gate * jax.nn.sigmoid(gate))).astype(BF16))
    y = y + _dot(jnp.concatenate(heads, axis=1), w_ref[NSA_WIDTH:, :])
    o_ref[0] = x_ref[0] + gt_ref[0] * y


def _out_proj(x, gt, o_nsa, o_gla, o_g, gla_norm, w_out, tm):
    bx, tx, d = x.shape
    xs = pl.BlockSpec((1, tm, d), lambda b, i: (b, i, 0))
    return pl.pallas_call(
        _out_kernel,
        grid=(bx, tx // tm),
        in_specs=[xs, _mod_spec(gt, tm, 2),
                  pl.BlockSpec((1, tm, NSA_WIDTH), lambda b, i: (b, i, 0)),
                  pl.BlockSpec((1, tm, GLA_WIDTH), lambda b, i: (b, i, 0)),
                  pl.BlockSpec((1, tm, GLA_WIDTH), lambda b, i: (b, i, 0)),
                  pl.BlockSpec((1, GLA_DV), lambda b, i: (0, 0)),
                  pl.BlockSpec(w_out.shape, lambda b, i: (0, 0))],
        out_specs=xs,
        out_shape=jax.ShapeDtypeStruct(x.shape, F32),
        compiler_params=_cparams(("parallel", "parallel")),
        name="out_proj",
    )(x, gt, o_nsa, o_gla, o_g, gla_norm, w_out)


def _nsa_sample_kernel(pt_ref, *rest, n_pages, page, past, wlen, nseq):
    del pt_ref
    per = 6 + 2 * n_pages
    (w1_ref, pe_ref, b1_ref, w2_ref, ovt_ref, o_ref, x_scr, bias_scr) = rest[nseq * per:]

    @pl.when(pl.program_id(0) == 0)
    def _():
        for kv in range(2):
            acc = jnp.zeros((8, 2 * LANES), F32)
            for s2 in range(CMP_BLOCK // 2):
                pe2 = jnp.broadcast_to(pe_ref[kv, s2:s2 + 1, :], (8, 2 * LANES))
                acc = acc + _dot(pe2.astype(BF16), w1_ref[kv, s2])
            bias_scr[kv] = acc + b1_ref[kv]

    ncb = past // CMP_STRIDE
    nrow = ncb + 8
    cpp = page // CMP_STRIDE
    r_i = lax.broadcasted_iota(jnp.int32, (page, page), 0)
    p_i = lax.broadcasted_iota(jnp.int32, (page, page), 1)
    in_chunk = jnp.bitwise_and(p_i, CMP_STRIDE - 1)
    perm_t = jnp.where(r_i * CMP_STRIDE == in_chunk * (cpp * CMP_STRIDE) + (p_i - in_chunk), 1.0, 0.0).astype(BF16)
    first = lax.broadcasted_iota(jnp.int32, (8, 1), 0) == 0
    seqs = [rest[s * per:(s + 1) * per] for s in range(nseq)]
    for s, refs in enumerate(seqs):
        base = s * nrow
        for kv in range(2):
            for pg in range(n_pages):
                rows = _dot_nt(perm_t, refs[5 + pg][0, 0, kv].reshape(LANES, page).astype(BF16))
                for off in range(CMP_STRIDE):
                    x_scr[kv, off, base + pg * cpp:base + (pg + 1) * cpp, :] = rows[off * cpp:(off + 1) * cpp, :]
            new = refs[2][0][:, kv * LANES:(kv + 1) * LANES]
            for off in range(CMP_STRIDE):
                x_scr[kv, off, base + ncb:base + nrow, :] = (jnp.where(first, new, 0.0) if off == 0
                                                             else jnp.zeros((8, LANES), F32))
    ckvs = [[None, None] for _ in range(nseq)]
    half = CMP_STRIDE // 2
    for kv in range(2):
        acc_lo = jnp.zeros((nseq * nrow, 2 * LANES), F32)
        acc_hi = jnp.zeros((nseq * nrow, 2 * LANES), F32)
        for j in range(half):
            lhs = jnp.concatenate([x_scr[kv, 2 * j], x_scr[kv, 2 * j + 1]], axis=1).astype(BF16)
            acc_lo = acc_lo + _dot(lhs, w1_ref[kv, j])
            acc_hi = acc_hi + _dot(lhs, w1_ref[kv, half + j])
        hs = [jax.nn.gelu(acc_lo[s * nrow:s * nrow + ncb]
                          + pltpu.roll(acc_hi[s * nrow:(s + 1) * nrow], nrow - 1, 0)[:ncb] + bias_scr[kv][0:1])
              for s in range(nseq)]
        ckv = _dot(jnp.concatenate(hs, axis=0).astype(BF16), w2_ref[kv]).astype(BF16)
        for s in range(nseq):
            ckvs[s][kv] = ckv[s * ncb:(s + 1) * ncb]

    for s, refs in enumerate(seqs):
        _nsa_sample_seq(refs[:5], ckvs[s], refs[5 + n_pages:5 + 2 * n_pages], refs[5 + 2 * n_pages],
                        ovt_ref, o_ref.at[s], n_pages=n_pages, page=page, past=past, wlen=wlen)


def _nsa_sample_seq(row_refs, ckv, slc_pages, cw_ref, ovt_ref, o_ref, *, n_pages, page, past, wlen):
    qall_ref, gate_ref, _, kvs_ref, kvw_ref = row_refs
    nh = NSA_HEADS
    qa = qall_ref[0]
    qa_f = qa.astype(F32)
    hrow = lax.broadcasted_iota(jnp.int32, (nh, 1), 0)
    slope = jnp.exp2(-(hrow + 1).astype(F32)) * LOG2E
    lane = lax.broadcasted_iota(jnp.int32, (1, LANES), 1)
    c_end = lane * CMP_STRIDE + (CMP_BLOCK - 1)
    dist_c = past - c_end
    mask_c = dist_c >= 0
    lg_c = jnp.where(mask_c, _dot_nt(qa, ckv[0]) - slope * dist_c.astype(F32), NEG)
    e_c = jnp.where(mask_c, jnp.exp2(lg_c - jnp.max(lg_c, axis=-1, keepdims=True)), 0.0)
    pc = e_c / jnp.maximum(jnp.sum(e_c, axis=-1, keepdims=True), 1e-30)
    o_cmp = _dot(pc.astype(BF16), ckv[1])

    ps0 = jnp.sum(jnp.where(hrow < NSA_GROUP, pc, 0.0), axis=0, keepdims=True)
    ps1 = jnp.sum(jnp.where(hrow >= NSA_GROUP, pc, 0.0), axis=0, keepdims=True)
    psum = jnp.where(hrow == 0, ps0, jnp.where(hrow == 1, ps1, 0.0))
    psum = jnp.concatenate([psum, jnp.zeros((LANES - nh, LANES), F32)], axis=0)
    p_hi = psum.astype(BF16)
    p_lo = (psum - p_hi.astype(F32)).astype(BF16)
    imp_t = _dot_nt(ovt_ref[...], p_hi) + _dot_nt(ovt_ref[...], p_lo)
    blk = lax.broadcasted_iota(jnp.int32, (LANES, 1), 0)
    cur = past // SEL_BLOCK
    forced = jnp.where(blk == 0, 1.0, 0.0) + jnp.where(blk == cur, 1.0, 0.0) + jnp.where(blk == cur - 1, 1.0, 0.0)
    score = jnp.where(blk * SEL_BLOCK <= past, imp_t + jnp.where(forced > 0.5, FORCE_BONUS, 0.0), NEG)
    sel = _top_k_mask(score, blk.astype(F32), SEL_TOPK, axis=0).T
    sel_h = jnp.where(hrow < NSA_GROUP, sel[0:1], sel[1:2])

    def attend(tiles, new_row, new_neg):
        lgs = [_dot(qa, k_t) + neg for k_t, _, neg in tiles]
        lg_new = jnp.sum(qa_f * new_row[:, :LANES], axis=-1, keepdims=True) + new_neg
        m = lg_new
        for lg in lgs:
            m = jnp.maximum(m, jnp.max(lg, axis=-1, keepdims=True))
        p_new = jnp.exp2(lg_new - m)
        l = p_new
        acc = p_new * new_row[:, LANES:]
        for lg, (_, v_t, _) in zip(lgs, tiles):
            p = jnp.exp2(lg - m)
            l = l + jnp.sum(p, axis=-1, keepdims=True)
            acc = acc + _dot_nt(p.astype(BF16), v_t)
        return acc / l

    per = page // SEL_BLOCK
    tiles = []
    for pg in range(n_pages):
        blkp = slc_pages[pg]
        dist = past - (pg * page + lane)
        selx = jnp.zeros((nh, LANES), F32)
        for u in range(per):
            in_u = jnp.where(lane >= u * SEL_BLOCK, jnp.where(lane < (u + 1) * SEL_BLOCK, 1.0, 0.0), 0.0)
            selx = selx + in_u * sel_h[:, pg * per + u:pg * per + u + 1]
        neg = jnp.where(dist >= 0, (selx - 1.0) * 1e30, NEG) - slope * dist.astype(F32)
        tiles.append((blkp[0, 0, 0].reshape(LANES, page).astype(BF16),
                      blkp[0, 0, 1].reshape(LANES, page).astype(BF16), neg))
    o_slc = attend(tiles, kvs_ref[0], (sel_h[:, cur:cur + 1] - 1.0) * 1e30)

    wlane = lax.broadcasted_iota(jnp.int32, (1, wlen), 1)
    w_pos = past - wlen + wlane
    dist = past - w_pos
    ok = jnp.where(dist >= 0, jnp.where(dist < WINDOW, jnp.where(w_pos >= 0, 1.0, 0.0), 0.0), 0.0)
    neg = (ok - 1.0) * 1e30 - slope * dist.astype(F32)
    tiles = [(cw_ref[0, 0, 0].reshape(LANES, wlen).astype(BF16), cw_ref[0, 0, 1].reshape(LANES, wlen).astype(BF16), neg)]
    o_win = attend(tiles, kvw_ref[0], jnp.zeros((nh, 1), F32))

    gates = gate_ref[0]
    pick = lambda a: jnp.where(hrow < NSA_GROUP, a[:, :NSA_HD], a[:, NSA_HD:])
    o_ref[...] = gates[:, 0:1] * pick(o_cmp) + gates[:, 1:2] * pick(o_slc) + gates[:, 2:3] * pick(o_win)


def _nsa_sample(layer, page_table, q_all, gates, kvc, kvs, kvw, cache_c, cache_s, cache_w, cw):
    nb, n_pages = page_table.shape
    depth, n_phys, page = cache_c.shape[:3]
    past = n_pages * page
    wlen = cache_w.shape[2]
    ncb = past // CMP_STRIDE
    cc = cache_c.transpose(0, 1, 3, 4, 5, 2)
    cs = cache_s.transpose(0, 1, 3, 4, 5, 2)
    cwin = cache_w.transpose(0, 1, 3, 4, 5, 2)
    nseq = NSA_SAMPLE_SEQS if nb % NSA_SAMPLE_SEQS == 0 else 1
    full = lambda a: pl.BlockSpec(a.shape, lambda b, pt: (0,) * a.ndim)
    ovt = _overlap_matrix(ncb, LANES).T
    wargs = (cw["w1s"], cw["pes"], cw["b1s"], cw["w2s"], ovt)
    in_specs, args = [], []
    for s in range(nseq):
        seq = lambda b, s=s: nseq * b + s
        heads = pl.BlockSpec((1, NSA_HEADS, LANES), lambda b, pt, seq=seq: (seq(b), 0, 0))
        row = pl.BlockSpec((1, 1, KV_COLS), lambda b, pt, seq=seq: (seq(b), 0, 0))
        pages = [pl.BlockSpec((1, 1, 2, NSA_KV_HEADS, NSA_HD, page),
                              lambda b, pt, seq=seq, j=j: (layer, pt[seq(b), j], 0, 0, 0, 0)) for j in range(n_pages)]
        win = pl.BlockSpec((1, 1, 2, NSA_KV_HEADS, NSA_HD, wlen), lambda b, pt, seq=seq: (layer, seq(b), 0, 0, 0, 0))
        in_specs += [heads, heads, row, row, row] + pages + pages + [win]
        args += [q_all, gates, kvc, kvs, kvw] + [cc] * n_pages + [cs] * n_pages + [cwin]
    grid_spec = pltpu.PrefetchScalarGridSpec(
        num_scalar_prefetch=1,
        grid=(nb // nseq,),
        in_specs=in_specs + [full(a) for a in wargs],
        out_specs=pl.BlockSpec((nseq, NSA_HEADS, NSA_HD), lambda b, pt: (b, 0, 0)),
        scratch_shapes=[pltpu.VMEM((2, CMP_STRIDE, nseq * (ncb + 8), LANES), F32), pltpu.VMEM((2, 8, 2 * LANES), F32)],
    )
    return pl.pallas_call(
        functools.partial(_nsa_sample_kernel, n_pages=n_pages, page=page, past=past, wlen=wlen, nseq=nseq),
        grid_spec=grid_spec,
        out_shape=jax.ShapeDtypeStruct((nb, NSA_HEADS, NSA_HD), F32),
        compiler_params=_cparams(("arbitrary",)),
        name="nsa_sample",
    )(page_table, *args, *wargs)


def _gla_sample_kernel(col_ref, v_ref, s_ref, o_ref, sn_ref, *, bb):
    for b in range(bb):
        cols = col_ref[b]
        for h in range(GLA_HEADS):
            a = jnp.exp(cols[:, h:h + 1])
            k = cols[:, GLA_HEADS + h:GLA_HEADS + h + 1]
            q = cols[:, 2 * GLA_HEADS + h:2 * GLA_HEADS + h + 1]
            v = v_ref[b, h:h + 1, :]
            s0 = s_ref[b, h]
            qk = jnp.sum(q * k, axis=0, keepdims=True)
            o_ref[b, h:h + 1, :] = jnp.sum((q * a) * s0, axis=0, keepdims=True) + qk * v
            sn_ref[b, h] = a * s0 + k * v


def _gla_sample(cols, v, state, bb):
    nb = v.shape[0]
    return pl.pallas_call(
        functools.partial(_gla_sample_kernel, bb=bb),
        grid=(nb // bb,),
        in_specs=[pl.BlockSpec((bb, GLA_DK, 16), lambda i: (i, 0, 0)),
                  pl.BlockSpec((bb, GLA_HEADS, GLA_DV), lambda i: (i, 0, 0)),
                  pl.BlockSpec((bb, GLA_HEADS, GLA_DK, GLA_DV), lambda i: (i, 0, 0, 0))],
        out_specs=[pl.BlockSpec((bb, GLA_HEADS, GLA_DV), lambda i: (i, 0, 0)),
                   pl.BlockSpec((bb, GLA_HEADS, GLA_DK, GLA_DV), lambda i: (i, 0, 0, 0))],
        out_shape=[jax.ShapeDtypeStruct((nb, GLA_HEADS, GLA_DV), F32),
                   jax.ShapeDtypeStruct((nb, GLA_HEADS, GLA_DK, GLA_DV), F32)],
        compiler_params=_cparams(("parallel",)),
        name="gla_sample",
    )(cols, v, state)


def _proj_weight(w):
    d = w.shape[0]
    offs = np.cumsum((0,) + IN_SIZES)
    seg = lambda i: w[:, offs[i]:offs[i + 1]]

    def slots(a, n):
        per = a.shape[1] // n
        return jnp.pad(a.reshape(d, n, per), ((0, 0), (0, 0), (0, LANES - per))).reshape(d, n * LANES)

    return jnp.concatenate([slots(seg(0), NSA_HEADS), seg(1), seg(2), seg(3), slots(seg(4), NSA_KV_HEADS),
                            seg(5), seg(6), seg(7), slots(seg(8), 1), seg(9)], axis=1)


def _cmp_weights(pe, w1, b1, w2):
    g = NSA_KV_HEADS
    eye = jnp.eye(2, dtype=F32)
    eg = jnp.eye(g, dtype=F32)
    w1r = w1.reshape(2, 2, CMP_STRIDE, NSA_HD, CMP_HIDDEN)
    big = jnp.einsum("krsdh,ka,gb->rskgdabh", w1r, eye, eg).reshape(2, CMP_STRIDE * KV_COLS, 2 * g * CMP_HIDDEN)
    w2big = jnp.einsum("khd,ka,gb->kghabd", w2, eye, eg).reshape(2 * g * CMP_HIDDEN, KV_COLS)
    per = pe.reshape(2, 2, CMP_STRIDE, NSA_HD)
    pet = jnp.broadcast_to(per.transpose(1, 2, 0, 3)[:, :, :, None, :], (2, CMP_STRIDE, 2, g, NSA_HD))
    pet = pet.reshape(2, 1, CMP_STRIDE * KV_COLS)
    b1big = jnp.broadcast_to(b1[:, None, :], (2, g, CMP_HIDDEN)).reshape(1, 2 * g * CMP_HIDDEN)
    w1s = jnp.einsum("ksdh,gb->ksgdbh", w1.reshape(2, CMP_BLOCK, NSA_HD, CMP_HIDDEN), eg)
    w1s = w1s.reshape(2, CMP_BLOCK // 2, 2 * g * NSA_HD, g * CMP_HIDDEN)
    w2s = jnp.einsum("khd,gb->kghbd", w2, eg).reshape(2, g * CMP_HIDDEN, g * NSA_HD)
    pes = jnp.broadcast_to(pe[:, :, None, :], (2, CMP_BLOCK, g, NSA_HD)).reshape(2, CMP_BLOCK // 2, 2 * g * NSA_HD)
    b1s = jnp.broadcast_to(b1[:, None, None, :], (2, 1, g, CMP_HIDDEN)).reshape(2, 1, g * CMP_HIDDEN)
    return {
        "w1s": w1s.astype(BF16), "w2s": w2s.astype(BF16), "pes": pes, "b1s": b1s,
        "w1lo_t": big[0].T.astype(BF16), "w1hi_t": big[1].T.astype(BF16),
        "pe_lo": pet[0], "pe_hi": pet[1], "b1_col": b1big.T, "w2_t": w2big.T.astype(BF16),
    }


def _mods(mod, rows):
    m = mod.reshape(mod.shape[0], N_MOD, D_MODEL)
    if rows == 1:
        return [m[:, j][:, None, :] for j in range(N_MOD)]
    return [m[:, j][None, :, :] for j in range(N_MOD)]


def _layer_prompt(x, mods, lw, tm, tq, tk, final_gain):
    b, t, _ = x.shape
    x = _ffn(x, mods[0], mods[1], mods[2], lw["norm_g"][0:1], lw["f1_in"], lw["f1_out"], lw["layer"], tm)
    (qn, kvc, _, _, kvct, kvst, kvwt, ks, vs, kw, vw, gn, qg, kg, vg, la, og) = _proj(
        x, mods[3], mods[4], lw["norm_g"][1:2], lw["w_all"], lw["wkvt"], lw["wa2p"], lw["ba"], lw["qb"], tm)
    g = NSA_KV_HEADS
    ncp = t // CMP_STRIDE
    ckvt = _cmp_prompt(kvc.reshape(b, ncp, CMP_STRIDE * KV_COLS), lw["cmp"]).astype(BF16)
    ckvt = ckvt.reshape(b, 2, g, NSA_HD, ncp)
    ck = _aug_rows([ckvt[:, 0], _alibi_k_rows(np.arange(ncp) * CMP_STRIDE + CMP_BLOCK - 1)], LANES, (b, g))
    per_group = lambda a: a.reshape(b, g, a.shape[1] // g, t)
    o_nsa = _nsa_prompt(qn, gn, ck, ckvt[:, 1], per_group(ks), per_group(vs), per_group(kw), per_group(vw), tq, tk)
    o_gla, s_fin = _gla_prompt(qg, kg, la, vg, min(8, t // GLA_CHUNK))
    x = _out_proj(x, mods[5], o_nsa, o_gla, og, lw["gla_norm"], lw["w_out"], tm)
    x = _ffn(x, mods[6], mods[7], mods[8], lw["norm_g"][2:3], lw["f2_in"], lw["f2_out"], lw["layer"], tm, final_gain)
    win = min(WINDOW, t)
    back = lambda a: a.reshape(b, 2, NSA_KV_HEADS, NSA_HD, a.shape[-1]).transpose(0, 4, 1, 2, 3)
    return x, (back(kvct), back(kvst), back(kvwt[:, :, t - win:]), s_fin)


def _layer_sample(layer, x, mods, lw, cache_c, cache_s, cache_w, state, page_table, final_gain):
    nb = x.shape[1]
    x = _ffn(x, mods[0], mods[1], mods[2], lw["norm_g"][0:1], lw["f1_in"], lw["f1_out"], lw["layer"], nb)
    (qn, kvc, kvs, kvw, kvct, kvst, kvwt, _, _, _, _, gn, qg, kg, vg, la, og) = _proj(
        x, mods[3], mods[4], lw["norm_g"][1:2], lw["w_all"], lw["wkvt"], lw["wa2p"], lw["ba"], lw["qb"], nb)
    q4 = qn[0].reshape(nb, NSA_KV_HEADS, NSA_GROUP, LANES)[..., :NSA_HD]
    q_all = (q4[:, :, :, None, :] * jnp.eye(NSA_KV_HEADS, dtype=BF16)[None, :, None, :, None]).reshape(nb, NSA_HEADS, LANES)
    gates = gn[0].reshape(nb, NSA_KV_HEADS, LANES)[..., :3 * NSA_GROUP].reshape(nb, NSA_HEADS, 3)
    gates = jnp.pad(gates, ((0, 0), (0, 0), (0, LANES - 3)))
    r3 = lambda a: a[0][:, None, :]
    o_nsa = _nsa_sample(layer, page_table, q_all, gates, r3(kvc), r3(kvs), r3(kvw),
                        cache_c, cache_s, cache_w, lw["cmp"])
    o_nsa = o_nsa.reshape(1, nb, NSA_WIDTH)
    col = lambda a: a[0].reshape(nb, GLA_HEADS, GLA_DK).transpose(0, 2, 1)
    cols = jnp.concatenate([col(la), col(kg), col(qg), jnp.zeros((nb, GLA_DK, 16 - 3 * GLA_HEADS), F32)], axis=-1)
    o_gla, s_new = _gla_sample(cols, vg[0].reshape(nb, GLA_HEADS, GLA_DV), state, 8)
    o_gla = o_gla.reshape(1, nb, GLA_WIDTH)
    x = _out_proj(x, mods[5], o_nsa, o_gla, og, lw["gla_norm"], lw["w_out"], nb)
    x = _ffn(x, mods[6], mods[7], mods[8], lw["norm_g"][2:3], lw["f2_in"], lw["f2_out"], lw["layer"], nb, final_gain)
    kv5 = lambda a: a[0].reshape(2, NSA_KV_HEADS, NSA_HD, nb).transpose(3, 0, 1, 2)[:, None]
    return x, (kv5(kvct), kv5(kvst), kv5(kvwt), s_new)


def kernel(x_prompt, x_sample, c_prompt, c_sample, cache_kv_cmp, cache_kv_slc, cache_kv_win, state_gla, page_table, w_ada, b_ada, norm_g, ffn1_w_in, ffn1_w_out, w_in, cmp_pe, cmp_w1, cmp_b1, cmp_w2, gla_wa2, gla_ba, gla_norm, w_out, ffn2_w_in, ffn2_w_out, final_norm):
    depth = w_ada.shape[0]
    b, t, d = x_prompt.shape
    nb = x_sample.shape[0]
    bp = -(-b // 8) * 8
    c_all = jnp.concatenate([c_prompt, jnp.zeros((bp - b, d), F32), c_sample], axis=0)
    mod_all = _ada(c_all, w_ada, b_ada)

    tm = min(512, t)
    tq = min(256, t)
    tk = min(512, t)
    xp = x_prompt
    xs = x_sample.reshape(1, nb, d)
    outs_p, outs_s = [], []
    f1_in, f1_out = ffn1_w_in.astype(BF16), ffn1_w_out.astype(BF16)
    f2_in, f2_out = ffn2_w_in.astype(BF16), ffn2_w_out.astype(BF16)
    for l in range(depth):
        lw = {
            "norm_g": norm_g[l],
            "layer": l, "f1_in": f1_in, "f1_out": f1_out, "f2_in": f2_in, "f2_out": f2_out,
            "w_all": _proj_weight(w_in[l]).astype(BF16),
            "qb": _q_bias(),
            "wkvt": w_in[l][:, NSA_WIDTH:NSA_WIDTH + 3 * KV_COLS].T.astype(BF16),
            "wa2p": jnp.pad(gla_wa2[l], ((0, 128 - GLA_RANK), (0, 0))).astype(BF16),
            "ba": gla_ba[l][None, :],
            "gla_norm": gla_norm[l][None, :],
            "w_out": w_out[l].astype(BF16),
            "cmp": _cmp_weights(cmp_pe[l], cmp_w1[l], cmp_b1[l], cmp_w2[l]),
        }
        fg = final_norm[None, :] if l == depth - 1 else None
        xp, st_p = _layer_prompt(xp, _mods(mod_all[l, :b], 1), lw, tm, tq, tk, fg)
        xs, st_s = _layer_sample(l, xs, _mods(mod_all[l, bp:], nb), lw, cache_kv_cmp, cache_kv_slc,
                                 cache_kv_win, state_gla[l], page_table, fg)
        outs_p.append(st_p)
        outs_s.append(st_s)
    stack = lambda outs, j: jnp.stack([o[j] for o in outs])
    return (xp, xs.reshape(nb, 1, d),
            stack(outs_p, 0), stack(outs_p, 1), stack(outs_p, 2), stack(outs_p, 3),
            stack(outs_s, 0), stack(outs_s, 1), stack(outs_s, 2), stack(outs_s, 3))
```

```python
import functools

import numpy as np
import jax
import jax.numpy as jnp
from jax import lax
from jax.experimental import pallas as pl
from jax.experimental.pallas import tpu as pltpu

F32 = jnp.float32
BF16 = jnp.bfloat16

D_MODEL = 1024
NSA_HD = 64
NSA_HEADS = 8
NSA_KV_HEADS = 2
NSA_GROUP = 4
NSA_WIDTH = NSA_HEADS * NSA_HD
CMP_BLOCK = 32
CMP_STRIDE = 16
CMP_HIDDEN = 2 * NSA_HD
SEL_BLOCK = 64
SEL_TOPK = 16
WINDOW = 512
FORCE_BONUS = 1000.0
GLA_HEADS = 4
GLA_DV = 128
GLA_DK = 64
GLA_WIDTH = GLA_HEADS * GLA_DV
GLA_RANK = 16
GLA_TAU = 16.0
GLA_CHUNK = 64
GLA_SUB = 16
FFN_DIM = 2816
N_MOD = 9
EPS = 1e-6
KV_COLS = 2 * NSA_KV_HEADS * NSA_HD
IN_SIZES = (NSA_WIDTH, KV_COLS, KV_COLS, KV_COLS, 3 * NSA_HEADS,
            GLA_HEADS * GLA_DK, GLA_HEADS * GLA_DK, GLA_WIDTH, GLA_RANK, GLA_WIDTH)
IN_PAD = (NSA_HEADS * 128, KV_COLS, KV_COLS, KV_COLS, NSA_KV_HEADS * 128,
          GLA_HEADS * GLA_DK, GLA_HEADS * GLA_DK, GLA_WIDTH, 128, GLA_WIDTH)
IN_OFF = tuple(int(v) for v in np.cumsum((0,) + IN_PAD))

LANES = 128
NEG = -1e30
LOG2E = 1.4426950408889634
VMEM_LIMIT = 56 * 1024 * 1024
FFN_TF = FFN_DIM // 2
NSA_SAMPLE_SEQS = 2
GLA_UNROLL = 8


def _cparams(sem):
    return pltpu.CompilerParams(dimension_semantics=sem, vmem_limit_bytes=VMEM_LIMIT)


def _dot(a, b):
    return jnp.dot(a, b, preferred_element_type=F32)


def _dot_nt(a, b):
    return lax.dot_general(a, b, (((1,), (1,)), ((), ())), preferred_element_type=F32)


def _split3(x):
    x1 = x.astype(BF16)
    r = x - x1.astype(F32)
    x2 = r.astype(BF16)
    x3 = (r - x2.astype(F32)).astype(BF16)
    return x1, x2, x3


def _rms_mod(x, g, sc, sh):
    y = x * lax.rsqrt(jnp.mean(x * x, axis=-1, keepdims=True) + EPS) * g
    return y * (1.0 + sc) + sh


def _masked_softmax(lg, mask):
    lg = jnp.where(mask, lg, NEG)
    m = jnp.max(lg, axis=-1, keepdims=True)
    e = jnp.where(mask, jnp.exp(lg - m), 0.0)
    return e / jnp.maximum(jnp.sum(e, axis=-1, keepdims=True), 1e-30)


def _top_k_mask(score, blk, k, axis=-1):
    sel = jnp.zeros_like(score)
    big = float(score.shape[axis])
    for _ in range(k):
        m = jnp.max(score, axis=axis, keepdims=True)
        idx = jnp.min(jnp.where(score == m, blk, big), axis=axis, keepdims=True)
        hit = blk == idx
        sel = jnp.where(hit, 1.0, sel)
        score = jnp.where(hit, -jnp.inf, score)
    return sel


def _ada_kernel(c_ref, w_ref, b_ref, o_ref):
    c = c_ref[...]
    s = c * jax.nn.sigmoid(c)
    o_ref[0] = _dot(s.astype(BF16), w_ref[0].astype(BF16)) + b_ref[0]


def _ada(c_all, w_ada, b_ada):
    depth, d, n = w_ada.shape
    m = c_all.shape[0]
    tn = 1024
    return pl.pallas_call(
        _ada_kernel,
        grid=(depth, n // tn),
        in_specs=[pl.BlockSpec((m, d), lambda l, j: (0, 0)),
                  pl.BlockSpec((1, d, tn), lambda l, j: (l, 0, j)),
                  pl.BlockSpec((1, 1, tn), lambda l, j: (l, 0, j))],
        out_specs=pl.BlockSpec((1, m, tn), lambda l, j: (l, 0, j)),
        out_shape=jax.ShapeDtypeStruct((depth, m, n), F32),
        compiler_params=_cparams(("parallel", "parallel")),
        name="ada",
    )(c_all, w_ada, b_ada.reshape(depth, 1, n))


def _mod_spec(mod, tm, nargs):
    r = mod.shape[1]
    d = mod.shape[2]
    if nargs == 3:
        if r == 1:
            return pl.BlockSpec((1, 1, d), lambda b, i, f: (b, 0, 0))
        return pl.BlockSpec((1, tm, d), lambda b, i, f: (b, i, 0))
    if r == 1:
        return pl.BlockSpec((1, 1, d), lambda b, i: (b, 0, 0))
    return pl.BlockSpec((1, tm, d), lambda b, i: (b, i, 0))


def _ffn_kernel(x_ref, sh_ref, sc_ref, gt_ref, g_ref, wi_ref, wo_ref, fg_ref, *rest, final, mix):
    o_ref = rest[-1]
    x = x_ref[0]
    if mix:
        x = x + rest[0][0] * _mix_out(*rest[1:6])
    h = _rms_mod(x, g_ref[...], sc_ref[0], sh_ref[0]).astype(BF16)
    y = jnp.zeros(x.shape, F32)
    for f in range(FFN_DIM // FFN_TF):
        g = _dot(h, wi_ref[:, f * FFN_TF:(f + 1) * FFN_TF])
        u = _dot(h, wi_ref[:, FFN_DIM + f * FFN_TF:FFN_DIM + (f + 1) * FFN_TF])
        a = (g * jax.nn.sigmoid(g) * u).astype(BF16)
        y = y + _dot(a, wo_ref[f * FFN_TF:(f + 1) * FFN_TF, :])
    x = x + 0.5 * gt_ref[0] * y
    if final:
        x = x * lax.rsqrt(jnp.mean(x * x, axis=-1, keepdims=True) + EPS) * fg_ref[...]
    o_ref[0] = x


def _mix_out(on_ref, ogla_ref, og_ref, gn_ref, w_ref):
    y = _dot(on_ref[0].astype(BF16), w_ref[:NSA_WIDTH, :])
    og = og_ref[0]
    ogla = ogla_ref[0]
    heads = []
    for h in range(GLA_HEADS):
        o = ogla[:, h * GLA_DV:(h + 1) * GLA_DV]
        o = o * lax.rsqrt(jnp.mean(o * o, axis=-1, keepdims=True) + EPS) * gn_ref[...]
        gate = og[:, h * GLA_DV:(h + 1) * GLA_DV]
        heads.append((o * (gate * jax.nn.sigmoid(gate))).astype(BF16))
    return y + _dot(jnp.concatenate(heads, axis=1), w_ref[NSA_WIDTH:, :])


def _ffn(x, sh, sc, gt, g, w_in, w_out, layer, tm, final_gain=None, mixer=None):
    bx, tx, d = x.shape
    xs = pl.BlockSpec((1, tm, d), lambda b, i: (b, i, 0))
    ms = _mod_spec(sh, tm, 2)
    once = lambda a: pl.BlockSpec(a.shape, lambda b, i: (0,) * a.ndim, pipeline_mode=pl.Buffered(1))
    of_layer = lambda a: pl.BlockSpec((None,) + a.shape[1:], lambda b, i: (layer, 0, 0), pipeline_mode=pl.Buffered(1))
    fg = g if final_gain is None else final_gain
    in_specs = [xs, ms, ms, ms, once(g), of_layer(w_in), of_layer(w_out), once(fg)]
    args = [x, sh, sc, gt, g, w_in, w_out, fg]
    if mixer is not None:
        half = pl.BlockSpec((1, tm, NSA_WIDTH), lambda b, i: (b, i, 0))
        in_specs += [ms, half, half, half, once(mixer[4]), once(mixer[5])]
        args += list(mixer)
    return pl.pallas_call(
        functools.partial(_ffn_kernel, final=final_gain is not None, mix=mixer is not None),
        grid=(bx, tx // tm),
        in_specs=in_specs,
        out_specs=xs,
        out_shape=jax.ShapeDtypeStruct(x.shape, F32),
        compiler_params=_cparams(("parallel", "parallel")),
        name="ffn",
    )(*args)


def _proj_kernel(x_ref, sh_ref, sc_ref, g_ref, w_ref, wkvt_ref, wa2_ref, ba_ref, qb_ref, oh_ref, pos_ref, one_ref,
                 qn_ref, kvc_ref, kvs_ref, kvw_ref, kvct_ref, kvst_ref, kvwt_ref, ksa_ref, vsa_ref, kwa_ref, vwa_ref,
                 gn_ref, qg_ref, kg_ref, vg_ref, la_ref, og_ref):
    h = _rms_mod(x_ref[0], g_ref[...], sc_ref[0], sh_ref[0]).astype(BF16)
    p = _dot(h, w_ref[...])
    o = IN_OFF
    qn_ref[0] = (p[:, o[0]:o[1]] * (NSA_HD ** -0.5 * LOG2E) + qb_ref[...]).astype(BF16)
    kvc_ref[0] = p[:, o[1]:o[2]]
    kvs_ref[0] = p[:, o[2]:o[3]]
    kvw_ref[0] = p[:, o[3]:o[4]]
    kvt = _dot_nt(wkvt_ref[...], h)
    kvct_ref[0] = kvt[0:KV_COLS]
    kvst_ref[0] = kvt[KV_COLS:2 * KV_COLS]
    kvwt_ref[0] = kvt[2 * KV_COLS:3 * KV_COLS]
    kb = kvt.astype(BF16)
    oh, pos, one = oh_ref[...], pos_ref[...], one_ref[...]
    hd = NSA_HD
    for src, k_ref, v_ref in ((KV_COLS, ksa_ref, vsa_ref), (2 * KV_COLS, kwa_ref, vwa_ref)):
        k_parts, v_parts = [], []
        for gi in range(NSA_KV_HEADS):
            k_parts += ([oh] if k_ref is ksa_ref else []) + [kb[src + gi * hd:src + (gi + 1) * hd], pos]
            v_parts += [kb[src + (NSA_KV_HEADS + gi) * hd:src + (NSA_KV_HEADS + gi + 1) * hd], one]
        k_ref[0] = jnp.concatenate(k_parts, axis=0)
        v_ref[0] = jnp.concatenate(v_parts, axis=0)
    gn_ref[0] = jax.nn.sigmoid(p[:, o[4]:o[5]])
    qg_ref[0] = p[:, o[5]:o[6]] * (GLA_DK ** -0.5)
    kg_ref[0] = p[:, o[6]:o[7]]
    vg_ref[0] = p[:, o[7]:o[8]]
    a_pre = _dot(p[:, o[8]:o[9]].astype(BF16), wa2_ref[...]) + ba_ref[...]
    log_sig = jnp.minimum(a_pre, 0.0) - jnp.log(1.0 + jnp.exp(-jnp.abs(a_pre)))
    la_ref[0] = log_sig * (1.0 / GLA_TAU)
    og_ref[0] = p[:, o[9]:o[10]]


def _proj(x, sh, sc, g, w_all, wkvt, wa2p, ba, qb, tm):
    bx, tx, d = x.shape
    xs = pl.BlockSpec((1, tm, d), lambda b, i: (b, i, 0))
    ms = _mod_spec(sh, tm, 2)
    full = lambda shape: pl.BlockSpec(shape, lambda b, i: (0,) * len(shape))
    rows = lambda w, dt: (pl.BlockSpec((1, tm, w), lambda b, i: (b, i, 0)), jax.ShapeDtypeStruct((bx, tx, w), dt))
    cols = lambda w, dt: (pl.BlockSpec((1, w, tm), lambda b, i: (b, 0, i)), jax.ShapeDtypeStruct((bx, w, tx), dt))
    pos_np = np.arange(tx)
    nsp = max(LANES, tx // SEL_BLOCK)
    oh = jnp.asarray(np.arange(nsp)[:, None] == pos_np[None, :] // SEL_BLOCK, BF16)
    pos = jnp.pad(_alibi_k_rows(pos_np), ((0, NSA_HD - 4), (0, 0)))
    one = jnp.pad(jnp.ones((1, tx), BF16), ((0, NSA_HD - 1), (0, 0)))
    lanes = lambda a: pl.BlockSpec((a.shape[0], tm), lambda b, i: (0, i))
    g2 = NSA_KV_HEADS
    outs = [rows(IN_PAD[0], BF16), rows(KV_COLS, F32), rows(KV_COLS, F32), rows(KV_COLS, F32),
            cols(KV_COLS, F32), cols(KV_COLS, F32), cols(KV_COLS, F32),
            cols(g2 * (nsp + LANES), BF16), cols(g2 * LANES, BF16), cols(g2 * LANES, BF16), cols(g2 * LANES, BF16),
            rows(IN_PAD[4], F32), rows(256, F32), rows(256, F32), rows(GLA_WIDTH, F32), rows(256, F32),
            rows(GLA_WIDTH, F32)]
    return pl.pallas_call(
        _proj_kernel,
        grid=(bx, tx // tm),
        in_specs=[xs, ms, ms, full((1, d)), full(w_all.shape), full(wkvt.shape), full(wa2p.shape),
                  full(ba.shape), full(qb.shape), lanes(oh), lanes(pos), lanes(one)],
        out_specs=[o[0] for o in outs],
        out_shape=[o[1] for o in outs],
        compiler_params=_cparams(("parallel", "parallel")),
        name="proj",
    )(x, sh, sc, g, w_all, wkvt, wa2p, ba, qb, oh, pos, one)


def _cmp_kernel(x_ref, pelo_ref, pehi_ref, w1lo_ref, w1hi_ref, b1_ref, w2_ref, o_ref):
    x = x_ref[0]
    n = x.shape[0]
    h_lo = _dot_nt(w1lo_ref[...], (x + pelo_ref[...]).astype(BF16))
    h_hi = _dot_nt(w1hi_ref[...], (x + pehi_ref[...]).astype(BF16))
    h = h_lo + pltpu.roll(h_hi, n - 1, 1) + b1_ref[...]
    o_ref[0] = _dot(w2_ref[...], jax.nn.gelu(h).astype(BF16))


def _cmp_prompt(x16, cw):
    b, n, w = x16.shape
    full = lambda a: pl.BlockSpec(a.shape, lambda i: (0,) * a.ndim)
    args = (cw["pe_lo"], cw["pe_hi"], cw["w1lo_t"], cw["w1hi_t"], cw["b1_col"], cw["w2_t"])
    return pl.pallas_call(
        _cmp_kernel,
        grid=(b,),
        in_specs=[pl.BlockSpec((1, n, w), lambda i: (i, 0, 0))] + [full(a) for a in args],
        out_specs=pl.BlockSpec((1, KV_COLS, n), lambda i: (i, 0, 0)),
        out_shape=jax.ShapeDtypeStruct((b, KV_COLS, n), F32),
        compiler_params=_cparams(("parallel",)),
        name="cmp_prompt",
    )(x16, *args)


def _nsa_prompt_kernel(q_ref, gate_ref, ck_ref, cv_ref, ks_ref, vs_ref, kw_ref, vw_ref, ovt_ref,
                       o_ref, ids_scr, *, tq, tk, ncp, nsp):
    i = pl.program_id(2)
    q0 = i * tq
    nr = NSA_GROUP * tq
    row = q0 + jnp.bitwise_and(lax.broadcasted_iota(jnp.int32, (nr, 1), 0), tq - 1)
    row_t = q0 + lax.broadcasted_iota(jnp.int32, (tq, 1), 0)
    qblk = q_ref[0]
    q = jnp.concatenate([qblk[:, r * LANES:(r + 1) * LANES] for r in range(NSA_GROUP)], axis=0)

    def per_head(x, mask):
        return (x.reshape(NSA_GROUP, tq, x.shape[-1]) + mask[None]).reshape(x.shape)

    c_end = lax.broadcasted_iota(jnp.int32, (1, ncp), 1) * CMP_STRIDE + (CMP_BLOCK - 1)
    lg = per_head(_dot(q, ck_ref[0, 0]), jnp.where(c_end <= row_t, 0.0, NEG))
    e = jnp.exp2(lg - jnp.max(lg, axis=-1, keepdims=True))
    pc = e * jnp.where(row >= CMP_BLOCK - 1, 1.0 / jnp.sum(e, axis=-1, keepdims=True), 0.0)
    o_cmp = _dot_nt(pc.astype(BF16), cv_ref[0, 0])
    psum = pc[0:tq]
    for r in range(1, NSA_GROUP):
        psum = psum + pc[r * tq:(r + 1) * tq]
    p_hi = psum.astype(BF16)
    p_lo = (psum - p_hi.astype(F32)).astype(BF16)
    imp_t = _dot_nt(ovt_ref[...], p_hi) + _dot_nt(ovt_ref[...], p_lo)

    wl = WINDOW + tq
    w0 = pl.multiple_of(jnp.maximum(q0 - WINDOW, 0), tq)
    dist_w = row_t - (w0 + lax.broadcasted_iota(jnp.int32, (1, wl), 1))
    neg_w = jnp.where(dist_w >= 0, jnp.where(dist_w < WINDOW, 0.0, NEG), NEG)
    lg = per_head(_dot(q, kw_ref[0, 0, :, pl.ds(w0, wl)]), neg_w)
    p = jnp.exp2(lg - jnp.max(lg, axis=-1, keepdims=True)).astype(BF16)
    ow = _dot_nt(p, vw_ref[0, 0, :, pl.ds(w0, wl)])
    o_win = ow[:, :NSA_HD] / ow[:, NSA_HD:NSA_HD + 1]
    gates = gate_ref[0]
    gcol = lambda j: jnp.concatenate([gates[:, 3 * r + j:3 * r + j + 1] for r in range(NSA_GROUP)], axis=0)
    o_two = gcol(0) * o_cmp + gcol(2) * o_win

    blk = lax.broadcasted_iota(jnp.int32, (nsp, 1), 0)
    t_lane = q0 + lax.broadcasted_iota(jnp.int32, (1, tq), 1)
    cur = jnp.right_shift(t_lane, 6)
    forced = jnp.where(blk == 0, 1.0, 0.0) + jnp.where(blk == cur, 1.0, 0.0) + jnp.where(blk == cur - 1, 1.0, 0.0)
    valid = blk * SEL_BLOCK <= t_lane
    score = jnp.where(valid, imp_t + jnp.where(forced > 0.5, FORCE_BONUS, 0.0), NEG)
    sel_t = _top_k_mask(score, blk.astype(F32), SEL_TOPK, axis=0)
    sel_t = jnp.where(valid, sel_t, 0.0)
    sel_bias = ((sel_t - 1.0) * 1e30).T.astype(BF16)
    qa = jnp.concatenate([jnp.concatenate([sel_bias] * NSA_GROUP, axis=0), q], axis=1)

    n_full = q0 // tk
    per_tile = tk // SEL_BLOCK
    n_tiles = nsp // per_tile
    blk_any = jnp.max(sel_t, axis=1, keepdims=True)
    tile_any = jnp.max(blk_any.reshape(n_tiles, per_tile, 1), axis=1)
    bit = jnp.left_shift(1, lax.broadcasted_iota(jnp.int32, (n_tiles, 1), 0))
    tile_bits = jnp.sum(jnp.where(tile_any > 0.0, bit, 0))
    n_used = jnp.int32(0)
    for kt in range(min(n_tiles, ks_ref.shape[3] // tk)):
        use = jnp.logical_and(jnp.bitwise_and(jnp.right_shift(tile_bits, kt), 1) == 1, kt < n_full)
        ids_scr[n_used] = kt
        n_used = n_used + use.astype(jnp.int32)

    def sel_tile(kt, carry, diag):
        m, acc = carry
        k0 = pl.multiple_of(kt * tk, tk)
        lg = _dot(qa, ks_ref[0, 0, :, pl.ds(k0, tk)])
        if diag:
            lg = jnp.where(k0 + lax.broadcasted_iota(jnp.int32, (1, tk), 1) <= row, lg, NEG)
        m_new = jnp.maximum(m, jnp.max(lg, axis=-1, keepdims=True))
        p = jnp.exp2(lg - m_new).astype(BF16)
        return m_new, jnp.exp2(m - m_new) * acc + _dot_nt(p, vs_ref[0, 0, :, pl.ds(k0, tk)])

    carry = sel_tile(n_full, (jnp.full((nr, 1), NEG, F32), jnp.zeros((nr, LANES), F32)), True)

    def two_tiles(j, c):
        return sel_tile(ids_scr[2 * j + 1], sel_tile(ids_scr[2 * j], c, False), False)

    carry = lax.fori_loop(0, n_used // 2, two_tiles, carry)
    _, acc = lax.cond(n_used % 2 == 1, lambda c: sel_tile(ids_scr[n_used - 1], c, False), lambda c: c, carry)
    o_slc = acc[:, :NSA_HD] / acc[:, NSA_HD:NSA_HD + 1]
    o = o_two + gcol(1) * o_slc
    o_ref[0] = jnp.concatenate([o[r * tq:(r + 1) * tq] for r in range(NSA_GROUP)], axis=1)


def _overlap_matrix(ncp, nsp):
    c = np.arange(ncp)[:, None] * CMP_STRIDE
    s = np.arange(nsp)[None, :] * SEL_BLOCK
    return jnp.asarray((c <= s + SEL_BLOCK - 1) & (c + CMP_BLOCK - 1 >= s), dtype=BF16)


def _bf16_round(x):
    return np.asarray(x, dtype=BF16).astype(np.float32)


def _q_bias():
    slopes = np.exp2(-8.0 * np.arange(1, NSA_HEADS + 1) / NSA_HEADS)
    c_hi = float(_bf16_round(LOG2E))
    c_lo = float(_bf16_round(LOG2E - c_hi))
    qb = np.zeros((NSA_HEADS, LANES), np.float32)
    qb[:, NSA_HD:NSA_HD + 4] = np.stack([c_hi * slopes * SEL_BLOCK, c_lo * slopes * SEL_BLOCK,
                                         c_hi * slopes, c_lo * slopes], axis=1)
    return jnp.asarray(qb.reshape(1, NSA_HEADS * LANES))


def _alibi_k_rows(pos):
    pos = np.asarray(pos)
    return jnp.asarray(np.stack([pos // SEL_BLOCK, pos // SEL_BLOCK, pos % SEL_BLOCK, pos % SEL_BLOCK]), BF16)


def _aug_rows(parts, total, lead):
    parts = [jnp.broadcast_to(p, lead + p.shape[-2:]) for p in parts]
    used = sum(p.shape[-2] for p in parts)
    if total > used:
        parts.append(jnp.zeros(lead + (total - used, parts[0].shape[-1]), BF16))
    return jnp.concatenate(parts, axis=-2)


def _nsa_prompt(q_pad, gates, ck, cv, ks, vs, kw, vw, tq, tk):
    b, t, _ = q_pad.shape
    g, r = NSA_KV_HEADS, NSA_GROUP
    ncp = ck.shape[3]
    nsp = max(LANES, t // SEL_BLOCK)
    kv = lambda a: pl.BlockSpec((1, 1) + a.shape[2:], lambda bi, gi, i: (bi, gi, 0, 0))
    ovt = _overlap_matrix(ncp, nsp).T
    return pl.pallas_call(
        functools.partial(_nsa_prompt_kernel, tq=tq, tk=tk, ncp=ncp, nsp=nsp),
        grid=(b, g, t // tq),
        in_specs=[pl.BlockSpec((1, tq, r * LANES), lambda bi, gi, i: (bi, i, gi)),
                  pl.BlockSpec((1, tq, LANES), lambda bi, gi, i: (bi, i, gi)),
                  kv(ck), kv(cv), kv(ks), kv(vs), kv(kw), kv(vw),
                  pl.BlockSpec((nsp, ncp), lambda bi, gi, i: (0, 0))],
        out_specs=pl.BlockSpec((1, tq, r * NSA_HD), lambda bi, gi, i: (bi, i, gi)),
        out_shape=jax.ShapeDtypeStruct((b, t, NSA_WIDTH), F32),
        scratch_shapes=[pltpu.SMEM((nsp // (tk // SEL_BLOCK),), jnp.int32)],
        compiler_params=_cparams(("parallel", "parallel", "arbitrary")),
        name="nsa_prompt",
    )(q_pad, gates, ck, cv, ks, vs, kw, vw, ovt)


def _gla_prompt_kernel(q_ref, k_ref, la_ref, v_ref, o_ref, s_ref, s_scr, *, nch):
    c = GLA_CHUNK
    ti = pl.program_id(2)

    @pl.when(ti == 0)
    def _():
        s_scr[...] = jnp.zeros_like(s_scr)

    two_dk, two_dv = 2 * GLA_DK, 2 * GLA_DV
    r_i = lax.broadcasted_iota(jnp.int32, (c, c), 0)
    c_i = lax.broadcasted_iota(jnp.int32, (c, c), 1)
    tril = jnp.where(c_i <= r_i, 1.0, 0.0).astype(BF16)
    e_r = lax.broadcasted_iota(jnp.int32, (two_dk, two_dk), 0)
    e_c = lax.broadcasted_iota(jnp.int32, (two_dk, two_dk), 1)
    eye = jnp.where(e_r == e_c, 1.0, 0.0).astype(BF16)
    b_r = lax.broadcasted_iota(jnp.int32, (two_dk, two_dv), 0) // GLA_DK
    b_c = lax.broadcasted_iota(jnp.int32, (two_dk, two_dv), 1) // GLA_DV
    same_head = jnp.where(b_r == b_c, 1.0, 0.0)
    ones_bd = same_head.astype(BF16)
    head0 = lax.broadcasted_iota(jnp.int32, (1, two_dk), 1) < GLA_DK
    pos = lax.broadcasted_iota(jnp.int32, (c, 1), 0)
    nsub = c // GLA_SUB
    w_i = lax.broadcasted_iota(jnp.int32, (1, GLA_SUB, 1), 1)

    def chunk(ci):
        rows = pl.ds(pl.multiple_of(ci * c, c), c)
        q = q_ref[0, rows, :]
        k = k_ref[0, rows, :]
        la = la_ref[0, rows, :]
        v = v_ref[0, rows, :]
        vb = v.astype(BF16)
        s0 = s_scr[...]
        a1, a2, a3 = _split3(la)
        cum = _dot(tril, a1) + _dot(tril, a2) + _dot(tril, a3)
        last = cum[c - 1:c, :]
        o = _dot((q * jnp.exp(cum)).astype(BF16), s0.astype(BF16))
        attn = jnp.zeros((2 * c, c), F32)
        for j in range(nsub - 1):
            ce = cum[GLA_SUB * (j + 1) - 1:GLA_SUB * (j + 1), :]
            qh = jnp.where(pos >= GLA_SUB * (j + 1), q * jnp.exp(jnp.minimum(cum - ce, 0.0)), 0.0)
            in_j = jnp.where(pos >= GLA_SUB * j, jnp.where(pos < GLA_SUB * (j + 1), 1.0, 0.0), 0.0)
            kh = in_j * (k * jnp.exp(jnp.minimum(ce - cum, 0.0)))
            qh2 = jnp.concatenate([jnp.where(head0, qh, 0.0), jnp.where(head0, 0.0, qh)], axis=0)
            attn = attn + _dot_nt(qh2.astype(BF16), kh.astype(BF16))
        o_off = _dot(attn.astype(BF16), vb)
        o = o + jnp.concatenate([o_off[:c, :GLA_DV], o_off[c:, GLA_DV:]], axis=1)
        q3 = q.reshape(nsub, GLA_SUB, two_dk)
        k3 = k.reshape(nsub, GLA_SUB, two_dk)
        c3 = cum.reshape(nsub, GLA_SUB, two_dk)
        v3 = v.reshape(nsub, GLA_SUB, two_dv)
        ws = []
        for u in range(GLA_SUB):
            w = q3 * k3[:, u:u + 1, :] * jnp.exp(jnp.minimum(c3 - c3[:, u:u + 1, :], 0.0))
            ws.append(jnp.where(w_i >= u, w, 0.0).reshape(c, two_dk).astype(BF16))
        a_rep = _dot(jnp.concatenate(ws, axis=0), ones_bd)
        od = jnp.zeros((nsub, GLA_SUB, two_dv), F32)
        for u in range(GLA_SUB):
            od = od + a_rep[u * c:(u + 1) * c].reshape(nsub, GLA_SUB, two_dv) * v3[:, u:u + 1, :]
        o_ref[0, rows, :] = o + od.reshape(c, two_dv)
        kd = (k * jnp.exp(last - cum)).astype(BF16)
        kd_t = _dot_nt(eye, kd).astype(BF16)
        l1, l2, l3 = _split3(jnp.broadcast_to(last, (8, two_dk)))
        last_col = (_dot_nt(eye, l1) + _dot_nt(eye, l2) + _dot_nt(eye, l3))[:, 0:1]
        s_scr[...] = jnp.exp(last_col) * s0 + same_head * _dot(kd_t, vb)

    def several(i, _):
        for j in range(GLA_UNROLL):
            chunk(GLA_UNROLL * i + j)
        return 0

    lax.fori_loop(0, nch // GLA_UNROLL, several, 0)

    @pl.when(ti == pl.num_programs(2) - 1)
    def _():
        s = s_scr[...]
        s_ref[0, 0] = s[:GLA_DK, :GLA_DV]
        s_ref[0, 1] = s[GLA_DK:, GLA_DV:]


def _gla_prompt(q, k, la, v, nch):
    b, t, _ = q.shape
    tc = nch * GLA_CHUNK
    hp = GLA_HEADS // 2
    qs = pl.BlockSpec((1, tc, 2 * GLA_DK), lambda bi, pi, ti: (bi, ti, pi))
    vs = pl.BlockSpec((1, tc, 2 * GLA_DV), lambda bi, pi, ti: (bi, ti, pi))
    return pl.pallas_call(
        functools.partial(_gla_prompt_kernel, nch=nch),
        grid=(b, hp, t // tc),
        in_specs=[qs, qs, qs, vs],
        out_specs=[vs, pl.BlockSpec((1, 2, GLA_DK, GLA_DV), lambda bi, pi, ti: (bi, pi, 0, 0))],
        out_shape=[jax.ShapeDtypeStruct((b, t, GLA_WIDTH), F32),
                   jax.ShapeDtypeStruct((b, GLA_HEADS, GLA_DK, GLA_DV), F32)],
        scratch_shapes=[pltpu.VMEM((2 * GLA_DK, 2 * GLA_DV), F32)],
        compiler_params=_cparams(("parallel", "parallel", "arbitrary")),
        name="gla_prompt",
    )(q, k, la, v)


def _nsa_sample_kernel(pt_ref, *rest, n_pages, page, past, wlen, nseq):
    del pt_ref
    per = 6 + 2 * n_pages
    (w1_ref, pe_ref, b1_ref, w2_ref, ovt_ref, o_ref, x_scr, bias_scr) = rest[nseq * per:]

    @pl.when(pl.program_id(0) == 0)
    def _():
        for kv in range(2):
            acc = jnp.zeros((8, 2 * LANES), F32)
            for s2 in range(CMP_BLOCK // 2):
                pe2 = jnp.broadcast_to(pe_ref[kv, s2:s2 + 1, :], (8, 2 * LANES))
                acc = acc + _dot(pe2.astype(BF16), w1_ref[kv, s2])
            bias_scr[kv] = acc + b1_ref[kv]

    ncb = past // CMP_STRIDE
    nrow = ncb + 8
    cpp = page // CMP_STRIDE
    r_i = lax.broadcasted_iota(jnp.int32, (page, page), 0)
    p_i = lax.broadcasted_iota(jnp.int32, (page, page), 1)
    in_chunk = jnp.bitwise_and(p_i, CMP_STRIDE - 1)
    perm_t = jnp.where(r_i * CMP_STRIDE == in_chunk * (cpp * CMP_STRIDE) + (p_i - in_chunk), 1.0, 0.0).astype(BF16)
    first = lax.broadcasted_iota(jnp.int32, (8, 1), 0) == 0
    seqs = [rest[s * per:(s + 1) * per] for s in range(nseq)]
    for s, refs in enumerate(seqs):
        base = s * nrow
        for kv in range(2):
            for pg in range(n_pages):
                rows = _dot_nt(perm_t, refs[5 + pg][0, 0, kv].reshape(LANES, page).astype(BF16))
                for off in range(CMP_STRIDE):
                    x_scr[kv, off, base + pg * cpp:base + (pg + 1) * cpp, :] = rows[off * cpp:(off + 1) * cpp, :]
            new = refs[2][0][:, kv * LANES:(kv + 1) * LANES]
            for off in range(CMP_STRIDE):
                x_scr[kv, off, base + ncb:base + nrow, :] = (jnp.where(first, new, 0.0) if off == 0
                                                             else jnp.zeros((8, LANES), F32))
    ckvs = [[None, None] for _ in range(nseq)]
    half = CMP_STRIDE // 2
    for kv in range(2):
        acc_lo = jnp.zeros((nseq * nrow, 2 * LANES), F32)
        acc_hi = jnp.zeros((nseq * nrow, 2 * LANES), F32)
        for j in range(half):
            lhs = jnp.concatenate([x_scr[kv, 2 * j], x_scr[kv, 2 * j + 1]], axis=1).astype(BF16)
            acc_lo = acc_lo + _dot(lhs, w1_ref[kv, j])
            acc_hi = acc_hi + _dot(lhs, w1_ref[kv, half + j])
        hs = [jax.nn.gelu(acc_lo[s * nrow:s * nrow + ncb]
                          + pltpu.roll(acc_hi[s * nrow:(s + 1) * nrow], nrow - 1, 0)[:ncb] + bias_scr[kv][0:1])
              for s in range(nseq)]
        ckv = _dot(jnp.concatenate(hs, axis=0).astype(BF16), w2_ref[kv]).astype(BF16)
        for s in range(nseq):
            ckvs[s][kv] = ckv[s * ncb:(s + 1) * ncb]

    for s, refs in enumerate(seqs):
        _nsa_sample_seq(refs[:5], ckvs[s], refs[5 + n_pages:5 + 2 * n_pages], refs[5 + 2 * n_pages],
                        ovt_ref, o_ref.at[s], n_pages=n_pages, page=page, past=past, wlen=wlen)


def _nsa_sample_seq(row_refs, ckv, slc_pages, cw_ref, ovt_ref, o_ref, *, n_pages, page, past, wlen):
    qall_ref, gate_ref, _, kvs_ref, kvw_ref = row_refs
    nh = NSA_HEADS
    qa = qall_ref[0]
    qa_f = qa.astype(F32)
    hrow = lax.broadcasted_iota(jnp.int32, (nh, 1), 0)
    slope = jnp.exp2(-(hrow + 1).astype(F32)) * LOG2E
    lane = lax.broadcasted_iota(jnp.int32, (1, LANES), 1)
    c_end = lane * CMP_STRIDE + (CMP_BLOCK - 1)
    dist_c = past - c_end
    mask_c = dist_c >= 0
    lg_c = jnp.where(mask_c, _dot_nt(qa, ckv[0]) - slope * dist_c.astype(F32), NEG)
    e_c = jnp.where(mask_c, jnp.exp2(lg_c - jnp.max(lg_c, axis=-1, keepdims=True)), 0.0)
    pc = e_c / jnp.maximum(jnp.sum(e_c, axis=-1, keepdims=True), 1e-30)
    o_cmp = _dot(pc.astype(BF16), ckv[1])

    ps0 = jnp.sum(jnp.where(hrow < NSA_GROUP, pc, 0.0), axis=0, keepdims=True)
    ps1 = jnp.sum(jnp.where(hrow >= NSA_GROUP, pc, 0.0), axis=0, keepdims=True)
    psum = jnp.where(hrow == 0, ps0, jnp.where(hrow == 1, ps1, 0.0))
    psum = jnp.concatenate([psum, jnp.zeros((LANES - nh, LANES), F32)], axis=0)
    p_hi = psum.astype(BF16)
    p_lo = (psum - p_hi.astype(F32)).astype(BF16)
    imp_t = _dot_nt(ovt_ref[...], p_hi) + _dot_nt(ovt_ref[...], p_lo)
    blk = lax.broadcasted_iota(jnp.int32, (LANES, 1), 0)
    cur = past // SEL_BLOCK
    forced = jnp.where(blk == 0, 1.0, 0.0) + jnp.where(blk == cur, 1.0, 0.0) + jnp.where(blk == cur - 1, 1.0, 0.0)
    score = jnp.where(blk * SEL_BLOCK <= past, imp_t + jnp.where(forced > 0.5, FORCE_BONUS, 0.0), NEG)
    sel = _top_k_mask(score, blk.astype(F32), SEL_TOPK, axis=0).T
    sel_h = jnp.where(hrow < NSA_GROUP, sel[0:1], sel[1:2])

    def attend(tiles, new_row, new_neg):
        lgs = [_dot(qa, k_t) + neg for k_t, _, neg in tiles]
        lg_new = jnp.sum(qa_f * new_row[:, :LANES], axis=-1, keepdims=True) + new_neg
        m = lg_new
        for lg in lgs:
            m = jnp.maximum(m, jnp.max(lg, axis=-1, keepdims=True))
        p_new = jnp.exp2(lg_new - m)
        l = p_new
        acc = p_new * new_row[:, LANES:]
        for lg, (_, v_t, _) in zip(lgs, tiles):
            p = jnp.exp2(lg - m)
            l = l + jnp.sum(p, axis=-1, keepdims=True)
            acc = acc + _dot_nt(p.astype(BF16), v_t)
        return acc / l

    per = page // SEL_BLOCK
    tiles = []
    for pg in range(n_pages):
        blkp = slc_pages[pg]
        dist = past - (pg * page + lane)
        selx = jnp.zeros((nh, LANES), F32)
        for u in range(per):
            in_u = jnp.where(lane >= u * SEL_BLOCK, jnp.where(lane < (u + 1) * SEL_BLOCK, 1.0, 0.0), 0.0)
            selx = selx + in_u * sel_h[:, pg * per + u:pg * per + u + 1]
        neg = jnp.where(dist >= 0, (selx - 1.0) * 1e30, NEG) - slope * dist.astype(F32)
        tiles.append((blkp[0, 0, 0].reshape(LANES, page).astype(BF16),
                      blkp[0, 0, 1].reshape(LANES, page).astype(BF16), neg))
    o_slc = attend(tiles, kvs_ref[0], (sel_h[:, cur:cur + 1] - 1.0) * 1e30)

    wlane = lax.broadcasted_iota(jnp.int32, (1, wlen), 1)
    w_pos = past - wlen + wlane
    dist = past - w_pos
    ok = jnp.where(dist >= 0, jnp.where(dist < WINDOW, jnp.where(w_pos >= 0, 1.0, 0.0), 0.0), 0.0)
    neg = (ok - 1.0) * 1e30 - slope * dist.astype(F32)
    tiles = [(cw_ref[0, 0, 0].reshape(LANES, wlen).astype(BF16), cw_ref[0, 0, 1].reshape(LANES, wlen).astype(BF16), neg)]
    o_win = attend(tiles, kvw_ref[0], jnp.zeros((nh, 1), F32))

    gates = gate_ref[0]
    pick = lambda a: jnp.where(hrow < NSA_GROUP, a[:, :NSA_HD], a[:, NSA_HD:])
    o_ref[...] = gates[:, 0:1] * pick(o_cmp) + gates[:, 1:2] * pick(o_slc) + gates[:, 2:3] * pick(o_win)


def _nsa_sample(layer, page_table, q_all, gates, kvc, kvs, kvw, cache_c, cache_s, cache_w, cw):
    nb, n_pages = page_table.shape
    depth, n_phys, page = cache_c.shape[:3]
    past = n_pages * page
    wlen = cache_w.shape[2]
    ncb = past // CMP_STRIDE
    cc = cache_c.transpose(0, 1, 3, 4, 5, 2)
    cs = cache_s.transpose(0, 1, 3, 4, 5, 2)
    cwin = cache_w.transpose(0, 1, 3, 4, 5, 2)
    nseq = NSA_SAMPLE_SEQS if nb % NSA_SAMPLE_SEQS == 0 else 1
    full = lambda a: pl.BlockSpec(a.shape, lambda b, pt: (0,) * a.ndim)
    ovt = _overlap_matrix(ncb, LANES).T
    wargs = (cw["w1s"], cw["pes"], cw["b1s"], cw["w2s"], ovt)
    in_specs, args = [], []
    for s in range(nseq):
        seq = lambda b, s=s: nseq * b + s
        heads = pl.BlockSpec((1, NSA_HEADS, LANES), lambda b, pt, seq=seq: (seq(b), 0, 0))
        row = pl.BlockSpec((1, 1, KV_COLS), lambda b, pt, seq=seq: (seq(b), 0, 0))
        pages = [pl.BlockSpec((1, 1, 2, NSA_KV_HEADS, NSA_HD, page),
                              lambda b, pt, seq=seq, j=j: (layer, pt[seq(b), j], 0, 0, 0, 0)) for j in range(n_pages)]
        win = pl.BlockSpec((1, 1, 2, NSA_KV_HEADS, NSA_HD, wlen), lambda b, pt, seq=seq: (layer, seq(b), 0, 0, 0, 0))
        in_specs += [heads, heads, row, row, row] + pages + pages + [win]
        args += [q_all, gates, kvc, kvs, kvw] + [cc] * n_pages + [cs] * n_pages + [cwin]
    grid_spec = pltpu.PrefetchScalarGridSpec(
        num_scalar_prefetch=1,
        grid=(nb // nseq,),
        in_specs=in_specs + [full(a) for a in wargs],
        out_specs=pl.BlockSpec((nseq, NSA_HEADS, NSA_HD), lambda b, pt: (b, 0, 0)),
        scratch_shapes=[pltpu.VMEM((2, CMP_STRIDE, nseq * (ncb + 8), LANES), F32), pltpu.VMEM((2, 8, 2 * LANES), F32)],
    )
    return pl.pallas_call(
        functools.partial(_nsa_sample_kernel, n_pages=n_pages, page=page, past=past, wlen=wlen, nseq=nseq),
        grid_spec=grid_spec,
        out_shape=jax.ShapeDtypeStruct((nb, NSA_HEADS, NSA_HD), F32),
        compiler_params=_cparams(("arbitrary",)),
        name="nsa_sample",
    )(page_table, *args, *wargs)


def _gla_sample_kernel(col_ref, v_ref, s_ref, o_ref, sn_ref, *, bb):
    for b in range(bb):
        cols = col_ref[b]
        for h in range(GLA_HEADS):
            a = jnp.exp(cols[:, h:h + 1])
            k = cols[:, GLA_HEADS + h:GLA_HEADS + h + 1]
            q = cols[:, 2 * GLA_HEADS + h:2 * GLA_HEADS + h + 1]
            v = v_ref[b, h:h + 1, :]
            s0 = s_ref[b, h]
            qk = jnp.sum(q * k, axis=0, keepdims=True)
            o_ref[b, h:h + 1, :] = jnp.sum((q * a) * s0, axis=0, keepdims=True) + qk * v
            sn_ref[b, h] = a * s0 + k * v


def _gla_sample(cols, v, state, bb):
    nb = v.shape[0]
    return pl.pallas_call(
        functools.partial(_gla_sample_kernel, bb=bb),
        grid=(nb // bb,),
        in_specs=[pl.BlockSpec((bb, GLA_DK, 16), lambda i: (i, 0, 0)),
                  pl.BlockSpec((bb, GLA_HEADS, GLA_DV), lambda i: (i, 0, 0)),
                  pl.BlockSpec((bb, GLA_HEADS, GLA_DK, GLA_DV), lambda i: (i, 0, 0, 0))],
        out_specs=[pl.BlockSpec((bb, GLA_HEADS, GLA_DV), lambda i: (i, 0, 0)),
                   pl.BlockSpec((bb, GLA_HEADS, GLA_DK, GLA_DV), lambda i: (i, 0, 0, 0))],
        out_shape=[jax.ShapeDtypeStruct((nb, GLA_HEADS, GLA_DV), F32),
                   jax.ShapeDtypeStruct((nb, GLA_HEADS, GLA_DK, GLA_DV), F32)],
        compiler_params=_cparams(("parallel",)),
        name="gla_sample",
    )(cols, v, state)


def _proj_weight(w):
    d = w.shape[0]
    offs = np.cumsum((0,) + IN_SIZES)
    seg = lambda i: w[:, offs[i]:offs[i + 1]]

    def slots(a, n):
        per = a.shape[1] // n
        return jnp.pad(a.reshape(d, n, per), ((0, 0), (0, 0), (0, LANES - per))).reshape(d, n * LANES)

    return jnp.concatenate([slots(seg(0), NSA_HEADS), seg(1), seg(2), seg(3), slots(seg(4), NSA_KV_HEADS),
                            seg(5), seg(6), seg(7), slots(seg(8), 1), seg(9)], axis=1)


def _cmp_weights(pe, w1, b1, w2):
    g = NSA_KV_HEADS
    eye = jnp.eye(2, dtype=F32)
    eg = jnp.eye(g, dtype=F32)
    w1r = w1.reshape(2, 2, CMP_STRIDE, NSA_HD, CMP_HIDDEN)
    big = jnp.einsum("krsdh,ka,gb->rskgdabh", w1r, eye, eg).reshape(2, CMP_STRIDE * KV_COLS, 2 * g * CMP_HIDDEN)
    w2big = jnp.einsum("khd,ka,gb->kghabd", w2, eye, eg).reshape(2 * g * CMP_HIDDEN, KV_COLS)
    per = pe.reshape(2, 2, CMP_STRIDE, NSA_HD)
    pet = jnp.broadcast_to(per.transpose(1, 2, 0, 3)[:, :, :, None, :], (2, CMP_STRIDE, 2, g, NSA_HD))
    pet = pet.reshape(2, 1, CMP_STRIDE * KV_COLS)
    b1big = jnp.broadcast_to(b1[:, None, :], (2, g, CMP_HIDDEN)).reshape(1, 2 * g * CMP_HIDDEN)
    w1s = jnp.einsum("ksdh,gb->ksgdbh", w1.reshape(2, CMP_BLOCK, NSA_HD, CMP_HIDDEN), eg)
    w1s = w1s.reshape(2, CMP_BLOCK // 2, 2 * g * NSA_HD, g * CMP_HIDDEN)
    w2s = jnp.einsum("khd,gb->kghbd", w2, eg).reshape(2, g * CMP_HIDDEN, g * NSA_HD)
    pes = jnp.broadcast_to(pe[:, :, None, :], (2, CMP_BLOCK, g, NSA_HD)).reshape(2, CMP_BLOCK // 2, 2 * g * NSA_HD)
    b1s = jnp.broadcast_to(b1[:, None, None, :], (2, 1, g, CMP_HIDDEN)).reshape(2, 1, g * CMP_HIDDEN)
    return {
        "w1s": w1s.astype(BF16), "w2s": w2s.astype(BF16), "pes": pes, "b1s": b1s,
        "w1lo_t": big[0].T.astype(BF16), "w1hi_t": big[1].T.astype(BF16),
        "pe_lo": pet[0], "pe_hi": pet[1], "b1_col": b1big.T, "w2_t": w2big.T.astype(BF16),
    }


def _mods(mod, rows):
    m = mod.reshape(mod.shape[0], N_MOD, D_MODEL)
    if rows == 1:
        return [m[:, j][:, None, :] for j in range(N_MOD)]
    return [m[:, j][None, :, :] for j in range(N_MOD)]


def _layer_prompt(x, mods, lw, tm, tq, tk, final_gain):
    b, t, _ = x.shape
    x = _ffn(x, mods[0], mods[1], mods[2], lw["norm_g"][0:1], lw["f1_in"], lw["f1_out"], lw["layer"], tm)
    (qn, kvc, _, _, kvct, kvst, kvwt, ks, vs, kw, vw, gn, qg, kg, vg, la, og) = _proj(
        x, mods[3], mods[4], lw["norm_g"][1:2], lw["w_all"], lw["wkvt"], lw["wa2p"], lw["ba"], lw["qb"], tm)
    g = NSA_KV_HEADS
    ncp = t // CMP_STRIDE
    ckvt = _cmp_prompt(kvc.reshape(b, ncp, CMP_STRIDE * KV_COLS), lw["cmp"]).astype(BF16)
    ckvt = ckvt.reshape(b, 2, g, NSA_HD, ncp)
    ck = _aug_rows([ckvt[:, 0], _alibi_k_rows(np.arange(ncp) * CMP_STRIDE + CMP_BLOCK - 1)], LANES, (b, g))
    per_group = lambda a: a.reshape(b, g, a.shape[1] // g, t)
    o_nsa = _nsa_prompt(qn, gn, ck, ckvt[:, 1], per_group(ks), per_group(vs), per_group(kw), per_group(vw), tq, tk)
    o_gla, s_fin = _gla_prompt(qg, kg, la, vg, min(8, t // GLA_CHUNK))
    x = _ffn(x, mods[6], mods[7], mods[8], lw["norm_g"][2:3], lw["f2_in"], lw["f2_out"], lw["layer"], tm, final_gain,
             mixer=(mods[5], o_nsa, o_gla, og, lw["gla_norm"], lw["w_out"]))
    win = min(WINDOW, t)
    back = lambda a: a.reshape(b, 2, NSA_KV_HEADS, NSA_HD, a.shape[-1]).transpose(0, 4, 1, 2, 3)
    return x, (back(kvct), back(kvst), back(kvwt[:, :, t - win:]), s_fin)


def _layer_sample(layer, x, mods, lw, cache_c, cache_s, cache_w, state, page_table, final_gain):
    nb = x.shape[1]
    x = _ffn(x, mods[0], mods[1], mods[2], lw["norm_g"][0:1], lw["f1_in"], lw["f1_out"], lw["layer"], nb)
    (qn, kvc, kvs, kvw, kvct, kvst, kvwt, _, _, _, _, gn, qg, kg, vg, la, og) = _proj(
        x, mods[3], mods[4], lw["norm_g"][1:2], lw["w_all"], lw["wkvt"], lw["wa2p"], lw["ba"], lw["qb"], nb)
    q4 = qn[0].reshape(nb, NSA_KV_HEADS, NSA_GROUP, LANES)[..., :NSA_HD]
    q_all = (q4[:, :, :, None, :] * jnp.eye(NSA_KV_HEADS, dtype=BF16)[None, :, None, :, None]).reshape(nb, NSA_HEADS, LANES)
    gates = gn[0].reshape(nb, NSA_KV_HEADS, LANES)[..., :3 * NSA_GROUP].reshape(nb, NSA_HEADS, 3)
    gates = jnp.pad(gates, ((0, 0), (0, 0), (0, LANES - 3)))
    r3 = lambda a: a[0][:, None, :]
    o_nsa = _nsa_sample(layer, page_table, q_all, gates, r3(kvc), r3(kvs), r3(kvw),
                        cache_c, cache_s, cache_w, lw["cmp"])
    o_nsa = o_nsa.reshape(1, nb, NSA_WIDTH)
    col = lambda a: a[0].reshape(nb, GLA_HEADS, GLA_DK).transpose(0, 2, 1)
    cols = jnp.concatenate([col(la), col(kg), col(qg), jnp.zeros((nb, GLA_DK, 16 - 3 * GLA_HEADS), F32)], axis=-1)
    o_gla, s_new = _gla_sample(cols, vg[0].reshape(nb, GLA_HEADS, GLA_DV), state, 8)
    o_gla = o_gla.reshape(1, nb, GLA_WIDTH)
    x = _ffn(x, mods[6], mods[7], mods[8], lw["norm_g"][2:3], lw["f2_in"], lw["f2_out"], lw["layer"], nb, final_gain,
             mixer=(mods[5], o_nsa, o_gla, og, lw["gla_norm"], lw["w_out"]))
    kv5 = lambda a: a[0].reshape(2, NSA_KV_HEADS, NSA_HD, nb).transpose(3, 0, 1, 2)[:, None]
    return x, (kv5(kvct), kv5(kvst), kv5(kvwt), s_new)


def kernel(x_prompt, x_sample, c_prompt, c_sample, cache_kv_cmp, cache_kv_slc, cache_kv_win, state_gla, page_table, w_ada, b_ada, norm_g, ffn1_w_in, ffn1_w_out, w_in, cmp_pe, cmp_w1, cmp_b1, cmp_w2, gla_wa2, gla_ba, gla_norm, w_out, ffn2_w_in, ffn2_w_out, final_norm):
    depth = w_ada.shape[0]
    b, t, d = x_prompt.shape
    nb = x_sample.shape[0]
    bp = -(-b // 8) * 8
    c_all = jnp.concatenate([c_prompt, jnp.zeros((bp - b, d), F32), c_sample], axis=0)
    mod_all = _ada(c_all, w_ada, b_ada)

    tm = min(512, t)
    tq = min(256, t)
    tk = min(512, t)
    xp = x_prompt
    xs = x_sample.reshape(1, nb, d)
    outs_p, outs_s = [], []
    f1_in, f1_out = ffn1_w_in.astype(BF16), ffn1_w_out.astype(BF16)
    f2_in, f2_out = ffn2_w_in.astype(BF16), ffn2_w_out.astype(BF16)
    for l in range(depth):
        lw = {
            "norm_g": norm_g[l],
            "layer": l, "f1_in": f1_in, "f1_out": f1_out, "f2_in": f2_in, "f2_out": f2_out,
            "w_all": _proj_weight(w_in[l]).astype(BF16),
            "qb": _q_bias(),
            "wkvt": w_in[l][:, NSA_WIDTH:NSA_WIDTH + 3 * KV_COLS].T.astype(BF16),
            "wa2p": jnp.pad(gla_wa2[l], ((0, 128 - GLA_RANK), (0, 0))).astype(BF16),
            "ba": gla_ba[l][None, :],
            "gla_norm": gla_norm[l][None, :],
            "w_out": w_out[l].astype(BF16),
            "cmp": _cmp_weights(cmp_pe[l], cmp_w1[l], cmp_b1[l], cmp_w2[l]),
        }
        fg = final_norm[None, :] if l == depth - 1 else None
        xp, st_p = _layer_prompt(xp, _mods(mod_all[l, :b], 1), lw, tm, tq, tk, fg)
        xs, st_s = _layer_sample(l, xs, _mods(mod_all[l, bp:], nb), lw, cache_kv_cmp, cache_kv_slc,
                                 cache_kv_win, state_gla[l], page_table, fg)
        outs_p.append(st_p)
        outs_s.append(st_s)
    stack = lambda outs, j: jnp.stack([o[j] for o in outs])
    return (xp, xs.reshape(nb, 1, d),
            stack(outs_p, 0), stack(outs_p, 1), stack(outs_p, 2), stack(outs_p, 3),
            stack(outs_s, 0), stack(outs_s, 1), stack(outs_s, 2), stack(outs_s, 3))
```
